```python
import jax, jax.numpy as jnp
from jax import lax
import numpy as np

D_MODEL = 1024
BATCH = 8
SEQ = 4096
DEPTH = 4

MEM_LEN = 256
MIX_W = D_MODEL // 2
N_BRANCH = 4
CONV_W = MIX_W
CONV_K = 31
DN_HEADS = 4
DN_HEAD_DIM = MIX_W // DN_HEADS
DN_CONV_K = 4
DN_CHUNK = 64
GM_W = MIX_W
GM_GROUPS = 4
GM_CHUNK = 128
POOL_W = MIX_W
POOL_WINDOWS = (2, 4, 8, 16)
POOL_GROUPS = len(POOL_WINDOWS)
XA_HEADS = 4
XA_HEAD_DIM = D_MODEL // XA_HEADS
FFN_W = 4 * D_MODEL
IN_W = 2 * CONV_W + 4 * MIX_W + 2 * DN_HEADS + 2 * GM_W + POOL_W + N_BRANCH * D_MODEL

kernel_name = 'hybrid_gated_conv_deltanet_gmlp_pool_trunk'


def rmsnorm(x, g, eps=1e-6):
    xf = x.astype(jnp.float32)
    y = xf * lax.rsqrt(jnp.mean(xf * xf, axis=-1, keepdims=True) + eps)
    return (y * g.astype(jnp.float32)).astype(x.dtype)


def layernorm(x, g, b, eps=1e-5):
    xf = x.astype(jnp.float32)
    mu = jnp.mean(xf, axis=-1, keepdims=True)
    var = jnp.mean(jnp.square(xf - mu), axis=-1, keepdims=True)
    y = (xf - mu) * lax.rsqrt(var + eps)
    return (y * g.astype(jnp.float32) + b.astype(jnp.float32)).astype(x.dtype)


def causal_dwconv(x, w):
    K, C = w.shape
    return lax.conv_general_dilated(x, w[:, None, :].astype(x.dtype), window_strides=(1,),
                                    padding=[(K - 1, 0)],
                                    dimension_numbers=('NWC', 'WIO', 'NWC'),
                                    feature_group_count=C)


def gated_delta_rule(q, k, v, g, beta):
    B, S, H, Dk = q.shape
    Dv = v.shape[-1]
    C = DN_CHUNK
    N = S // C
    f32 = jnp.float32
    q = q.astype(f32)
    k = k.astype(f32)
    q = q * lax.rsqrt(jnp.sum(q * q, -1, keepdims=True) + 1e-6) * (Dk ** -0.5)
    k = k * lax.rsqrt(jnp.sum(k * k, -1, keepdims=True) + 1e-6)

    def chunks(t):
        t = t.astype(f32).reshape((B, N, C, H) + t.shape[3:])
        return jnp.moveaxis(t, 3, 2)

    q, k, v, g, beta = chunks(q), chunks(k), chunks(v), chunks(g), chunks(beta)
    gam = jnp.cumsum(g, axis=-1)
    causal = jnp.tril(jnp.ones((C, C), bool))
    strict = jnp.tril(jnp.ones((C, C), bool), -1)
    decay = jnp.exp(jnp.where(causal, gam[..., :, None] - gam[..., None, :], -jnp.inf))
    kb = k * beta[..., None]
    a = jnp.where(strict, jnp.einsum('bnhid,bnhjd->bnhij', kb, k) * decay, 0.0)
    rhs = jnp.concatenate([v * beta[..., None], kb * jnp.exp(gam)[..., None]], axis=-1)
    sol = lax.linalg.triangular_solve(a + jnp.eye(C, dtype=f32), rhs, left_side=True,
                                      lower=True, unit_diagonal=True)
    u, w = sol[..., :Dv], sol[..., Dv:]
    attn = jnp.einsum('bnhid,bnhjd->bnhij', q, k) * decay
    q_dec = q * jnp.exp(gam)[..., None]
    g_last = gam[..., -1]
    k_dec = k * jnp.exp(g_last[..., None] - gam)[..., None]

    def step(state, xs):
        u_n, w_n, attn_n, qd_n, kd_n, gl_n = xs
        v_new = u_n - jnp.einsum('bhck,bhkv->bhcv', w_n, state)
        o_n = (jnp.einsum('bhck,bhkv->bhcv', qd_n, state)
               + jnp.einsum('bhij,bhjv->bhiv', attn_n, v_new))
        state = (state * jnp.exp(gl_n)[..., None, None]
                 + jnp.einsum('bhck,bhcv->bhkv', kd_n, v_new))
        return state, o_n

    xs = tuple(jnp.moveaxis(t, 1, 0) for t in (u, w, attn, q_dec, k_dec, g_last))
    state0 = jnp.zeros((B, H, Dk, Dv), f32)
    _, o = lax.scan(step, state0, xs)
    return jnp.transpose(o, (1, 0, 3, 2, 4)).reshape(B, S, H, Dv)


def multiscale_pool(x):
    B, S, C = x.shape
    Cg = C // POOL_GROUPS
    xf = x.astype(jnp.float32).reshape(B, S, POOL_GROUPS, Cg)
    cs = jnp.cumsum(xf, axis=1)
    cs_pad = jnp.concatenate([jnp.zeros((B, 1, POOL_GROUPS, Cg), jnp.float32), cs], axis=1)
    t = jnp.arange(S)
    outs = []
    for gi, win in enumerate(POOL_WINDOWS):
        upper = cs[:, :, gi]
        lower = jnp.concatenate([jnp.zeros((B, win - 1, Cg), jnp.float32),
                                 cs_pad[:, :S - win + 1, gi]], axis=1)
        count = jnp.minimum(t + 1, win).astype(jnp.float32)[None, :, None]
        outs.append((upper - lower) / count)
    return (jnp.stack(outs, axis=2) - xf).astype(x.dtype)


def hybrid_mixer(h, w_in, conv_a_w, conv_a_b, ln_a_g, ln_a_b, dn_conv_w, dn_a_log, dn_dt_bias,
                 dn_norm_g, gm_ln_g, gm_ln_b, gm_ws, gm_bs, pool_w, pool_scale, w_branch, w_out):
    B, S, _ = h.shape
    f32 = jnp.float32
    z = h @ w_in
    sizes = [2 * CONV_W, MIX_W, MIX_W, MIX_W, MIX_W, DN_HEADS, DN_HEADS, 2 * GM_W, POOL_W,
             N_BRANCH * D_MODEL]
    idx = np.cumsum(sizes)[:-1].tolist()
    a_in, dq, dk, dv, dgate, dbeta, da, gm_in, pool_in, gate_in = jnp.split(z, idx, axis=-1)

    a1, a2 = jnp.split(a_in, 2, axis=-1)
    a = a1 * jax.nn.sigmoid(a2)
    a = causal_dwconv(a, conv_a_w) + conv_a_b
    a = jax.nn.silu(layernorm(a, ln_a_g, ln_a_b))

    qkv = jax.nn.silu(causal_dwconv(jnp.concatenate([dq, dk, dv], axis=-1), dn_conv_w))
    q, k, v = jnp.split(qkv, 3, axis=-1)
    heads = lambda t: t.reshape(B, S, DN_HEADS, DN_HEAD_DIM)
    beta = jax.nn.sigmoid(dbeta.astype(f32))
    g = -jnp.exp(dn_a_log.astype(f32)) * jax.nn.softplus(da.astype(f32) + dn_dt_bias.astype(f32))
    o = gated_delta_rule(heads(q), heads(k), heads(v), g, beta)
    o = rmsnorm(o, dn_norm_g) * jax.nn.silu(heads(dgate).astype(f32))
    o = o.reshape(B, S, MIX_W).astype(h.dtype)

    u, vg = jnp.split(jax.nn.gelu(gm_in, approximate=False), 2, axis=-1)
    vg = layernorm(vg, gm_ln_g, gm_ln_b)
    vc = vg.reshape(B, S // GM_CHUNK, GM_CHUNK, GM_GROUPS, GM_W // GM_GROUPS)
    ws = gm_ws * jnp.tril(jnp.ones((GM_CHUNK, GM_CHUNK), gm_ws.dtype))
    mixed = jnp.einsum('gts,bnsgc->bntgc', ws, vc) + gm_bs.T[:, :, None]
    c = u * mixed.reshape(B, S, GM_W)

    p = multiscale_pool(pool_in)
    p = jnp.einsum('bsgc,gcd->bsgd', p, pool_w).reshape(B, S, POOL_W) * pool_scale

    gates = jax.nn.sigmoid(gate_in).reshape(B, S, N_BRANCH, D_MODEL)
    br = jnp.stack([a, o, c, p], axis=2)
    proj = jnp.einsum('bsnc,ncd->bsnd', br, w_branch)
    merged = jnp.sum(gates * proj, axis=2)
    return merged @ w_out


def memory_cross_attention(h, mem, norm_mem, wq, wkv, wo):
    B, S, _ = h.shape
    M = mem.shape[1]
    m = rmsnorm(mem, norm_mem)
    q = (h @ wq).reshape(B, S, XA_HEADS, XA_HEAD_DIM)
    k, v = jnp.split(m @ wkv, 2, axis=-1)
    k = k.reshape(B, M, XA_HEADS, XA_HEAD_DIM)
    v = v.reshape(B, M, XA_HEADS, XA_HEAD_DIM)
    s = jnp.einsum('bshd,bmhd->bhsm', q, k).astype(jnp.float32) * (XA_HEAD_DIM ** -0.5)
    pr = jax.nn.softmax(s, axis=-1).astype(v.dtype)
    o = jnp.einsum('bhsm,bmhd->bshd', pr, v).reshape(B, S, D_MODEL)
    return o @ wo


def squared_relu_mlp(h, w1, w2):
    return jnp.square(jax.nn.relu(h @ w1)) @ w2


def _fwd_setup_inputs(seed: int = 0) -> dict:
    key = jax.random.key(seed)
    ks = jax.random.split(key, 40)
    f32 = jnp.float32
    L, D = DEPTH, D_MODEL
    nrm = lambda i, shape, scale: jax.random.normal(ks[i], shape, f32) * scale
    gain = lambda i, shape: 1.0 + 0.1 * jax.random.normal(ks[i], shape, f32)
    dt = jnp.exp(jax.random.uniform(ks[10], (L, DN_HEADS), f32) * (np.log(0.1) - np.log(0.001))
                 + np.log(0.001))
    return {
        'x': nrm(0, (BATCH, SEQ, D), 1.0),
        'mem': nrm(1, (BATCH, MEM_LEN, D), 1.0),
        'norm_mix': gain(2, (L, D)),
        'w_in': nrm(3, (L, D, IN_W), D ** -0.5),
        'conv_a_w': nrm(4, (L, CONV_K, CONV_W), CONV_K ** -0.5),
        'conv_a_b': nrm(5, (L, CONV_W), 0.02),
        'ln_a_g': gain(6, (L, CONV_W)),
        'ln_a_b': nrm(7, (L, CONV_W), 0.02),
        'dn_conv_w': nrm(8, (L, DN_CONV_K, 3 * MIX_W), DN_CONV_K ** -0.5),
        'dn_a_log': jnp.log(jax.random.uniform(ks[9], (L, DN_HEADS), f32, 1.0, 16.0)),
        'dn_dt_bias': dt + jnp.log(-jnp.expm1(-dt)),
        'dn_norm_g': gain(11, (L, DN_HEAD_DIM)),
        'gm_ln_g': gain(12, (L, GM_W)),
        'gm_ln_b': nrm(13, (L, GM_W), 0.02),
        'gm_ws': nrm(14, (L, GM_GROUPS, GM_CHUNK, GM_CHUNK), GM_CHUNK ** -0.5),
        'gm_bs': gain(15, (L, GM_GROUPS, GM_CHUNK)),
        'pool_w': nrm(16, (L, POOL_GROUPS, POOL_W // POOL_GROUPS, POOL_W // POOL_GROUPS),
                      (POOL_W // POOL_GROUPS) ** -0.5),
        'pool_scale': gain(17, (L, POOL_W)),
        'w_branch': nrm(18, (L, N_BRANCH, MIX_W, D), MIX_W ** -0.5),
        'w_out': nrm(19, (L, D, D), D ** -0.5),
        'norm_xa': gain(20, (L, D)),
        'norm_mem': gain(21, (L, D)),
        'xa_wq': nrm(22, (L, D, D), D ** -0.5),
        'xa_wkv': nrm(23, (L, D, 2 * D), D ** -0.5),
        'xa_wo': nrm(24, (L, D, D), D ** -0.5),
        'norm_mlp': gain(25, (L, D)),
        'mlp_w1': nrm(26, (L, D, FFN_W), D ** -0.5),
        'mlp_w2': nrm(27, (L, FFN_W, D), FFN_W ** -0.5),
        'norm_f': gain(28, (D,)),
    }


def _fwd_reference(x, mem, norm_mix, w_in, conv_a_w, conv_a_b, ln_a_g, ln_a_b, dn_conv_w, dn_a_log,
              dn_dt_bias, dn_norm_g, gm_ln_g, gm_ln_b, gm_ws, gm_bs, pool_w, pool_scale, w_branch,
              w_out, norm_xa, norm_mem, xa_wq, xa_wkv, xa_wo, norm_mlp, mlp_w1, mlp_w2, norm_f):
    for l in range(DEPTH):
        x = x + hybrid_mixer(rmsnorm(x, norm_mix[l]), w_in[l], conv_a_w[l], conv_a_b[l], ln_a_g[l],
                             ln_a_b[l], dn_conv_w[l], dn_a_log[l], dn_dt_bias[l], dn_norm_g[l],
                             gm_ln_g[l], gm_ln_b[l], gm_ws[l], gm_bs[l], pool_w[l], pool_scale[l],
                             w_branch[l], w_out[l])
        x = x + memory_cross_attention(rmsnorm(x, norm_xa[l]), mem, norm_mem[l], xa_wq[l],
                                       xa_wkv[l], xa_wo[l])
        x = x + squared_relu_mlp(rmsnorm(x, norm_mlp[l]), mlp_w1[l], mlp_w2[l])
    return rmsnorm(x, norm_f)


import jax as _jax
import jax.numpy as _jnp

TWIN_FORMAT = 'train_step'
FWD_PARAMS = ['x', 'mem', 'norm_mix', 'w_in', 'conv_a_w', 'conv_a_b', 'ln_a_g', 'ln_a_b', 'dn_conv_w', 'dn_a_log', 'dn_dt_bias', 'dn_norm_g', 'gm_ln_g', 'gm_ln_b', 'gm_ws', 'gm_bs', 'pool_w', 'pool_scale', 'w_branch', 'w_out', 'norm_xa', 'norm_mem', 'xa_wq', 'xa_wkv', 'xa_wo', 'norm_mlp', 'mlp_w1', 'mlp_w2', 'norm_f']
TWIN_WEIGHTS = ['norm_mix', 'w_in', 'conv_a_w', 'conv_a_b', 'ln_a_g', 'ln_a_b', 'dn_conv_w', 'dn_a_log', 'dn_dt_bias', 'dn_norm_g', 'gm_ln_g', 'gm_ln_b', 'gm_ws', 'gm_bs', 'pool_w', 'pool_scale', 'w_branch', 'w_out', 'norm_xa', 'norm_mem', 'xa_wq', 'xa_wkv', 'xa_wo', 'norm_mlp', 'mlp_w1', 'mlp_w2', 'norm_f']
TWIN_DIFF_INPUT = 'x'
TWIN_INPUTS = ['x', 'mem', 'norm_mix', 'w_in', 'conv_a_w', 'conv_a_b', 'ln_a_g', 'ln_a_b', 'dn_conv_w', 'dn_a_log', 'dn_dt_bias', 'dn_norm_g', 'gm_ln_g', 'gm_ln_b', 'gm_ws', 'gm_bs', 'pool_w', 'pool_scale', 'w_branch', 'w_out', 'norm_xa', 'norm_mem', 'xa_wq', 'xa_wkv', 'xa_wo', 'norm_mlp', 'mlp_w1', 'mlp_w2', 'norm_f', 'loss_target', 'm_norm_mix', 'm_w_in', 'm_conv_a_w', 'm_conv_a_b', 'm_ln_a_g', 'm_ln_a_b', 'm_dn_conv_w', 'm_dn_a_log', 'm_dn_dt_bias', 'm_dn_norm_g', 'm_gm_ln_g', 'm_gm_ln_b', 'm_gm_ws', 'm_gm_bs', 'm_pool_w', 'm_pool_scale', 'm_w_branch', 'm_w_out', 'm_norm_xa', 'm_norm_mem', 'm_xa_wq', 'm_xa_wkv', 'm_xa_wo', 'm_norm_mlp', 'm_mlp_w1', 'm_mlp_w2', 'm_norm_f', 'v_norm_mix', 'v_w_in', 'v_conv_a_w', 'v_conv_a_b', 'v_ln_a_g', 'v_ln_a_b', 'v_dn_conv_w', 'v_dn_a_log', 'v_dn_dt_bias', 'v_dn_norm_g', 'v_gm_ln_g', 'v_gm_ln_b', 'v_gm_ws', 'v_gm_bs', 'v_pool_w', 'v_pool_scale', 'v_w_branch', 'v_w_out', 'v_norm_xa', 'v_norm_mem', 'v_xa_wq', 'v_xa_wkv', 'v_xa_wo', 'v_norm_mlp', 'v_mlp_w1', 'v_mlp_w2', 'v_norm_f']
TWIN_OUTPUTS = ['loss', 'grad_x', 'grad_norm_mix', 'grad_w_in', 'grad_conv_a_w', 'grad_conv_a_b', 'grad_ln_a_g', 'grad_ln_a_b', 'grad_dn_conv_w', 'grad_dn_a_log', 'grad_dn_dt_bias', 'grad_dn_norm_g', 'grad_gm_ln_g', 'grad_gm_ln_b', 'grad_gm_ws', 'grad_gm_bs', 'grad_pool_w', 'grad_pool_scale', 'grad_w_branch', 'grad_w_out', 'grad_norm_xa', 'grad_norm_mem', 'grad_xa_wq', 'grad_xa_wkv', 'grad_xa_wo', 'grad_norm_mlp', 'grad_mlp_w1', 'grad_mlp_w2', 'grad_norm_f', 'delta_norm_mix', 'delta_w_in', 'delta_conv_a_w', 'delta_conv_a_b', 'delta_ln_a_g', 'delta_ln_a_b', 'delta_dn_conv_w', 'delta_dn_a_log', 'delta_dn_dt_bias', 'delta_dn_norm_g', 'delta_gm_ln_g', 'delta_gm_ln_b', 'delta_gm_ws', 'delta_gm_bs', 'delta_pool_w', 'delta_pool_scale', 'delta_w_branch', 'delta_w_out', 'delta_norm_xa', 'delta_norm_mem', 'delta_xa_wq', 'delta_xa_wkv', 'delta_xa_wo', 'delta_norm_mlp', 'delta_mlp_w1', 'delta_mlp_w2', 'delta_norm_f', 'new_m_norm_mix', 'new_m_w_in', 'new_m_conv_a_w', 'new_m_conv_a_b', 'new_m_ln_a_g', 'new_m_ln_a_b', 'new_m_dn_conv_w', 'new_m_dn_a_log', 'new_m_dn_dt_bias', 'new_m_dn_norm_g', 'new_m_gm_ln_g', 'new_m_gm_ln_b', 'new_m_gm_ws', 'new_m_gm_bs', 'new_m_pool_w', 'new_m_pool_scale', 'new_m_w_branch', 'new_m_w_out', 'new_m_norm_xa', 'new_m_norm_mem', 'new_m_xa_wq', 'new_m_xa_wkv', 'new_m_xa_wo', 'new_m_norm_mlp', 'new_m_mlp_w1', 'new_m_mlp_w2', 'new_m_norm_f', 'new_v_norm_mix', 'new_v_w_in', 'new_v_conv_a_w', 'new_v_conv_a_b', 'new_v_ln_a_g', 'new_v_ln_a_b', 'new_v_dn_conv_w', 'new_v_dn_a_log', 'new_v_dn_dt_bias', 'new_v_dn_norm_g', 'new_v_gm_ln_g', 'new_v_gm_ln_b', 'new_v_gm_ws', 'new_v_gm_bs', 'new_v_pool_w', 'new_v_pool_scale', 'new_v_w_branch', 'new_v_w_out', 'new_v_norm_xa', 'new_v_norm_mem', 'new_v_xa_wq', 'new_v_xa_wkv', 'new_v_xa_wo', 'new_v_norm_mlp', 'new_v_mlp_w1', 'new_v_mlp_w2', 'new_v_norm_f']
TWIN_LEAF_KINDS = {'loss': 'loss', 'grad_x': 'grad_x', 'grad_norm_mix': 'grad_w', 'grad_w_in': 'grad_w', 'grad_conv_a_w': 'grad_w', 'grad_conv_a_b': 'grad_w', 'grad_ln_a_g': 'grad_w', 'grad_ln_a_b': 'grad_w', 'grad_dn_conv_w': 'grad_w', 'grad_dn_a_log': 'grad_w', 'grad_dn_dt_bias': 'grad_w', 'grad_dn_norm_g': 'grad_w', 'grad_gm_ln_g': 'grad_w', 'grad_gm_ln_b': 'grad_w', 'grad_gm_ws': 'grad_w', 'grad_gm_bs': 'grad_w', 'grad_pool_w': 'grad_w', 'grad_pool_scale': 'grad_w', 'grad_w_branch': 'grad_w', 'grad_w_out': 'grad_w', 'grad_norm_xa': 'grad_w', 'grad_norm_mem': 'grad_w', 'grad_xa_wq': 'grad_w', 'grad_xa_wkv': 'grad_w', 'grad_xa_wo': 'grad_w', 'grad_norm_mlp': 'grad_w', 'grad_mlp_w1': 'grad_w', 'grad_mlp_w2': 'grad_w', 'grad_norm_f': 'grad_w', 'delta_norm_mix': 'delta_w', 'delta_w_in': 'delta_w', 'delta_conv_a_w': 'delta_w', 'delta_conv_a_b': 'delta_w', 'delta_ln_a_g': 'delta_w', 'delta_ln_a_b': 'delta_w', 'delta_dn_conv_w': 'delta_w', 'delta_dn_a_log': 'delta_w', 'delta_dn_dt_bias': 'delta_w', 'delta_dn_norm_g': 'delta_w', 'delta_gm_ln_g': 'delta_w', 'delta_gm_ln_b': 'delta_w', 'delta_gm_ws': 'delta_w', 'delta_gm_bs': 'delta_w', 'delta_pool_w': 'delta_w', 'delta_pool_scale': 'delta_w', 'delta_w_branch': 'delta_w', 'delta_w_out': 'delta_w', 'delta_norm_xa': 'delta_w', 'delta_norm_mem': 'delta_w', 'delta_xa_wq': 'delta_w', 'delta_xa_wkv': 'delta_w', 'delta_xa_wo': 'delta_w', 'delta_norm_mlp': 'delta_w', 'delta_mlp_w1': 'delta_w', 'delta_mlp_w2': 'delta_w', 'delta_norm_f': 'delta_w', 'new_m_norm_mix': 'new_m', 'new_m_w_in': 'new_m', 'new_m_conv_a_w': 'new_m', 'new_m_conv_a_b': 'new_m', 'new_m_ln_a_g': 'new_m', 'new_m_ln_a_b': 'new_m', 'new_m_dn_conv_w': 'new_m', 'new_m_dn_a_log': 'new_m', 'new_m_dn_dt_bias': 'new_m', 'new_m_dn_norm_g': 'new_m', 'new_m_gm_ln_g': 'new_m', 'new_m_gm_ln_b': 'new_m', 'new_m_gm_ws': 'new_m', 'new_m_gm_bs': 'new_m', 'new_m_pool_w': 'new_m', 'new_m_pool_scale': 'new_m', 'new_m_w_branch': 'new_m', 'new_m_w_out': 'new_m', 'new_m_norm_xa': 'new_m', 'new_m_norm_mem': 'new_m', 'new_m_xa_wq': 'new_m', 'new_m_xa_wkv': 'new_m', 'new_m_xa_wo': 'new_m', 'new_m_norm_mlp': 'new_m', 'new_m_mlp_w1': 'new_m', 'new_m_mlp_w2': 'new_m', 'new_m_norm_f': 'new_m', 'new_v_norm_mix': 'new_v', 'new_v_w_in': 'new_v', 'new_v_conv_a_w': 'new_v', 'new_v_conv_a_b': 'new_v', 'new_v_ln_a_g': 'new_v', 'new_v_ln_a_b': 'new_v', 'new_v_dn_conv_w': 'new_v', 'new_v_dn_a_log': 'new_v', 'new_v_dn_dt_bias': 'new_v', 'new_v_dn_norm_g': 'new_v', 'new_v_gm_ln_g': 'new_v', 'new_v_gm_ln_b': 'new_v', 'new_v_gm_ws': 'new_v', 'new_v_gm_bs': 'new_v', 'new_v_pool_w': 'new_v', 'new_v_pool_scale': 'new_v', 'new_v_w_branch': 'new_v', 'new_v_w_out': 'new_v', 'new_v_norm_xa': 'new_v', 'new_v_norm_mem': 'new_v', 'new_v_xa_wq': 'new_v', 'new_v_xa_wkv': 'new_v', 'new_v_xa_wo': 'new_v', 'new_v_norm_mlp': 'new_v', 'new_v_mlp_w1': 'new_v', 'new_v_mlp_w2': 'new_v', 'new_v_norm_f': 'new_v'}


def _forward(args):
    return _fwd_reference(*[args[k] for k in FWD_PARAMS])


def _output_shape():
    def fwd():
        inp = _fwd_setup_inputs(0)
        return _fwd_reference(*[inp[k] for k in FWD_PARAMS])
    out = _jax.eval_shape(fwd)
    return out.shape, out.dtype

N_MICROBATCH = 1
ADAM_LR = 0.001
ADAM_B1 = 0.9
ADAM_B2 = 0.999
ADAM_EPS = 1e-08
ADAM_WD = 0.01
ADAM_STEP = 10
PER_EXAMPLE_BATCH_AXIS = {'x': 0, 'mem': 0, 'loss_target': 0}
SHARED_INPUTS = []
_WEIGHT_DTYPES = {'norm_mix': _jnp.float32, 'w_in': _jnp.float32, 'conv_a_w': _jnp.float32, 'conv_a_b': _jnp.float32, 'ln_a_g': _jnp.float32, 'ln_a_b': _jnp.float32, 'dn_conv_w': _jnp.float32, 'dn_a_log': _jnp.float32, 'dn_dt_bias': _jnp.float32, 'dn_norm_g': _jnp.float32, 'gm_ln_g': _jnp.float32, 'gm_ln_b': _jnp.float32, 'gm_ws': _jnp.float32, 'gm_bs': _jnp.float32, 'pool_w': _jnp.float32, 'pool_scale': _jnp.float32, 'w_branch': _jnp.float32, 'w_out': _jnp.float32, 'norm_xa': _jnp.float32, 'norm_mem': _jnp.float32, 'xa_wq': _jnp.float32, 'xa_wkv': _jnp.float32, 'xa_wo': _jnp.float32, 'norm_mlp': _jnp.float32, 'mlp_w1': _jnp.float32, 'mlp_w2': _jnp.float32, 'norm_f': _jnp.float32}
MOMENT_SCALE = {'norm_mix': 1.575612e-01, 'w_in': 5.320533e-02, 'conv_a_w': 1.044386e-01, 'conv_a_b': 6.468140e-01, 'ln_a_g': 2.617549e-01, 'ln_a_b': 3.736927e-01, 'dn_conv_w': 5.991054e-02, 'dn_a_log': 2.095825e-01, 'dn_dt_bias': 2.021600e-01, 'dn_norm_g': 2.056780e-01, 'gm_ln_g': 4.333959e-02, 'gm_ln_b': 4.641020e-02, 'gm_ws': 4.445192e-02, 'gm_bs': 6.293380e-02, 'pool_w': 8.710602e-02, 'pool_scale': 8.601457e-02, 'w_branch': 1.041510e-01, 'w_out': 2.066445e-01, 'norm_xa': 1.552611e-02, 'norm_mem': 2.907524e-02, 'xa_wq': 1.570582e-02, 'xa_wkv': 1.983653e-02, 'xa_wo': 2.341792e-02, 'norm_mlp': 2.264418e-01, 'mlp_w1': 1.111202e-01, 'mlp_w2': 4.800023e-01, 'norm_f': 3.283519e+01}


def _to_microbatches(a, axis):
    t = _jnp.moveaxis(a, axis, 0)
    t = t.reshape((N_MICROBATCH, t.shape[0] // N_MICROBATCH) + t.shape[1:])
    return _jnp.moveaxis(t, 1, axis + 1)


def setup_inputs(seed: int = 0) -> dict:
    inp = _fwd_setup_inputs(seed)
    key = _jax.random.fold_in(_jax.random.key(seed), 7919)
    shape, _ = _output_shape()
    out = dict(inp)
    out["loss_target"] = _jax.random.normal(_jax.random.fold_in(key, 0), shape, _jnp.float32)
    for i, name in enumerate(TWIN_WEIGHTS):
        w = inp[name].astype(_jnp.float32)
        if MOMENT_SCALE is None:
            s = _jnp.sqrt(_jnp.mean(_jnp.square(w)) + 1e-30)
        else:
            s = MOMENT_SCALE[name]
        km, kv = _jax.random.split(_jax.random.fold_in(key, i + 1))
        out[name] = w
        out["m_" + name] = s * _jax.random.normal(km, w.shape, _jnp.float32)
        out["v_" + name] = (s * s) * _jax.random.uniform(kv, w.shape, _jnp.float32, 0.5, 1.5)
    if N_MICROBATCH > 1:
        for name, axis in PER_EXAMPLE_BATCH_AXIS.items():
            out[name] = _to_microbatches(out[name], axis)
    return {'x': out['x'], 'mem': out['mem'], 'norm_mix': out['norm_mix'], 'w_in': out['w_in'], 'conv_a_w': out['conv_a_w'], 'conv_a_b': out['conv_a_b'], 'ln_a_g': out['ln_a_g'], 'ln_a_b': out['ln_a_b'], 'dn_conv_w': out['dn_conv_w'], 'dn_a_log': out['dn_a_log'], 'dn_dt_bias': out['dn_dt_bias'], 'dn_norm_g': out['dn_norm_g'], 'gm_ln_g': out['gm_ln_g'], 'gm_ln_b': out['gm_ln_b'], 'gm_ws': out['gm_ws'], 'gm_bs': out['gm_bs'], 'pool_w': out['pool_w'], 'pool_scale': out['pool_scale'], 'w_branch': out['w_branch'], 'w_out': out['w_out'], 'norm_xa': out['norm_xa'], 'norm_mem': out['norm_mem'], 'xa_wq': out['xa_wq'], 'xa_wkv': out['xa_wkv'], 'xa_wo': out['xa_wo'], 'norm_mlp': out['norm_mlp'], 'mlp_w1': out['mlp_w1'], 'mlp_w2': out['mlp_w2'], 'norm_f': out['norm_f'], 'loss_target': out['loss_target'], 'm_norm_mix': out['m_norm_mix'], 'm_w_in': out['m_w_in'], 'm_conv_a_w': out['m_conv_a_w'], 'm_conv_a_b': out['m_conv_a_b'], 'm_ln_a_g': out['m_ln_a_g'], 'm_ln_a_b': out['m_ln_a_b'], 'm_dn_conv_w': out['m_dn_conv_w'], 'm_dn_a_log': out['m_dn_a_log'], 'm_dn_dt_bias': out['m_dn_dt_bias'], 'm_dn_norm_g': out['m_dn_norm_g'], 'm_gm_ln_g': out['m_gm_ln_g'], 'm_gm_ln_b': out['m_gm_ln_b'], 'm_gm_ws': out['m_gm_ws'], 'm_gm_bs': out['m_gm_bs'], 'm_pool_w': out['m_pool_w'], 'm_pool_scale': out['m_pool_scale'], 'm_w_branch': out['m_w_branch'], 'm_w_out': out['m_w_out'], 'm_norm_xa': out['m_norm_xa'], 'm_norm_mem': out['m_norm_mem'], 'm_xa_wq': out['m_xa_wq'], 'm_xa_wkv': out['m_xa_wkv'], 'm_xa_wo': out['m_xa_wo'], 'm_norm_mlp': out['m_norm_mlp'], 'm_mlp_w1': out['m_mlp_w1'], 'm_mlp_w2': out['m_mlp_w2'], 'm_norm_f': out['m_norm_f'], 'v_norm_mix': out['v_norm_mix'], 'v_w_in': out['v_w_in'], 'v_conv_a_w': out['v_conv_a_w'], 'v_conv_a_b': out['v_conv_a_b'], 'v_ln_a_g': out['v_ln_a_g'], 'v_ln_a_b': out['v_ln_a_b'], 'v_dn_conv_w': out['v_dn_conv_w'], 'v_dn_a_log': out['v_dn_a_log'], 'v_dn_dt_bias': out['v_dn_dt_bias'], 'v_dn_norm_g': out['v_dn_norm_g'], 'v_gm_ln_g': out['v_gm_ln_g'], 'v_gm_ln_b': out['v_gm_ln_b'], 'v_gm_ws': out['v_gm_ws'], 'v_gm_bs': out['v_gm_bs'], 'v_pool_w': out['v_pool_w'], 'v_pool_scale': out['v_pool_scale'], 'v_w_branch': out['v_w_branch'], 'v_w_out': out['v_w_out'], 'v_norm_xa': out['v_norm_xa'], 'v_norm_mem': out['v_norm_mem'], 'v_xa_wq': out['v_xa_wq'], 'v_xa_wkv': out['v_xa_wkv'], 'v_xa_wo': out['v_xa_wo'], 'v_norm_mlp': out['v_norm_mlp'], 'v_mlp_w1': out['v_mlp_w1'], 'v_mlp_w2': out['v_mlp_w2'], 'v_norm_f': out['v_norm_f']}


def _loss(weights, diff, rest, loss_target):
    with _jax.named_scope("forward"):
        args = {**rest, TWIN_DIFF_INPUT: diff, **{k: w.astype(_WEIGHT_DTYPES[k]) for k, w in weights.items()}}
        y = _forward(args)
    with _jax.named_scope("loss_head"):
        err = _jnp.square(y.astype(_jnp.float32) - loss_target)
        return 0.5 * _jnp.sum(_jnp.mean(err, axis=-1)) if err.ndim else 0.5 * err


def _adamw(w, g, m, v):
    m = ADAM_B1 * m + (1.0 - ADAM_B1) * g
    v = ADAM_B2 * v + (1.0 - ADAM_B2) * _jnp.square(g)
    m_hat = m / (1.0 - ADAM_B1 ** ADAM_STEP)
    v_hat = v / (1.0 - ADAM_B2 ** ADAM_STEP)
    delta = -ADAM_LR * (m_hat / (_jnp.sqrt(v_hat) + ADAM_EPS) + ADAM_WD * w)
    return delta, m, v


def reference(x, mem, norm_mix, w_in, conv_a_w, conv_a_b, ln_a_g, ln_a_b, dn_conv_w, dn_a_log, dn_dt_bias, dn_norm_g, gm_ln_g, gm_ln_b, gm_ws, gm_bs, pool_w, pool_scale, w_branch, w_out, norm_xa, norm_mem, xa_wq, xa_wkv, xa_wo, norm_mlp, mlp_w1, mlp_w2, norm_f, loss_target, m_norm_mix, m_w_in, m_conv_a_w, m_conv_a_b, m_ln_a_g, m_ln_a_b, m_dn_conv_w, m_dn_a_log, m_dn_dt_bias, m_dn_norm_g, m_gm_ln_g, m_gm_ln_b, m_gm_ws, m_gm_bs, m_pool_w, m_pool_scale, m_w_branch, m_w_out, m_norm_xa, m_norm_mem, m_xa_wq, m_xa_wkv, m_xa_wo, m_norm_mlp, m_mlp_w1, m_mlp_w2, m_norm_f, v_norm_mix, v_w_in, v_conv_a_w, v_conv_a_b, v_ln_a_g, v_ln_a_b, v_dn_conv_w, v_dn_a_log, v_dn_dt_bias, v_dn_norm_g, v_gm_ln_g, v_gm_ln_b, v_gm_ws, v_gm_bs, v_pool_w, v_pool_scale, v_w_branch, v_w_out, v_norm_xa, v_norm_mem, v_xa_wq, v_xa_wkv, v_xa_wo, v_norm_mlp, v_mlp_w1, v_mlp_w2, v_norm_f):
    given = dict(x=x, mem=mem, norm_mix=norm_mix, w_in=w_in, conv_a_w=conv_a_w, conv_a_b=conv_a_b, ln_a_g=ln_a_g, ln_a_b=ln_a_b, dn_conv_w=dn_conv_w, dn_a_log=dn_a_log, dn_dt_bias=dn_dt_bias, dn_norm_g=dn_norm_g, gm_ln_g=gm_ln_g, gm_ln_b=gm_ln_b, gm_ws=gm_ws, gm_bs=gm_bs, pool_w=pool_w, pool_scale=pool_scale, w_branch=w_branch, w_out=w_out, norm_xa=norm_xa, norm_mem=norm_mem, xa_wq=xa_wq, xa_wkv=xa_wkv, xa_wo=xa_wo, norm_mlp=norm_mlp, mlp_w1=mlp_w1, mlp_w2=mlp_w2, norm_f=norm_f, loss_target=loss_target, m_norm_mix=m_norm_mix, m_w_in=m_w_in, m_conv_a_w=m_conv_a_w, m_conv_a_b=m_conv_a_b, m_ln_a_g=m_ln_a_g, m_ln_a_b=m_ln_a_b, m_dn_conv_w=m_dn_conv_w, m_dn_a_log=m_dn_a_log, m_dn_dt_bias=m_dn_dt_bias, m_dn_norm_g=m_dn_norm_g, m_gm_ln_g=m_gm_ln_g, m_gm_ln_b=m_gm_ln_b, m_gm_ws=m_gm_ws, m_gm_bs=m_gm_bs, m_pool_w=m_pool_w, m_pool_scale=m_pool_scale, m_w_branch=m_w_branch, m_w_out=m_w_out, m_norm_xa=m_norm_xa, m_norm_mem=m_norm_mem, m_xa_wq=m_xa_wq, m_xa_wkv=m_xa_wkv, m_xa_wo=m_xa_wo, m_norm_mlp=m_norm_mlp, m_mlp_w1=m_mlp_w1, m_mlp_w2=m_mlp_w2, m_norm_f=m_norm_f, v_norm_mix=v_norm_mix, v_w_in=v_w_in, v_conv_a_w=v_conv_a_w, v_conv_a_b=v_conv_a_b, v_ln_a_g=v_ln_a_g, v_ln_a_b=v_ln_a_b, v_dn_conv_w=v_dn_conv_w, v_dn_a_log=v_dn_a_log, v_dn_dt_bias=v_dn_dt_bias, v_dn_norm_g=v_dn_norm_g, v_gm_ln_g=v_gm_ln_g, v_gm_ln_b=v_gm_ln_b, v_gm_ws=v_gm_ws, v_gm_bs=v_gm_bs, v_pool_w=v_pool_w, v_pool_scale=v_pool_scale, v_w_branch=v_w_branch, v_w_out=v_w_out, v_norm_xa=v_norm_xa, v_norm_mem=v_norm_mem, v_xa_wq=v_xa_wq, v_xa_wkv=v_xa_wkv, v_xa_wo=v_xa_wo, v_norm_mlp=v_norm_mlp, v_mlp_w1=v_mlp_w1, v_mlp_w2=v_mlp_w2, v_norm_f=v_norm_f)
    weights = {n: given[n] for n in TWIN_WEIGHTS}
    shared = {n: given[n] for n in SHARED_INPUTS}
    per_example = {n: given[n] for n in ['x', 'mem']}
    grad_fn = _jax.value_and_grad(_loss, argnums=(0, 1))

    def one_microbatch(ex, loss_target):
        ex = dict(ex)
        diff = ex.pop(TWIN_DIFF_INPUT)
        return grad_fn(weights, diff, {**shared, **ex}, loss_target)

    if N_MICROBATCH == 1:
        loss, (grad_w, grad_x) = one_microbatch(per_example, given["loss_target"])
    else:
        def body(carry, xs):
            loss_sum, grad_sum = carry
            l_k, (gw_k, gx_k) = one_microbatch(xs[0], xs[1])
            with _jax.named_scope("update"):
                return (loss_sum + l_k, _jax.tree.map(_jnp.add, grad_sum, gw_k)), gx_k

        init = (_jnp.zeros((), _jnp.float32), _jax.tree.map(_jnp.zeros_like, weights))
        (loss, grad_w), grad_x = _jax.lax.scan(body, init, (per_example, given["loss_target"]))
    with _jax.named_scope("update"):
        delta_w, new_m, new_v = {}, {}, {}
        for n in TWIN_WEIGHTS:
            delta_w[n], new_m[n], new_v[n] = _adamw(weights[n], grad_w[n], given["m_" + n], given["v_" + n])
    return (loss, grad_x, *[grad_w[n] for n in TWIN_WEIGHTS], *[delta_w[n] for n in TWIN_WEIGHTS],
            *[new_m[n] for n in TWIN_WEIGHTS], *[new_v[n] for n in TWIN_WEIGHTS])
```

```python
import functools
import math

import numpy as np
import jax
import jax.numpy as jnp
from jax import lax
from jax.experimental import pallas as pl
from jax.experimental.pallas import tpu as pltpu

F32 = jnp.float32
BF16 = jnp.bfloat16
HIGHEST = lax.Precision.HIGHEST
MESH_ID = pl.DeviceIdType.MESH
VMEM_LIMIT_V7X = 56 << 20

DEPTH = 4
MIX_W = 512
DN_HEADS = 4
DN_HEAD_DIM = 128
DN_CHUNK = 64
GM_CHUNK = 128
GM_GROUPS = 4
POOL_WINDOWS = (2, 4, 8, 16)
XA_HEADS = 4
CONV_PAD = 32

ADAM_LR = 0.001
ADAM_B1 = 0.9
ADAM_B2 = 0.999
ADAM_EPS = 1e-08
ADAM_WD = 0.01
ADAM_STEP = 10

WEIGHTS = ['norm_mix', 'w_in', 'conv_a_w', 'conv_a_b', 'ln_a_g', 'ln_a_b', 'dn_conv_w', 'dn_a_log', 'dn_dt_bias',
           'dn_norm_g', 'gm_ln_g', 'gm_ln_b', 'gm_ws', 'gm_bs', 'pool_w', 'pool_scale', 'w_branch', 'w_out',
           'norm_xa', 'norm_mem', 'xa_wq', 'xa_wkv', 'xa_wo', 'norm_mlp', 'mlp_w1', 'mlp_w2', 'norm_f']
SHARDED = ['w_in', 'conv_a_w', 'dn_conv_w', 'w_branch', 'w_out', 'xa_wq', 'xa_wkv', 'xa_wo', 'mlp_w1', 'mlp_w2']
SENT_AS_BF16 = ['w_in', 'w_branch', 'w_out', 'xa_wq', 'xa_wkv', 'xa_wo', 'mlp_w1', 'mlp_w2']
REPLICATED = [n for n in WEIGHTS if n not in SHARDED]

Z_W = 8832
Z_ORDER = ((4616, 8712), (0, 3072), (3080, 4616), (3072, 3080))
ZB_GATE, ZB_A1, ZB_A2, ZB_Q, ZB_K, ZB_V, ZB_DGATE, ZB_GU, ZB_GV, ZB_POOL = 0, 8, 9, 10, 11, 12, 13, 14, 15, 16
ZB128_BD = 68


def _cparams(*sem):
    return pltpu.CompilerParams(dimension_semantics=sem, vmem_limit_bytes=VMEM_LIMIT_V7X)


def _dot(a, b, ca, cb):
    return lax.dot_general(a.astype(BF16), b.astype(BF16), (((ca,), (cb,)), ((), ())), preferred_element_type=F32)


def _doth(a, b, ca, cb):
    return lax.dot_general(a, b, (((ca,), (cb,)), ((), ())), precision=HIGHEST, preferred_element_type=F32)


def _make_mm(dot):
    @jax.custom_vjp
    def nn(a, b):
        return dot(a, b, 1, 0)
    nn.defvjp(lambda a, b: (dot(a, b, 1, 0), (a, b)), lambda r, g: (dot(g, r[1], 1, 1), dot(r[0], g, 0, 0)))

    @jax.custom_vjp
    def nt(a, b):
        return dot(a, b, 1, 1)
    nt.defvjp(lambda a, b: (dot(a, b, 1, 1), (a, b)), lambda r, g: (dot(g, r[1], 1, 0), dot(g, r[0], 0, 0)))

    @jax.custom_vjp
    def tn(a, b):
        return dot(a, b, 0, 0)
    tn.defvjp(lambda a, b: (dot(a, b, 0, 0), (a, b)), lambda r, g: (dot(r[1], g, 1, 1), dot(r[0], g, 1, 0)))
    return nn, nt, tn


mm, mm_nt, mm_tn = _make_mm(_dot)
mmh, mmh_nt, mmh_tn = _make_mm(_doth)


@jax.custom_vjp
def _mmw(a, w, wz):
    return _dot(a, w, 1, 0)


_mmw.defvjp(lambda a, w, wz: (_dot(a, w, 1, 0), (a, w)),
            lambda r, g: (_dot(g, r[1], 1, 1), jnp.zeros_like(r[1]), _dot(r[0], g, 0, 0)))


def mmw(a, wpair):
    w, wz = wpair
    return _dot(a, w, 1, 0) if wz is None else _mmw(a, w, wz)


def wsel(wpair, n):
    return (wpair[0][n], None if wpair[1] is None else wpair[1][n])


def _sigmoid(x):
    return 1.0 / (1.0 + jnp.exp(-x))


def _silu(x):
    return x * _sigmoid(x)


def _rms(x, g, eps=1e-6):
    return x * lax.rsqrt(jnp.mean(x * x, axis=-1, keepdims=True) + eps) * g


def _ln(x, g, b, eps=1e-5):
    mu = jnp.mean(x, axis=-1, keepdims=True)
    d = x - mu
    return d * lax.rsqrt(jnp.mean(d * d, axis=-1, keepdims=True) + eps) * g + b


def _gelu(x):
    return 0.5 * x * (1.0 + lax.erf(x * (2.0 ** -0.5)))


def _softplus(x):
    return jnp.maximum(x, 0.0) + jnp.log(1.0 + jnp.exp(-jnp.abs(x)))


def _row_spec(T, width, cb):
    return pl.BlockSpec((T, width), lambda i: (i, cb))


def _whole_spec(p):
    nd = p.ndim
    return pl.BlockSpec(p.shape, lambda i: (0,) * nd)


def _load_params(refs, kinds, with_zeros):
    out = []
    for r, k in zip(refs, kinds):
        if k == 'w':
            out.append((r[...], jnp.zeros(r.shape, F32) if with_zeros else None))
        else:
            out.append(r[...].astype(F32))
    return out


def rows_fwd(f, rows, params, kinds, outs, T, name):
    S = rows[0][0].shape[0]
    nr, npar = len(rows), len(params)

    def body(*refs):
        r = [x[...].astype(F32) for x in refs[:nr]]
        p = _load_params(refs[nr:nr + npar], kinds, False)
        res = f(*r, *p)
        for o_ref, o in zip(refs[nr + npar:], res):
            o_ref[...] = o.astype(o_ref.dtype)

    return pl.pallas_call(
        body, grid=(S // T,), name=name,
        in_specs=[_row_spec(T, w, cb) for _, w, cb in rows] + [_whole_spec(p) for p in params],
        out_specs=[_row_spec(T, w, 0) for w, _ in outs],
        out_shape=[jax.ShapeDtypeStruct((S, w), dt) for w, dt in outs],
        compiler_params=_cparams("parallel"),
    )(*[a for a, _, _ in rows], *params)


def rows_bwd(f, rows, params, kinds, cts, row_dtypes, T, name):
    S = rows[0][0].shape[0]
    nr, npar, nc = len(rows), len(params), len(cts)
    want = [i for i, dt in enumerate(row_dtypes) if dt is not None]

    def body(*refs):
        r = [x[...].astype(F32) for x in refs[:nr]]
        p = _load_params(refs[nr:nr + npar], kinds, True)
        g = [x[...].astype(F32) for x in refs[nr + npar:nr + npar + nc]]
        d_rows = refs[nr + npar + nc:nr + npar + nc + len(want)]
        d_params = refs[nr + npar + nc + len(want):]
        _, vjp = jax.vjp(f, *r, *p)
        grads = vjp(tuple(g))
        for o_ref, i in zip(d_rows, want):
            o_ref[...] = grads[i].astype(o_ref.dtype)
        first = pl.program_id(0) == 0
        for o_ref, gp, k in zip(d_params, grads[nr:], kinds):
            gp = gp[1] if k == 'w' else gp

            @pl.when(first)
            def _():
                o_ref[...] = gp

            @pl.when(jnp.logical_not(first))
            def _():
                o_ref[...] += gp

    res = pl.pallas_call(
        body, grid=(S // T,), name=name,
        in_specs=([_row_spec(T, w, cb) for _, w, cb in rows] + [_whole_spec(p) for p in params]
                  + [_row_spec(T, c.shape[1], 0) for c in cts]),
        out_specs=[_row_spec(T, rows[i][1], 0) for i in want] + [_whole_spec(p) for p in params],
        out_shape=([jax.ShapeDtypeStruct((S, rows[i][1]), row_dtypes[i]) for i in want]
                   + [jax.ShapeDtypeStruct(p.shape, F32) for p in params]),
        compiler_params=_cparams("arbitrary"),
    )(*[a for a, _, _ in rows], *params, *cts)
    return res[:len(want)], res[len(want):]


def f_norm(x, g):
    return (_rms(x, g),)


def f_glu(a1, a2):
    return (a1 * _sigmoid(a2),)


def f_lnsilu(cv, cb, g, b):
    return (_silu(_ln(cv + cb, g, b)),)


def f_gbeta(bd, alp, dtp):
    j = lax.broadcasted_iota(jnp.int32, (128, MIX_W), 0)
    head = lax.broadcasted_iota(jnp.int32, (128, MIX_W), 1) // DN_HEAD_DIM
    e_lo = (j == head).astype(F32)
    e_hi = (j == head + DN_HEADS).astype(F32)
    beta = _sigmoid(mmh(bd, e_lo))
    a_log = jnp.sum(mmh(alp, e_lo), axis=0, keepdims=True)
    dt_bias = jnp.sum(mmh(dtp, e_lo), axis=0, keepdims=True)
    g = -jnp.exp(a_log) * _softplus(mmh(bd, e_hi) + dt_bias)
    return g, beta


def f_gmlp(u_in, v_in, lg, lb, ws, b0, b1, b2, b3):
    T = u_in.shape[0]
    u = _gelu(u_in)
    vg = _ln(_gelu(v_in), lg, lb)
    tril = (lax.broadcasted_iota(jnp.int32, (GM_CHUNK, GM_CHUNK), 0)
            >= lax.broadcasted_iota(jnp.int32, (GM_CHUNK, GM_CHUNK), 1))
    bias = (b0, b1, b2, b3)
    chunks = []
    for r in range(T // GM_CHUNK):
        vr = vg[r * GM_CHUNK:(r + 1) * GM_CHUNK]
        cols = []
        for gi in range(GM_GROUPS):
            w = jnp.where(tril, ws[gi], 0.0)
            cols.append(mm(w, vr[:, gi * 128:(gi + 1) * 128]) + bias[gi])
        chunks.append(jnp.concatenate(cols, axis=1))
    mixed = chunks[0] if len(chunks) == 1 else jnp.concatenate(chunks, axis=0)
    return (u * mixed,)


def f_pool(cs, xin, pw, scale):
    T = cs.shape[0]
    t = pl.program_id(0) * T + lax.broadcasted_iota(jnp.int32, (T, 128), 0)
    cols = []
    for gi, win in enumerate(POOL_WINDOWS):
        count = jnp.minimum(t + 1, win).astype(F32)
        sl = slice(gi * 128, (gi + 1) * 128)
        cols.append(mm(cs[:, sl] / count - xin[:, sl], pw[gi]))
    return (jnp.concatenate(cols, axis=1) * scale,)


def f_merge(a, o, c, p, gate, wb):
    D = gate.shape[1] // 4
    acc = None
    for n, br in enumerate((a, o, c, p)):
        term = _sigmoid(gate[:, n * D:(n + 1) * D]) * mmw(br, wsel(wb, n))
        acc = term if acc is None else acc + term
    return (acc,)


def f_kv(mem, nm, wkv):
    return (mmw(_rms(mem, nm), wkv),)


def f_xattn(x, kv, nx, wq, wo):
    D = x.shape[1]
    hd = D // XA_HEADS
    q = mmw(_rms(x, nx), wq)
    heads = []
    for h in range(XA_HEADS):
        s = mm_nt(q[:, h * hd:(h + 1) * hd], kv[:, h * hd:(h + 1) * hd]) * (hd ** -0.5)
        s = s - jnp.max(s, axis=-1, keepdims=True)
        e = jnp.exp(s)
        pr = e / jnp.sum(e, axis=-1, keepdims=True)
        heads.append(mm(pr, kv[:, D + h * hd:D + (h + 1) * hd]))
    return (x + mmw(jnp.concatenate(heads, axis=1), wo),)


def f_loss(x, tgt, nf):
    err = _rms(x, nf) - tgt
    return (0.5 * jnp.sum(jnp.mean(err * err, axis=-1, keepdims=True), axis=0, keepdims=True),)


def matmul(a, b, mode, *, tm, tn, tk, name, out_dtype=F32, res=None, act=None, gate=None):
    if mode == 'nn':
        (M, K), N = a.shape, b.shape[1]
        a_spec = pl.BlockSpec((tm, tk), lambda i, j, k: (i, k))
        b_spec = pl.BlockSpec((tk, tn), lambda i, j, k: (k, j))
        ca, cb = 1, 0
    elif mode == 'nt':
        (M, K), N = a.shape, b.shape[0]
        a_spec = pl.BlockSpec((tm, tk), lambda i, j, k: (i, k))
        b_spec = pl.BlockSpec((tn, tk), lambda i, j, k: (j, k))
        ca, cb = 1, 1
    else:
        (K, M), N = a.shape, b.shape[1]
        a_spec = pl.BlockSpec((tk, tm), lambda i, j, k: (k, i))
        b_spec = pl.BlockSpec((tk, tn), lambda i, j, k: (k, j))
        ca, cb = 0, 0
    assert M % tm == 0 and N % tn == 0 and K % tk == 0, (a.shape, b.shape, mode, tm, tn, tk)
    nk = K // tk
    o_spec = pl.BlockSpec((tm, tn), lambda i, j, k: (i, j))
    extra = [e for e in (res, gate) if e is not None]

    def body(*refs):
        a_ref, b_ref = refs[:2]
        e_refs = refs[2:2 + len(extra)]
        o_refs = refs[2 + len(extra):2 + len(extra) + (2 if act else 1)]
        part = _dot(a_ref[...], b_ref[...], ca, cb)

        def finish(acc):
            if res is not None:
                acc = acc + e_refs[0][...]
            if gate is not None:
                acc = acc * (2.0 * jnp.maximum(e_refs[-1][...], 0.0))
            o_refs[0][...] = acc.astype(o_refs[0].dtype)
            if act:
                r = jnp.maximum(acc, 0.0)
                o_refs[1][...] = (r * r).astype(o_refs[1].dtype)

        if nk == 1:
            finish(part)
        else:
            acc_ref = refs[-1]
            k = pl.program_id(2)

            @pl.when(k == 0)
            def _():
                acc_ref[...] = part

            @pl.when(k > 0)
            def _():
                acc_ref[...] += part

            @pl.when(k == nk - 1)
            def _():
                finish(acc_ref[...])

    out_shape = [jax.ShapeDtypeStruct((M, N), out_dtype)]
    if act:
        out_shape.append(jax.ShapeDtypeStruct((M, N), BF16))
    res_ = pl.pallas_call(
        body, grid=(M // tm, N // tn, nk), name=name,
        in_specs=[a_spec, b_spec] + [o_spec] * len(extra),
        out_specs=[o_spec] * len(out_shape), out_shape=out_shape,
        scratch_shapes=[pltpu.VMEM((tm, tn), F32)] if nk > 1 else [],
        compiler_params=_cparams("parallel", "parallel", "arbitrary"),
    )(a, b, *extra)
    return res_ if act else res_[0]


def _conv_rows(S):
    return min(S, 512)


def conv_fwd(x, cb0, w, name):
    S = x.shape[0]
    K = w.shape[0]
    R = _conv_rows(S)

    def body(x_ref, w_ref, y_ref, pad_ref):
        pad_ref[pl.ds(0, CONV_PAD), :] = jnp.zeros((CONV_PAD, 128), F32)
        pad_ref[pl.ds(CONV_PAD, S), :] = x_ref[...]
        wv = w_ref[...]

        def chunk(r, carry):
            r0 = pl.multiple_of(r * R, R)
            win = pad_ref[pl.ds(r0, R + CONV_PAD), :]
            acc = jnp.zeros((R, 128), F32)
            for s in range(K):
                sh = win if s == 0 else pltpu.roll(win, s, 0)
                acc = acc + sh[CONV_PAD:, :] * wv[K - 1 - s:K - s, :]
            y_ref[pl.ds(r0, R), :] = acc
            return carry

        lax.fori_loop(0, S // R, chunk, 0)

    return pl.pallas_call(
        body, grid=(4,), name=name,
        in_specs=[pl.BlockSpec((S, 128), lambda j: (0, cb0 * 4 + j)), pl.BlockSpec((K, 128), lambda j: (0, j))],
        out_specs=pl.BlockSpec((S, 128), lambda j: (0, j)),
        out_shape=jax.ShapeDtypeStruct((S, MIX_W), F32),
        scratch_shapes=[pltpu.VMEM((S + CONV_PAD, 128), F32)],
        compiler_params=_cparams("parallel"),
    )(x, w)


def conv_bwd(x, cb0, w, dy, name, add=None, out_dtype=F32):
    S = x.shape[0]
    K = w.shape[0]
    R = _conv_rows(S)
    W = R + CONV_PAD

    def body(*refs):
        x_ref, w_ref, dy_ref = refs[:3]
        add_ref = refs[3] if add is not None else None
        dx_ref, dw_ref, xpad_ref, dypad_ref = refs[-4:]
        xpad_ref[pl.ds(0, CONV_PAD), :] = jnp.zeros((CONV_PAD, 128), F32)
        xpad_ref[pl.ds(CONV_PAD, S), :] = x_ref[...]
        dypad_ref[pl.ds(S, CONV_PAD), :] = jnp.zeros((CONV_PAD, 128), F32)
        dypad_ref[pl.ds(0, S), :] = dy_ref[...].astype(F32)
        dw_ref[...] = jnp.zeros((K, 128), F32)
        wv = w_ref[...]

        def chunk(r, carry):
            r0 = pl.multiple_of(r * R, R)
            xwin = xpad_ref[pl.ds(r0, W), :]
            dwin = dypad_ref[pl.ds(r0, W), :]
            dyc = dwin[:R, :]
            acc = jnp.zeros((R, 128), F32)
            for s in range(K):
                up = dwin if s == 0 else pltpu.roll(dwin, W - s, 0)
                acc = acc + up[:R, :] * wv[K - 1 - s:K - s, :]
                xs = xwin if s == 0 else pltpu.roll(xwin, s, 0)
                dw_ref[pl.ds(K - 1 - s, 1), :] += jnp.sum(dyc * xs[CONV_PAD:, :], axis=0, keepdims=True)
            if add_ref is not None:
                acc = acc + add_ref[pl.ds(r0, R), :].astype(F32)
            dx_ref[pl.ds(r0, R), :] = acc.astype(dx_ref.dtype)
            return carry

        lax.fori_loop(0, S // R, chunk, 0)

    col = pl.BlockSpec((S, 128), lambda j: (0, j))
    ins = [x, w, dy] + ([add] if add is not None else [])
    return pl.pallas_call(
        body, grid=(4,), name=name,
        in_specs=[pl.BlockSpec((S, 128), lambda j: (0, cb0 * 4 + j)), pl.BlockSpec((K, 128), lambda j: (0, j)), col]
        + ([col] if add is not None else []),
        out_specs=[col, pl.BlockSpec((K, 128), lambda j: (0, j))],
        out_shape=[jax.ShapeDtypeStruct((S, MIX_W), out_dtype), jax.ShapeDtypeStruct((K, MIX_W), F32)],
        scratch_shapes=[pltpu.VMEM((S + CONV_PAD, 128), F32), pltpu.VMEM((S + CONV_PAD, 128), F32)],
        compiler_params=_cparams("parallel"),
    )(*ins)


def _delta_chunk(states, qc, kc, vc, gate, ge, be, ng):
    C, Dh = DN_CHUNK, DN_HEAD_DIM
    ii = lax.broadcasted_iota(jnp.int32, (C, C), 0)
    jj = lax.broadcasted_iota(jnp.int32, (C, C), 1)
    causal, strict = ii >= jj, ii > jj
    tri = causal.astype(F32)
    eye = (ii == jj).astype(F32)
    new_states, outs = [], []
    for h in range(DN_HEADS):
        sl = slice(h * Dh, (h + 1) * Dh)
        q, k, v = _silu(qc[:, sl]), _silu(kc[:, sl]), _silu(vc[:, sl])
        q = q * lax.rsqrt(jnp.sum(q * q, axis=-1, keepdims=True) + 1e-6) * (Dh ** -0.5)
        k = k * lax.rsqrt(jnp.sum(k * k, axis=-1, keepdims=True) + 1e-6)
        g, beta = ge[:, sl], be[:, sl]
        gam = mmh(tri, g)
        g_last = mmh(jnp.ones((C, C), F32), g)
        gam_col = gam[:, :C]
        gam_row = mmh_nt(jnp.full((C, Dh), 1.0 / Dh, F32), gam)
        decay = jnp.where(causal, jnp.exp(jnp.where(causal, gam_col - gam_row, 0.0)), 0.0)
        kb = k * beta
        a = jnp.where(strict, mm_nt(kb, k) * decay, 0.0)
        inv = eye - a
        pw = a
        for _ in range(5):
            pw = mmh(pw, pw)
            inv = inv + mmh(inv, pw)
        e_gam = jnp.exp(gam)
        u = mmh(inv, v * beta)
        w = mmh(inv, kb * e_gam)
        attn = mm_nt(q, k) * decay
        s_in = states[h]
        v_new = u - mm(w, s_in)
        o = mm(q * e_gam, s_in) + mm(attn, v_new)
        scale = jnp.exp(jnp.concatenate([g_last, g_last], axis=0))
        new_states.append(s_in * scale + mm_tn(k * jnp.exp(g_last - gam), v_new))
        outs.append(_rms(o, ng) * _silu(gate[:, sl]))
    return new_states, jnp.concatenate(outs, axis=1)


def delta_fwd(qc, kc, vc, z, ge, be, ng, name):
    S = qc.shape[0]
    N = S // DN_CHUNK
    C = DN_CHUNK

    def body(q_ref, k_ref, v_ref, gate_ref, ge_ref, be_ref, ng_ref, o_ref, st_ref, s_ref):
        @pl.when(pl.program_id(0) == 0)
        def _():
            s_ref[...] = jnp.zeros(s_ref.shape, F32)

        states = [s_ref[h] for h in range(DN_HEADS)]
        for h in range(DN_HEADS):
            st_ref[0, h] = states[h]
        new_states, o = _delta_chunk(states, q_ref[...], k_ref[...], v_ref[...], gate_ref[...], ge_ref[...],
                                     be_ref[...], ng_ref[...])
        for h in range(DN_HEADS):
            s_ref[h] = new_states[h]
        o_ref[...] = o.astype(o_ref.dtype)

    blk = pl.BlockSpec((C, MIX_W), lambda n: (n, 0))
    return pl.pallas_call(
        body, grid=(N,), name=name,
        in_specs=[blk, blk, blk, pl.BlockSpec((C, MIX_W), lambda n: (n, ZB_DGATE)), blk, blk,
                  pl.BlockSpec((1, DN_HEAD_DIM), lambda n: (0, 0))],
        out_specs=[blk, pl.BlockSpec((1, DN_HEADS, DN_HEAD_DIM, DN_HEAD_DIM), lambda n: (n, 0, 0, 0))],
        out_shape=[jax.ShapeDtypeStruct((S, MIX_W), BF16),
                   jax.ShapeDtypeStruct((N, DN_HEADS, DN_HEAD_DIM, DN_HEAD_DIM), F32)],
        scratch_shapes=[pltpu.VMEM((DN_HEADS, DN_HEAD_DIM, DN_HEAD_DIM), F32)],
        compiler_params=_cparams("arbitrary"),
    )(qc, kc, vc, z, ge, be, ng)


def delta_bwd(qc, kc, vc, z, ge, be, ng, states, do, name):
    S = qc.shape[0]
    N = S // DN_CHUNK
    C = DN_CHUNK

    def body(q_ref, k_ref, v_ref, gate_ref, ge_ref, be_ref, ng_ref, st_ref, do_ref,
             dq_ref, dk_ref, dv_ref, dge_ref, dbe_ref, dgate_ref, dng_ref, ds_ref):
        first = pl.program_id(0) == 0

        @pl.when(first)
        def _():
            ds_ref[...] = jnp.zeros(ds_ref.shape, F32)

        states_in = [st_ref[0, h] for h in range(DN_HEADS)]
        args = (states_in, q_ref[...], k_ref[...], v_ref[...], gate_ref[...].astype(F32), ge_ref[...], be_ref[...],
                ng_ref[...])
        _, vjp = jax.vjp(_delta_chunk, *args)
        d_states, dq, dk, dv, dgate, dge, dbe, dng = vjp(([ds_ref[h] for h in range(DN_HEADS)],
                                                          do_ref[...].astype(F32)))
        for h in range(DN_HEADS):
            ds_ref[h] = d_states[h]
        dq_ref[...] = dq
        dk_ref[...] = dk
        dv_ref[...] = dv
        dge_ref[...] = dge
        dbe_ref[...] = dbe
        dgate_ref[...] = dgate.astype(dgate_ref.dtype)

        @pl.when(first)
        def _():
            dng_ref[...] = dng

        @pl.when(jnp.logical_not(first))
        def _():
            dng_ref[...] += dng

    blk = pl.BlockSpec((C, MIX_W), lambda n: (N - 1 - n, 0))
    ngs = pl.BlockSpec((1, DN_HEAD_DIM), lambda n: (0, 0))
    f32o = jax.ShapeDtypeStruct((S, MIX_W), F32)
    return pl.pallas_call(
        body, grid=(N,), name=name,
        in_specs=[blk, blk, blk, pl.BlockSpec((C, MIX_W), lambda n: (N - 1 - n, ZB_DGATE)), blk, blk, ngs,
                  pl.BlockSpec((1, DN_HEADS, DN_HEAD_DIM, DN_HEAD_DIM), lambda n: (N - 1 - n, 0, 0, 0)), blk],
        out_specs=[blk, blk, blk, blk, blk, blk, ngs],
        out_shape=[f32o, f32o, f32o, f32o, f32o, jax.ShapeDtypeStruct((S, MIX_W), BF16),
                   jax.ShapeDtypeStruct((1, DN_HEAD_DIM), F32)],
        scratch_shapes=[pltpu.VMEM((DN_HEADS, DN_HEAD_DIM, DN_HEAD_DIM), F32)],
        compiler_params=_cparams("arbitrary"),
    )(qc, kc, vc, z, ge, be, ng, states, do)


ANY = pl.BlockSpec(memory_space=pl.ANY)


def _place():
    return lax.axis_index("x"), lax.axis_index("y"), lax.axis_index("c")


def all_gather(shards, name):
    n = len(shards)

    def body(*refs):
        ins, outs = refs[:n], refs[n:2 * n]
        send_sems, recv_sems, local_sems = refs[2 * n:]
        x, y, c = _place()
        me, sibling = (x, y, c), (x, y, 1 - c)
        chips = [(1 - x, y), (x, 1 - y), (1 - x, 1 - y)]

        def copy(a, k, block, to, src=None):
            row = 4 * block[0] + 2 * block[1] + block[2]
            return pltpu.make_async_remote_copy(
                src_ref=outs[a].at[row] if src is None else src, dst_ref=outs[a].at[row],
                send_sem=send_sems.at[a, k], recv_sem=recv_sems.at[a, k], device_id=to, device_id_type=MESH_ID)

        mine = [pltpu.make_async_copy(ins[a], outs[a].at[4 * x + 2 * y + c], local_sems.at[a]) for a in range(n)]
        for cp in mine:
            cp.start()
        first = []
        for a in range(n):
            first.append(copy(a, 0, me, sibling, src=ins[a]))
            first += [copy(a, 1 + j, me, (*chip, c), src=ins[a]) for j, chip in enumerate(chips)]
        for cp in first:
            cp.start()
        passed = []
        for j, chip in enumerate(chips):
            for a in range(n):
                copy(a, 1 + j, (*chip, c), me).wait_recv()
                passed.append(copy(a, 4 + j, (*chip, c), sibling))
                passed[-1].start()
        for a in range(n):
            copy(a, 0, sibling, me).wait_recv()
            for j, chip in enumerate(chips):
                copy(a, 4 + j, (*chip, 1 - c), me).wait_recv()
        for cp in first + passed:
            cp.wait_send()
        for cp in mine:
            cp.wait()

    return pl.pallas_call(
        body, name=name, in_specs=[ANY] * n, out_specs=[ANY] * n,
        out_shape=[jax.ShapeDtypeStruct((8,) + s.shape, s.dtype) for s in shards],
        scratch_shapes=[pltpu.SemaphoreType.DMA((n, 7)), pltpu.SemaphoreType.DMA((n, 7)), pltpu.SemaphoreType.DMA((n,))],
    )(*shards)


def exchange_cores(grads, name):
    n = len(grads)

    def body(*refs):
        ins, outs = refs[:n], refs[n:2 * n]
        send_sems, recv_sems, local_sems = refs[2 * n:]
        x, y, c = _place()
        sibling = (x, y, 1 - c)
        local, remote = [], []
        for a in range(n):
            for k in range(4):
                local.append(pltpu.make_async_copy(ins[a].at[2 * k + c], outs[a].at[0, k], local_sems.at[a, k]))
                remote.append(pltpu.make_async_remote_copy(
                    src_ref=ins[a].at[2 * k + 1 - c], dst_ref=outs[a].at[1, k], send_sem=send_sems.at[a, k],
                    recv_sem=recv_sems.at[a, k], device_id=sibling, device_id_type=MESH_ID))
        for cp in local + remote:
            cp.start()
        for cp in remote:
            cp.wait()
        for cp in local:
            cp.wait()

    return pl.pallas_call(
        body, name=name, in_specs=[ANY] * n, out_specs=[ANY] * n,
        out_shape=[jax.ShapeDtypeStruct((2, 4) + g.shape[1:], g.dtype) for g in grads],
        scratch_shapes=[pltpu.SemaphoreType.DMA((n, 4))] * 3,
    )(*grads)


def exchange_chips(parts, name):
    n = len(parts)

    def body(*refs):
        ins, outs = refs[:n], refs[n:2 * n]
        send_sems, recv_sems, local_sems = refs[2 * n:]
        x, y, c = _place()
        chips = [(1 - x, y), (x, 1 - y), (1 - x, 1 - y)]
        local, remote = [], []
        for a in range(n):
            local.append(pltpu.make_async_copy(ins[a].at[2 * x + y], outs[a].at[0], local_sems.at[a]))
            for j, (px, py) in enumerate(chips):
                remote.append(pltpu.make_async_remote_copy(
                    src_ref=ins[a].at[2 * px + py], dst_ref=outs[a].at[1 + j], send_sem=send_sems.at[a, j],
                    recv_sem=recv_sems.at[a, j], device_id=(px, py, c), device_id_type=MESH_ID))
        for cp in local + remote:
            cp.start()
        for cp in remote:
            cp.wait()
        for cp in local:
            cp.wait()

    return pl.pallas_call(
        body, name=name, in_specs=[ANY] * n, out_specs=[ANY] * n,
        out_shape=[jax.ShapeDtypeStruct(p.shape, p.dtype) for p in parts],
        scratch_shapes=[pltpu.SemaphoreType.DMA((n, 3)), pltpu.SemaphoreType.DMA((n, 3)), pltpu.SemaphoreType.DMA((n,))],
    )(*parts)


def _as2d(a, lead=0):
    return a.reshape(a.shape[:lead] + (-1, a.shape[-1]))


def _row_tile(rows, cols, n_arrays):
    budget = VMEM_LIMIT_V7X // 2
    lanes = -(-cols // 128) * 128
    t = budget // (2 * n_arrays * lanes * 4)
    if t >= rows:
        return rows
    return max(8, t // 8 * 8)


def add_pairs(pair, name):
    _, four, R, C = pair.shape
    T = _row_tile(R, C, 3)

    def body(p_ref, o_ref):
        o_ref[0] = p_ref[0, 0] + p_ref[1, 0]

    return pl.pallas_call(
        body, grid=(four, pl.cdiv(R, T)), name=name,
        in_specs=[pl.BlockSpec((2, 1, T, C), lambda k, i: (0, k, i, 0))],
        out_specs=pl.BlockSpec((1, T, C), lambda k, i: (k, i, 0)),
        out_shape=jax.ShapeDtypeStruct((four, R, C), pair.dtype),
        compiler_params=_cparams("parallel", "parallel"),
    )(pair)


def _adamw(w, g, m, v):
    m = ADAM_B1 * m + (1.0 - ADAM_B1) * g
    v = ADAM_B2 * v + (1.0 - ADAM_B2) * jnp.square(g)
    m_hat = m / (1.0 - ADAM_B1 ** ADAM_STEP)
    v_hat = v / (1.0 - ADAM_B2 ** ADAM_STEP)
    delta = -ADAM_LR * (m_hat / (jnp.sqrt(v_hat) + ADAM_EPS) + ADAM_WD * w)
    return delta, m, v


def adamw_sum(parts, w, m, v, name):
    P, R, C = parts.shape
    T = _row_tile(R, C, P + 7)

    def body(p_ref, w_ref, m_ref, v_ref, g_ref, d_ref, nm_ref, nv_ref):
        g = p_ref[0]
        for k in range(1, P):
            g = g + p_ref[k]
        d, nm, nv = _adamw(w_ref[...], g, m_ref[...], v_ref[...])
        g_ref[...] = g
        d_ref[...] = d
        nm_ref[...] = nm
        nv_ref[...] = nv

    blk = pl.BlockSpec((T, C), lambda i: (i, 0))
    return pl.pallas_call(
        body, grid=(pl.cdiv(R, T),), name=name,
        in_specs=[pl.BlockSpec((P, T, C), lambda i: (0, i, 0)), blk, blk, blk],
        out_specs=[blk] * 4, out_shape=[jax.ShapeDtypeStruct((R, C), F32)] * 4,
        compiler_params=_cparams("parallel"),
    )(parts, w, m, v)


def f_norm_res(x, g):
    return _rms(x, g), x


def _pool_taps():
    taps = np.zeros((16, MIX_W), np.float32)
    for gi, win in enumerate(POOL_WINDOWS):
        taps[16 - win:, gi * 128:(gi + 1) * 128] = 1.0
    return jnp.asarray(taps)


def loss_head(x, tgt, nf, T, name):
    S, D = x.shape

    def body(x_ref, t_ref, g_ref, l_ref, dx_ref, dg_ref):
        val, vjp = jax.vjp(f_loss, x_ref[...], t_ref[...], g_ref[...])
        dx, _, dg = vjp((jnp.ones((1, 1), F32),))
        dx_ref[...] = dx
        first = pl.program_id(0) == 0
        lv = jnp.broadcast_to(val[0], (1, 128))

        @pl.when(first)
        def _():
            l_ref[...] = lv
            dg_ref[...] = dg

        @pl.when(jnp.logical_not(first))
        def _():
            l_ref[...] += lv
            dg_ref[...] += dg

    row = pl.BlockSpec((T, D), lambda i: (i, 0))
    return pl.pallas_call(
        body, grid=(S // T,), name=name,
        in_specs=[row, row, pl.BlockSpec((1, D), lambda i: (0, 0))],
        out_specs=[pl.BlockSpec((1, 128), lambda i: (0, 0)), row, pl.BlockSpec((1, D), lambda i: (0, 0))],
        out_shape=[jax.ShapeDtypeStruct((1, 128), F32), jax.ShapeDtypeStruct((S, D), F32),
                   jax.ShapeDtypeStruct((1, D), F32)],
        compiler_params=_cparams("arbitrary"),
    )(x, tgt, nf)


def _tiles(S):
    return min(S, 512)


def layer_fwd(x, mem, W, tag):
    S, D = x.shape
    T = min(S, 256)
    tm = _tiles(S)
    sv = {'x': x}
    (h1,) = rows_fwd(f_norm, [(x, D, 0)], [W['norm_mix']], 'p', [(D, BF16)], T, tag + 'norm_mix')
    z = matmul(h1, W['w_in'], 'nn', tm=tm, tn=384, tk=D, name=tag + 'w_in')
    (a_pre,) = rows_fwd(f_glu, [(z, 512, ZB_A1), (z, 512, ZB_A2)], [], '', [(512, F32)], T, tag + 'glu')
    a_cv = conv_fwd(a_pre, 0, W['conv_a_w'], tag + 'conv_a')
    (a,) = rows_fwd(f_lnsilu, [(a_cv, 512, 0)], [W['conv_a_b'], W['ln_a_g'], W['ln_a_b']], 'ppp', [(512, BF16)], T,
                    tag + 'ln_a')
    qc = conv_fwd(z, ZB_Q, W['dn_wq'], tag + 'conv_q')
    kc = conv_fwd(z, ZB_K, W['dn_wk'], tag + 'conv_k')
    vc = conv_fwd(z, ZB_V, W['dn_wv'], tag + 'conv_v')
    ge, be = rows_fwd(f_gbeta, [(z, 128, ZB128_BD)], [W['alp'], W['dtp']], 'pp', [(512, F32), (512, F32)], T,
                      tag + 'gbeta')
    o, states = delta_fwd(qc, kc, vc, z, ge, be, W['dn_norm_g'], tag + 'delta')
    gm_params = [W['gm_ln_g'], W['gm_ln_b'], W['gm_ws']] + W['gm_b']
    (c,) = rows_fwd(f_gmlp, [(z, 512, ZB_GU), (z, 512, ZB_GV)], gm_params, 'p' * 7, [(512, BF16)], T, tag + 'gmlp')
    cs = conv_fwd(z, ZB_POOL, _pool_taps(), tag + 'pool_sum')
    (p,) = rows_fwd(f_pool, [(cs, 512, 0), (z, 512, ZB_POOL)], [W['pool_w'], W['pool_scale']], 'pp', [(512, BF16)], T,
                    tag + 'pool')
    (merged,) = rows_fwd(f_merge, [(a, 512, 0), (o, 512, 0), (c, 512, 0), (p, 512, 0), (z, 4 * D, ZB_GATE)],
                         [W['w_branch']], 'w', [(D, BF16)], min(S, 128), tag + 'merge')
    x1 = matmul(merged, W['w_out'], 'nn', tm=tm, tn=512, tk=D, res=x, name=tag + 'w_out')
    (kv,) = rows_fwd(f_kv, [(mem, D, 0)], [W['norm_mem'], W['xa_wkv']], 'pw', [(2 * D, F32)], mem.shape[0],
                     tag + 'kv')
    (x2,) = rows_fwd(f_xattn, [(x1, D, 0)], [kv, W['norm_xa'], W['xa_wq'], W['xa_wo']], 'ppww', [(D, F32)], T,
                     tag + 'xattn')
    (h3,) = rows_fwd(f_norm, [(x2, D, 0)], [W['norm_mlp']], 'p', [(D, BF16)], T, tag + 'norm_mlp')
    pre, r = matmul(h3, W['mlp_w1'], 'nn', tm=tm, tn=512, tk=D, act='relu2', name=tag + 'mlp_w1')
    x3 = matmul(r, W['mlp_w2'], 'nn', tm=tm, tn=512, tk=2048, res=x2, name=tag + 'mlp_w2')
    sv.update(h1=h1, z=z, a_pre=a_pre, a_cv=a_cv, a=a, qc=qc, kc=kc, vc=vc, ge=ge, be=be, o=o, states=states, c=c,
              cs=cs, p=p, merged=merged, x1=x1, kv=kv, x2=x2, h3=h3, pre=pre, r=r)
    return x3, sv


def layer_bwd(dx, mem, W, sv, tag):
    S, D = dx.shape
    T = min(S, 256)
    tm = _tiles(S)
    tag = tag + 'b_'
    G = {}
    z = sv['z']
    da = matmul(dx, W['mlp_w2'], 'nt', tm=tm, tn=512, tk=D, gate=sv['pre'], out_dtype=BF16, name=tag + 'mlp_da')
    G['mlp_w2'] = matmul(sv['r'], dx, 'tn', tm=512, tn=512, tk=tm, name=tag + 'mlp_gw2')
    dh3 = matmul(da, W['mlp_w1'], 'nt', tm=tm, tn=512, tk=2048, name=tag + 'mlp_dh')
    G['mlp_w1'] = matmul(sv['h3'], da, 'tn', tm=512, tn=512, tk=tm, name=tag + 'mlp_gw1')
    (dx2,), (G['norm_mlp'],) = rows_bwd(f_norm_res, [(sv['x2'], D, 0)], [W['norm_mlp']], 'p', [dh3, dx], [F32], T,
                                        tag + 'norm_mlp')
    (dx1,), (dkv, G['norm_xa'], G['xa_wq'], G['xa_wo']) = rows_bwd(
        f_xattn, [(sv['x1'], D, 0)], [sv['kv'], W['norm_xa'], W['xa_wq'], W['xa_wo']], 'ppww', [dx2], [F32],
        min(S, 128), tag + 'xattn')
    _, (G['norm_mem'], G['xa_wkv']) = rows_bwd(f_kv, [(mem, D, 0)], [W['norm_mem'], W['xa_wkv']], 'pw', [dkv], [None],
                                               mem.shape[0], tag + 'kv')
    dmerged = matmul(dx1, W['w_out'], 'nt', tm=tm, tn=512, tk=D, out_dtype=BF16, name=tag + 'dmerged')
    G['w_out'] = matmul(sv['merged'], dx1, 'tn', tm=512, tn=512, tk=tm, name=tag + 'gw_out')
    (d_a, d_o, d_c, d_p, dz_gate), (G['w_branch'],) = rows_bwd(
        f_merge, [(sv['a'], 512, 0), (sv['o'], 512, 0), (sv['c'], 512, 0), (sv['p'], 512, 0), (z, 4 * D, ZB_GATE)],
        [W['w_branch']], 'w', [dmerged], [F32, F32, F32, F32, BF16], min(S, 128), tag + 'merge')
    (dcs, dpx), (G['pool_w'], G['pool_scale']) = rows_bwd(
        f_pool, [(sv['cs'], 512, 0), (z, 512, ZB_POOL)], [W['pool_w'], W['pool_scale']], 'pp', [d_p], [F32, F32], T,
        tag + 'pool')
    dz_pool, _ = conv_bwd(z, ZB_POOL, _pool_taps(), dcs, tag + 'pool_sum', add=dpx, out_dtype=BF16)
    gm_params = [W['gm_ln_g'], W['gm_ln_b'], W['gm_ws']] + W['gm_b']
    (dz_gu, dz_gv), gm_g = rows_bwd(f_gmlp, [(z, 512, ZB_GU), (z, 512, ZB_GV)], gm_params, 'p' * 7, [d_c],
                                    [BF16, BF16], T, tag + 'gmlp')
    G['gm_ln_g'], G['gm_ln_b'], G['gm_ws'] = gm_g[:3]
    G['gm_b'] = list(gm_g[3:])
    dqc, dkc, dvc, dge, dbe, dz_dgate, G['dn_norm_g'] = delta_bwd(
        sv['qc'], sv['kc'], sv['vc'], z, sv['ge'], sv['be'], W['dn_norm_g'], sv['states'], d_o, tag + 'delta')
    (dz_bd,), (G['alp'], G['dtp']) = rows_bwd(f_gbeta, [(z, 128, ZB128_BD)], [W['alp'], W['dtp']], 'pp', [dge, dbe],
                                              [BF16], T, tag + 'gbeta')
    dz_q, G['dn_wq'] = conv_bwd(z, ZB_Q, W['dn_wq'], dqc, tag + 'conv_q', out_dtype=BF16)
    dz_k, G['dn_wk'] = conv_bwd(z, ZB_K, W['dn_wk'], dkc, tag + 'conv_k', out_dtype=BF16)
    dz_v, G['dn_wv'] = conv_bwd(z, ZB_V, W['dn_wv'], dvc, tag + 'conv_v', out_dtype=BF16)
    (da_cv,), (G['conv_a_b'], G['ln_a_g'], G['ln_a_b']) = rows_bwd(
        f_lnsilu, [(sv['a_cv'], 512, 0)], [W['conv_a_b'], W['ln_a_g'], W['ln_a_b']], 'ppp', [d_a], [F32], T,
        tag + 'ln_a')
    da_pre, G['conv_a_w'] = conv_bwd(sv['a_pre'], 0, W['conv_a_w'], da_cv, tag + 'conv_a')
    (dz_a1, dz_a2), _ = rows_bwd(f_glu, [(z, 512, ZB_A1), (z, 512, ZB_A2)], [], '', [da_pre], [BF16, BF16], T,
                                 tag + 'glu')
    dz = jnp.concatenate([dz_gate, dz_a1, dz_a2, dz_q, dz_k, dz_v, dz_dgate, dz_gu, dz_gv, dz_pool, dz_bd], axis=1)
    dh1 = matmul(dz, W['w_in'], 'nt', tm=tm, tn=D, tk=384, name=tag + 'dh1')
    G['w_in'] = matmul(sv['h1'], dz, 'tn', tm=512, tn=384, tk=tm, name=tag + 'gw_in')
    (dx0,), (G['norm_mix'],) = rows_bwd(f_norm_res, [(sv['x'], D, 0)], [W['norm_mix']], 'p', [dh1, dx1], [F32], T,
                                        tag + 'norm_mix')
    return dx0, G


def local_step(x, mem, tgt, layers, norm_f):
    saved = []
    for l, W in enumerate(layers):
        x, sv = layer_fwd(x, mem, W, f'l{l}_')
        saved.append(sv)
    loss, dx, g_nf = loss_head(x, tgt, norm_f, min(x.shape[0], 256), 'loss_head')
    grads = [None] * len(layers)
    for l in reversed(range(len(layers))):
        dx, grads[l] = layer_bwd(dx, mem, layers[l], saved[l], f'l{l}_')
    return loss, dx, grads, g_nf


def _row(v):
    return v.reshape(1, -1)


def _lane_pad(v):
    return jnp.zeros((8, 128), F32).at[0, :v.shape[0]].set(v)


def layer_weights(full, l):
    w_in = full['w_in'][l]
    cols = [w_in[:, a:b] for a, b in Z_ORDER]
    cols.append(jnp.zeros((w_in.shape[0], Z_W - sum(b - a for a, b in Z_ORDER)), w_in.dtype))
    dn_w = full['dn_conv_w'][l]
    W = {n: _row(full[n][l]) for n in ('norm_mix', 'conv_a_b', 'ln_a_g', 'ln_a_b', 'dn_norm_g', 'gm_ln_g', 'gm_ln_b',
                                       'pool_scale', 'norm_xa', 'norm_mem', 'norm_mlp')}
    W.update(w_in=jnp.concatenate(cols, axis=1), conv_a_w=full['conv_a_w'][l],
             dn_wq=dn_w[:, :MIX_W], dn_wk=dn_w[:, MIX_W:2 * MIX_W], dn_wv=dn_w[:, 2 * MIX_W:],
             alp=_lane_pad(full['dn_a_log'][l]), dtp=_lane_pad(full['dn_dt_bias'][l]),
             gm_ws=full['gm_ws'][l], gm_b=[full['gm_bs'][l][g].reshape(GM_CHUNK, 1) for g in range(GM_GROUPS)],
             pool_w=full['pool_w'][l])
    for n in ('w_branch', 'w_out', 'xa_wq', 'xa_wkv', 'xa_wo', 'mlp_w1', 'mlp_w2'):
        W[n] = full[n][l]
    return W


def layer_grads(G):
    g_in = G['w_in']
    starts = np.cumsum([0] + [b - a for a, b in Z_ORDER])
    pieces = sorted(zip(Z_ORDER, starts[:-1]))
    out = {n: G[n].reshape(-1) for n in ('norm_mix', 'conv_a_b', 'ln_a_g', 'ln_a_b', 'dn_norm_g', 'gm_ln_g', 'gm_ln_b',
                                          'pool_scale', 'norm_xa', 'norm_mem', 'norm_mlp')}
    out.update(w_in=jnp.concatenate([g_in[:, s:s + b - a] for (a, b), s in pieces], axis=1),
               conv_a_w=G['conv_a_w'], dn_conv_w=jnp.concatenate([G['dn_wq'], G['dn_wk'], G['dn_wv']], axis=1),
               dn_a_log=G['alp'][0, :DN_HEADS], dn_dt_bias=G['dtp'][0, :DN_HEADS], gm_ws=G['gm_ws'],
               gm_bs=jnp.stack([b.reshape(-1) for b in G['gm_b']]), pool_w=G['pool_w'])
    for n in ('w_branch', 'w_out', 'xa_wq', 'xa_wkv', 'xa_wo', 'mlp_w1', 'mlp_w2'):
        out[n] = G[n]
    return out


COLUMN_SHARDED = ('w_in', 'conv_a_w', 'dn_conv_w', 'w_branch', 'xa_wkv', 'mlp_w1')


def _unshard(name, g):
    if name in COLUMN_SHARDED:
        t = jnp.moveaxis(g, 0, -2)
        return t.reshape(t.shape[:-2] + (t.shape[-2] * t.shape[-1],))
    t = jnp.moveaxis(g, 0, 1)
    return t.reshape((t.shape[0], t.shape[1] * t.shape[2]) + t.shape[3:])


def _shard_rows(name, g):
    if name in COLUMN_SHARDED:
        t = g.reshape(g.shape[:-1] + (8, g.shape[-1] // 8))
        return jnp.moveaxis(t, -2, 0)
    t = g.reshape((g.shape[0], 8, g.shape[1] // 8) + g.shape[2:])
    return jnp.moveaxis(t, 1, 0)


def _pack(arrays):
    flat = jnp.concatenate([a.reshape(-1) for a in arrays])
    rows = -(-flat.shape[0] // 1024) * 8
    return jnp.pad(flat, (0, rows * 128 - flat.shape[0])).reshape(rows, 128)


def _unpack(packed, like):
    flat = packed.reshape(-1)
    out, at = [], 0
    for a in like:
        out.append(flat[at:at + a.size].reshape(a.shape))
        at += a.size
    return out


def kernel(x, mem, norm_mix, w_in, conv_a_w, conv_a_b, ln_a_g, ln_a_b, dn_conv_w, dn_a_log, dn_dt_bias, dn_norm_g, gm_ln_g, gm_ln_b, gm_ws, gm_bs, pool_w, pool_scale, w_branch, w_out, norm_xa, norm_mem, xa_wq, xa_wkv, xa_wo, norm_mlp, mlp_w1, mlp_w2, norm_f, loss_target, m_norm_mix, m_w_in, m_conv_a_w, m_conv_a_b, m_ln_a_g, m_ln_a_b, m_dn_conv_w, m_dn_a_log, m_dn_dt_bias, m_dn_norm_g, m_gm_ln_g, m_gm_ln_b, m_gm_ws, m_gm_bs, m_pool_w, m_pool_scale, m_w_branch, m_w_out, m_norm_xa, m_norm_mem, m_xa_wq, m_xa_wkv, m_xa_wo, m_norm_mlp, m_mlp_w1, m_mlp_w2, m_norm_f, v_norm_mix, v_w_in, v_conv_a_w, v_conv_a_b, v_ln_a_g, v_ln_a_b, v_dn_conv_w, v_dn_a_log, v_dn_dt_bias, v_dn_norm_g, v_gm_ln_g, v_gm_ln_b, v_gm_ws, v_gm_bs, v_pool_w, v_pool_scale, v_w_branch, v_w_out, v_norm_xa, v_norm_mem, v_xa_wq, v_xa_wkv, v_xa_wo, v_norm_mlp, v_mlp_w1, v_mlp_w2, v_norm_f):
    args = locals()
    w = {n: args[n] for n in WEIGHTS}
    m = {n: args['m_' + n] for n in WEIGHTS}
    v = {n: args['v_' + n] for n in WEIGHTS}

    gathered = all_gather([w[n].astype(BF16) if n in SENT_AS_BF16 else w[n] for n in SHARDED], 'gather_weights')
    full = {n: w[n] for n in REPLICATED}
    for n, g in zip(SHARDED, gathered):
        full[n] = _unshard(n, g)
    layers = [layer_weights(full, l) for l in range(DEPTH)]
    loss, grad_x, grads, g_nf = local_step(x[0], mem[0], loss_target[0], layers, _row(norm_f))
    per_layer = [layer_grads(g) for g in grads]
    gfull = {n: jnp.stack([g[n] for g in per_layer]) for n in WEIGHTS if n != 'norm_f'}
    gfull['norm_f'] = g_nf.reshape(-1)

    out = {}
    pairs = exchange_cores([_shard_rows(n, gfull[n]) for n in SHARDED], 'reduce_cores')
    sums = [add_pairs(_as2d(p, 2), 'reduce_add_' + n) for n, p in zip(SHARDED, pairs)]
    parts = exchange_chips(sums, 'reduce_chips')
    for n, p in zip(SHARDED, parts):
        res = adamw_sum(p, _as2d(w[n]), _as2d(m[n]), _as2d(v[n]), 'adamw_' + n)
        out[n] = [r.reshape(w[n].shape) for r in res]
    (partials,) = all_gather([_pack([gfull[n] for n in REPLICATED])], 'gather_small_grads')
    res = adamw_sum(partials, _pack([w[n] for n in REPLICATED]), _pack([m[n] for n in REPLICATED]),
                    _pack([v[n] for n in REPLICATED]), 'adamw_small')
    like = [w[n] for n in REPLICATED]
    for k, r in enumerate(res):
        for n, a in zip(REPLICATED, _unpack(r, like)):
            out.setdefault(n, [None] * 4)[k] = a

    total = lax.psum(loss[0, 0], ('x', 'y', 'c'))
    return (total, grad_x[None], *[out[n][0] for n in WEIGHTS], *[out[n][1] for n in WEIGHTS],
            *[out[n][2] for n in WEIGHTS], *[out[n][3] for n in WEIGHTS])
```

```python
import functools
import math

import numpy as np
import jax
import jax.numpy as jnp
from jax import lax
from jax.experimental import pallas as pl
from jax.experimental.pallas import tpu as pltpu

F32 = jnp.float32
BF16 = jnp.bfloat16
HIGH = lax.Precision.HIGH
MESH_ID = pl.DeviceIdType.MESH
VMEM_LIMIT_V7X = 56 << 20

DEPTH = 4
MIX_W = 512
DN_HEADS = 4
DN_HEAD_DIM = 128
DN_CHUNK = 64
GM_CHUNK = 128
GM_GROUPS = 4
POOL_WINDOWS = (2, 4, 8, 16)
XA_HEADS = 4
CONV_PAD = 32

ADAM_LR = 0.001
ADAM_B1 = 0.9
ADAM_B2 = 0.999
ADAM_EPS = 1e-08
ADAM_WD = 0.01
ADAM_STEP = 10

WEIGHTS = ['norm_mix', 'w_in', 'conv_a_w', 'conv_a_b', 'ln_a_g', 'ln_a_b', 'dn_conv_w', 'dn_a_log', 'dn_dt_bias',
           'dn_norm_g', 'gm_ln_g', 'gm_ln_b', 'gm_ws', 'gm_bs', 'pool_w', 'pool_scale', 'w_branch', 'w_out',
           'norm_xa', 'norm_mem', 'xa_wq', 'xa_wkv', 'xa_wo', 'norm_mlp', 'mlp_w1', 'mlp_w2', 'norm_f']
SHARDED = ['w_in', 'conv_a_w', 'dn_conv_w', 'w_branch', 'w_out', 'xa_wq', 'xa_wkv', 'xa_wo', 'mlp_w1', 'mlp_w2']
SENT_AS_BF16 = ['w_in', 'w_branch', 'w_out', 'xa_wq', 'xa_wkv', 'xa_wo', 'mlp_w1', 'mlp_w2']
REPLICATED = [n for n in WEIGHTS if n not in SHARDED]

Z_W = 8832
Z_ORDER = ((4616, 8712), (0, 3072), (3080, 4616), (3072, 3080))
ZB_GATE, ZB_A1, ZB_A2, ZB_Q, ZB_K, ZB_V, ZB_DGATE, ZB_GU, ZB_GV, ZB_POOL = 0, 8, 9, 10, 11, 12, 13, 14, 15, 16
ZB128_BD = 68


def _cparams(*sem):
    return pltpu.CompilerParams(dimension_semantics=sem, vmem_limit_bytes=VMEM_LIMIT_V7X)


def _dot(a, b, ca, cb):
    return lax.dot_general(a.astype(BF16), b.astype(BF16), (((ca,), (cb,)), ((), ())), preferred_element_type=F32)


def _doth(a, b, ca, cb):
    return lax.dot_general(a, b, (((ca,), (cb,)), ((), ())), precision=HIGH, preferred_element_type=F32)


def _make_mm(dot):
    @jax.custom_vjp
    def nn(a, b):
        return dot(a, b, 1, 0)
    nn.defvjp(lambda a, b: (dot(a, b, 1, 0), (a, b)), lambda r, g: (dot(g, r[1], 1, 1), dot(r[0], g, 0, 0)))

    @jax.custom_vjp
    def nt(a, b):
        return dot(a, b, 1, 1)
    nt.defvjp(lambda a, b: (dot(a, b, 1, 1), (a, b)), lambda r, g: (dot(g, r[1], 1, 0), dot(g, r[0], 0, 0)))

    @jax.custom_vjp
    def tn(a, b):
        return dot(a, b, 0, 0)
    tn.defvjp(lambda a, b: (dot(a, b, 0, 0), (a, b)), lambda r, g: (dot(r[1], g, 1, 1), dot(r[0], g, 1, 0)))
    return nn, nt, tn


mm, mm_nt, mm_tn = _make_mm(_dot)
mmh, mmh_nt, mmh_tn = _make_mm(_doth)


@jax.custom_vjp
def _mmw(a, w, wz):
    return _dot(a, w, 1, 0)


_mmw.defvjp(lambda a, w, wz: (_dot(a, w, 1, 0), (a, w)),
            lambda r, g: (_dot(g, r[1], 1, 1), jnp.zeros_like(r[1]), _dot(r[0], g, 0, 0)))


def mmw(a, wpair):
    w, wz = wpair
    return _dot(a, w, 1, 0) if wz is None else _mmw(a, w, wz)


def wsel(wpair, n):
    return (wpair[0][n], None if wpair[1] is None else wpair[1][n])


def _sigmoid(x):
    return 1.0 / (1.0 + jnp.exp(-x))


def _silu(x):
    return x * _sigmoid(x)


def _rms(x, g, eps=1e-6):
    return x * lax.rsqrt(jnp.mean(x * x, axis=-1, keepdims=True) + eps) * g


def _ln(x, g, b, eps=1e-5):
    mu = jnp.mean(x, axis=-1, keepdims=True)
    d = x - mu
    return d * lax.rsqrt(jnp.mean(d * d, axis=-1, keepdims=True) + eps) * g + b


def _gelu(x):
    return 0.5 * x * (1.0 + lax.erf(x * (2.0 ** -0.5)))


def _softplus(x):
    return jnp.maximum(x, 0.0) + jnp.log(1.0 + jnp.exp(-jnp.abs(x)))


def _row_spec(T, width, cb):
    return pl.BlockSpec((T, width), lambda i: (i, cb))


def _whole_spec(p):
    nd = p.ndim
    return pl.BlockSpec(p.shape, lambda i: (0,) * nd)


def _load_params(refs, kinds, with_zeros):
    out = []
    for r, k in zip(refs, kinds):
        if k == 'w':
            out.append((r[...], jnp.zeros(r.shape, F32) if with_zeros else None))
        else:
            out.append(r[...].astype(F32))
    return out


def rows_fwd(f, rows, params, kinds, outs, T, name):
    S = rows[0][0].shape[0]
    nr, npar = len(rows), len(params)

    def body(*refs):
        r = [x[...].astype(F32) for x in refs[:nr]]
        p = _load_params(refs[nr:nr + npar], kinds, False)
        res = f(*r, *p)
        for o_ref, o in zip(refs[nr + npar:], res):
            o_ref[...] = o.astype(o_ref.dtype)

    return pl.pallas_call(
        body, grid=(S // T,), name=name,
        in_specs=[_row_spec(T, w, cb) for _, w, cb in rows] + [_whole_spec(p) for p in params],
        out_specs=[_row_spec(T, w, 0) for w, _ in outs],
        out_shape=[jax.ShapeDtypeStruct((S, w), dt) for w, dt in outs],
        compiler_params=_cparams("parallel"),
    )(*[a for a, _, _ in rows], *params)


def rows_bwd(f, rows, params, kinds, cts, row_dtypes, T, name):
    S = rows[0][0].shape[0]
    nr, npar, nc = len(rows), len(params), len(cts)
    want = [i for i, dt in enumerate(row_dtypes) if dt is not None]

    def body(*refs):
        r = [x[...].astype(F32) for x in refs[:nr]]
        p = _load_params(refs[nr:nr + npar], kinds, True)
        g = [x[...].astype(F32) for x in refs[nr + npar:nr + npar + nc]]
        d_rows = refs[nr + npar + nc:nr + npar + nc + len(want)]
        d_params = refs[nr + npar + nc + len(want):]
        _, vjp = jax.vjp(f, *r, *p)
        grads = vjp(tuple(g))
        for o_ref, i in zip(d_rows, want):
            o_ref[...] = grads[i].astype(o_ref.dtype)
        first = pl.program_id(0) == 0
        for o_ref, gp, k in zip(d_params, grads[nr:], kinds):
            gp = gp[1] if k == 'w' else gp

            @pl.when(first)
            def _():
                o_ref[...] = gp

            @pl.when(jnp.logical_not(first))
            def _():
                o_ref[...] += gp

    res = pl.pallas_call(
        body, grid=(S // T,), name=name,
        in_specs=([_row_spec(T, w, cb) for _, w, cb in rows] + [_whole_spec(p) for p in params]
                  + [_row_spec(T, c.shape[1], 0) for c in cts]),
        out_specs=[_row_spec(T, rows[i][1], 0) for i in want] + [_whole_spec(p) for p in params],
        out_shape=([jax.ShapeDtypeStruct((S, rows[i][1]), row_dtypes[i]) for i in want]
                   + [jax.ShapeDtypeStruct(p.shape, F32) for p in params]),
        compiler_params=_cparams("arbitrary"),
    )(*[a for a, _, _ in rows], *params, *cts)
    return res[:len(want)], res[len(want):]


def f_norm(x, g):
    return (_rms(x, g),)


def f_glu(a1, a2):
    return (a1 * _sigmoid(a2),)


def f_lnsilu(cv, cb, g, b):
    return (_silu(_ln(cv + cb, g, b)),)


def f_gbeta(bd, alp, dtp):
    j = lax.broadcasted_iota(jnp.int32, (128, MIX_W), 0)
    head = lax.broadcasted_iota(jnp.int32, (128, MIX_W), 1) // DN_HEAD_DIM
    e_lo = (j == head).astype(F32)
    e_hi = (j == head + DN_HEADS).astype(F32)
    beta = _sigmoid(mmh(bd, e_lo))
    a_log = jnp.sum(mmh(alp, e_lo), axis=0, keepdims=True)
    dt_bias = jnp.sum(mmh(dtp, e_lo), axis=0, keepdims=True)
    g = -jnp.exp(a_log) * _softplus(mmh(bd, e_hi) + dt_bias)
    return g, beta


def f_gmlp(u_in, v_in, lg, lb, ws, b0, b1, b2, b3):
    T = u_in.shape[0]
    u = _gelu(u_in)
    vg = _ln(_gelu(v_in), lg, lb)
    tril = (lax.broadcasted_iota(jnp.int32, (GM_CHUNK, GM_CHUNK), 0)
            >= lax.broadcasted_iota(jnp.int32, (GM_CHUNK, GM_CHUNK), 1))
    bias = (b0, b1, b2, b3)
    chunks = []
    for r in range(T // GM_CHUNK):
        vr = vg[r * GM_CHUNK:(r + 1) * GM_CHUNK]
        cols = []
        for gi in range(GM_GROUPS):
            w = jnp.where(tril, ws[gi], 0.0)
            cols.append(mm(w, vr[:, gi * 128:(gi + 1) * 128]) + bias[gi])
        chunks.append(jnp.concatenate(cols, axis=1))
    mixed = chunks[0] if len(chunks) == 1 else jnp.concatenate(chunks, axis=0)
    return (u * mixed,)


def f_pool(cs, xin, pw, scale):
    T = cs.shape[0]
    t = pl.program_id(0) * T + lax.broadcasted_iota(jnp.int32, (T, 128), 0)
    cols = []
    for gi, win in enumerate(POOL_WINDOWS):
        count = jnp.minimum(t + 1, win).astype(F32)
        sl = slice(gi * 128, (gi + 1) * 128)
        cols.append(mm(cs[:, sl] / count - xin[:, sl], pw[gi]))
    return (jnp.concatenate(cols, axis=1) * scale,)


def f_merge(a, o, c, p, gate, wb):
    D = gate.shape[1] // 4
    acc = None
    for n, br in enumerate((a, o, c, p)):
        term = _sigmoid(gate[:, n * D:(n + 1) * D]) * mmw(br, wsel(wb, n))
        acc = term if acc is None else acc + term
    return (acc,)


def f_kv(mem, nm, wkv):
    return (mmw(_rms(mem, nm), wkv),)


def f_xattn(x, kv, nx, wq, wo):
    D = x.shape[1]
    hd = D // XA_HEADS
    q = mmw(_rms(x, nx), wq)
    heads = []
    for h in range(XA_HEADS):
        s = mm_nt(q[:, h * hd:(h + 1) * hd], kv[:, h * hd:(h + 1) * hd]) * (hd ** -0.5)
        s = s - jnp.max(s, axis=-1, keepdims=True)
        e = jnp.exp(s)
        pr = e / jnp.sum(e, axis=-1, keepdims=True)
        heads.append(mm(pr, kv[:, D + h * hd:D + (h + 1) * hd]))
    return (x + mmw(jnp.concatenate(heads, axis=1), wo),)


def f_loss(x, tgt, nf):
    err = _rms(x, nf) - tgt
    return (0.5 * jnp.sum(jnp.mean(err * err, axis=-1, keepdims=True), axis=0, keepdims=True),)


def _mm_tiles(mode, M, N, K):
    wide = 512 if N % 512 == 0 else 384
    if mode == 'tn':
        return min(M, 512), wide, K
    tm = min(M, 1024)
    if K <= 1024:
        return tm, wide, K
    if K % 2048 == 0:
        return tm, min(N, 1024) if mode == 'nt' else wide, 2048
    return tm, min(N, 1024), 384


def matmul(a, b, mode, *, name, out_dtype=F32, res=None, act=None, gate=None):
    M, K = a.shape if mode != 'tn' else a.shape[::-1]
    tm, tn, tk = _mm_tiles(mode, M, b.shape[0] if mode == 'nt' else b.shape[1], K)
    if mode == 'nn':
        (M, K), N = a.shape, b.shape[1]
        a_spec = pl.BlockSpec((tm, tk), lambda i, j, k: (i, k))
        b_spec = pl.BlockSpec((tk, tn), lambda i, j, k: (k, j))
        ca, cb = 1, 0
    elif mode == 'nt':
        (M, K), N = a.shape, b.shape[0]
        a_spec = pl.BlockSpec((tm, tk), lambda i, j, k: (i, k))
        b_spec = pl.BlockSpec((tn, tk), lambda i, j, k: (j, k))
        ca, cb = 1, 1
    else:
        (K, M), N = a.shape, b.shape[1]
        a_spec = pl.BlockSpec((tk, tm), lambda i, j, k: (k, i))
        b_spec = pl.BlockSpec((tk, tn), lambda i, j, k: (k, j))
        ca, cb = 0, 0
    assert M % tm == 0 and N % tn == 0 and K % tk == 0, (a.shape, b.shape, mode, tm, tn, tk)
    nk = K // tk
    o_spec = pl.BlockSpec((tm, tn), lambda i, j, k: (i, j))
    extra = [e for e in (res, gate) if e is not None]

    def body(*refs):
        a_ref, b_ref = refs[:2]
        e_refs = refs[2:2 + len(extra)]
        o_refs = refs[2 + len(extra):2 + len(extra) + (2 if act else 1)]
        part = _dot(a_ref[...], b_ref[...], ca, cb)

        def finish(acc):
            if res is not None:
                acc = acc + e_refs[0][...]
            if gate is not None:
                acc = acc * (2.0 * jnp.maximum(e_refs[-1][...], 0.0))
            o_refs[0][...] = acc.astype(o_refs[0].dtype)
            if act:
                r = jnp.maximum(acc, 0.0)
                o_refs[1][...] = (r * r).astype(o_refs[1].dtype)

        if nk == 1:
            finish(part)
        else:
            acc_ref = refs[-1]
            k = pl.program_id(2)

            @pl.when(k == 0)
            def _():
                acc_ref[...] = part

            @pl.when(k > 0)
            def _():
                acc_ref[...] += part

            @pl.when(k == nk - 1)
            def _():
                finish(acc_ref[...])

    out_shape = [jax.ShapeDtypeStruct((M, N), out_dtype)]
    if act:
        out_shape.append(jax.ShapeDtypeStruct((M, N), BF16))
    res_ = pl.pallas_call(
        body, grid=(M // tm, N // tn, nk), name=name,
        in_specs=[a_spec, b_spec] + [o_spec] * len(extra),
        out_specs=[o_spec] * len(out_shape), out_shape=out_shape,
        scratch_shapes=[pltpu.VMEM((tm, tn), F32)] if nk > 1 else [],
        compiler_params=_cparams("parallel", "parallel", "arbitrary"),
    )(a, b, *extra)
    return res_ if act else res_[0]


def _conv_rows(S):
    return min(S, 512)


def conv_fwd(x, cb0, w, name):
    S = x.shape[0]
    K = w.shape[0]
    R = _conv_rows(S)

    def body(x_ref, w_ref, y_ref, pad_ref):
        pad_ref[pl.ds(0, CONV_PAD), :] = jnp.zeros((CONV_PAD, 128), F32)
        pad_ref[pl.ds(CONV_PAD, S), :] = x_ref[...]
        wv = w_ref[...]

        def chunk(r, carry):
            r0 = pl.multiple_of(r * R, R)
            win = pad_ref[pl.ds(r0, R + CONV_PAD), :]
            acc = jnp.zeros((R, 128), F32)
            for s in range(K):
                sh = win if s == 0 else pltpu.roll(win, s, 0)
                acc = acc + sh[CONV_PAD:, :] * wv[K - 1 - s:K - s, :]
            y_ref[pl.ds(r0, R), :] = acc
            return carry

        lax.fori_loop(0, S // R, chunk, 0)

    return pl.pallas_call(
        body, grid=(4,), name=name,
        in_specs=[pl.BlockSpec((S, 128), lambda j: (0, cb0 * 4 + j)), pl.BlockSpec((K, 128), lambda j: (0, j))],
        out_specs=pl.BlockSpec((S, 128), lambda j: (0, j)),
        out_shape=jax.ShapeDtypeStruct((S, MIX_W), F32),
        scratch_shapes=[pltpu.VMEM((S + CONV_PAD, 128), F32)],
        compiler_params=_cparams("parallel"),
    )(x, w)


def conv_bwd(x, cb0, w, dy, name, add=None, out_dtype=F32):
    S = x.shape[0]
    K = w.shape[0]
    R = _conv_rows(S)
    W = R + CONV_PAD

    def body(*refs):
        x_ref, w_ref, dy_ref = refs[:3]
        add_ref = refs[3] if add is not None else None
        dx_ref, dw_ref, xpad_ref, dypad_ref = refs[-4:]
        xpad_ref[pl.ds(0, CONV_PAD), :] = jnp.zeros((CONV_PAD, 128), F32)
        xpad_ref[pl.ds(CONV_PAD, S), :] = x_ref[...]
        dypad_ref[pl.ds(S, CONV_PAD), :] = jnp.zeros((CONV_PAD, 128), F32)
        dypad_ref[pl.ds(0, S), :] = dy_ref[...].astype(F32)
        dw_ref[...] = jnp.zeros((K, 128), F32)
        wv = w_ref[...]

        def chunk(r, carry):
            r0 = pl.multiple_of(r * R, R)
            xwin = xpad_ref[pl.ds(r0, W), :]
            dwin = dypad_ref[pl.ds(r0, W), :]
            dyc = dwin[:R, :]
            acc = jnp.zeros((R, 128), F32)
            for s in range(K):
                up = dwin if s == 0 else pltpu.roll(dwin, W - s, 0)
                acc = acc + up[:R, :] * wv[K - 1 - s:K - s, :]
                xs = xwin if s == 0 else pltpu.roll(xwin, s, 0)
                dw_ref[pl.ds(K - 1 - s, 1), :] += jnp.sum(dyc * xs[CONV_PAD:, :], axis=0, keepdims=True)
            if add_ref is not None:
                acc = acc + add_ref[pl.ds(r0, R), :].astype(F32)
            dx_ref[pl.ds(r0, R), :] = acc.astype(dx_ref.dtype)
            return carry

        lax.fori_loop(0, S // R, chunk, 0)

    col = pl.BlockSpec((S, 128), lambda j: (0, j))
    ins = [x, w, dy] + ([add] if add is not None else [])
    return pl.pallas_call(
        body, grid=(4,), name=name,
        in_specs=[pl.BlockSpec((S, 128), lambda j: (0, cb0 * 4 + j)), pl.BlockSpec((K, 128), lambda j: (0, j)), col]
        + ([col] if add is not None else []),
        out_specs=[col, pl.BlockSpec((K, 128), lambda j: (0, j))],
        out_shape=[jax.ShapeDtypeStruct((S, MIX_W), out_dtype), jax.ShapeDtypeStruct((K, MIX_W), F32)],
        scratch_shapes=[pltpu.VMEM((S + CONV_PAD, 128), F32), pltpu.VMEM((S + CONV_PAD, 128), F32)],
        compiler_params=_cparams("parallel"),
    )(*ins)


DN_PREP_ROWS = 2 * DN_CHUNK


def _unit_lower_inverse(a):
    C = a.shape[0]
    eye = (lax.broadcasted_iota(jnp.int32, (C, C), 0) == lax.broadcasted_iota(jnp.int32, (C, C), 1)).astype(F32)
    inv = eye - a
    pw = a
    for _ in range(5):
        pw = mmh(pw, pw)
        inv = inv + mmh(inv, pw)
    return inv


@jax.custom_vjp
def _known_inverse(a, inv):
    return inv


_known_inverse.defvjp(lambda a, inv: (inv, inv),
                      lambda inv, g: (-mmh_nt(mmh_tn(inv, g), inv), jnp.zeros_like(inv)))


def _delta_prep(qc, kc, vc, ge, be, inv_known=None):
    C, Dh = DN_CHUNK, DN_HEAD_DIM
    ii = lax.broadcasted_iota(jnp.int32, (C, C), 0)
    jj = lax.broadcasted_iota(jnp.int32, (C, C), 1)
    causal, strict = ii >= jj, ii > jj
    tri = causal.astype(F32)
    us, ws, qds, kds, scs, attns, invs = [], [], [], [], [], [], []
    for h in range(DN_HEADS):
        sl = slice(h * Dh, (h + 1) * Dh)
        q, k, v = _silu(qc[:, sl]), _silu(kc[:, sl]), _silu(vc[:, sl])
        q = q * lax.rsqrt(jnp.sum(q * q, axis=-1, keepdims=True) + 1e-6) * (Dh ** -0.5)
        k = k * lax.rsqrt(jnp.sum(k * k, axis=-1, keepdims=True) + 1e-6)
        g, beta = ge[:, sl], be[:, sl]
        gam = mmh(tri, g)
        g_last = mmh(jnp.ones((C, C), F32), g)
        gam_col = gam[:, :C]
        gam_row = mmh_nt(jnp.full((C, Dh), 1.0 / Dh, F32), gam)
        decay = jnp.where(causal, jnp.exp(jnp.where(causal, gam_col - gam_row, 0.0)), 0.0)
        kb = k * beta
        a = jnp.where(strict, mm_nt(kb, k) * decay, 0.0)
        inv = _unit_lower_inverse(a) if inv_known is None else _known_inverse(a, inv_known[h])
        e_gam = jnp.exp(gam)
        us.append(mmh(inv, v * beta))
        ws.append(mmh(inv, kb * e_gam))
        qds.append(q * e_gam)
        kds.append(k * jnp.exp(g_last - gam))
        scs.append(jnp.exp(g_last))
        attns.append(mm_nt(q, k) * decay)
        invs.append(inv)
    cat = lambda parts: jnp.concatenate(parts, axis=1)
    return cat(us), cat(ws), cat(qds), cat(kds), cat(scs), attns, invs


def _delta_step(states, u, w, qd, kd, sc, attns, gate, ng):
    Dh = DN_HEAD_DIM
    new_states, outs = [], []
    for h in range(DN_HEADS):
        sl = slice(h * Dh, (h + 1) * Dh)
        s_in = states[h]
        v_new = u[:, sl] - mm(w[:, sl], s_in)
        o = mm(qd[:, sl], s_in) + mm(attns[h], v_new)
        new_states.append(s_in * jnp.concatenate([sc[:, sl], sc[:, sl]], axis=0) + mm_tn(kd[:, sl], v_new))
        outs.append(_rms(o, ng) * _silu(gate[:, sl]))
    return new_states, jnp.concatenate(outs, axis=1)


def _prep_rows(fn, blocks, inv_blocks=None):
    n = blocks[0].shape[0] // DN_CHUNK
    res = []
    for r in range(n):
        rows = slice(r * DN_CHUNK, (r + 1) * DN_CHUNK)
        known = None if inv_blocks is None else [iv[rows] for iv in inv_blocks]
        res.append(fn(*[b[rows] for b in blocks], known))
    if n == 1:
        return res[0]
    return jax.tree.map(lambda *parts: jnp.concatenate(parts, axis=0), *res)


def delta_prep_fwd(qc, kc, vc, ge, be, name):
    S = qc.shape[0]
    R = min(S, DN_PREP_ROWS)

    def body(q_ref, k_ref, v_ref, ge_ref, be_ref, u_ref, w_ref, qd_ref, kd_ref, sc_ref, at_ref, iv_ref):
        u, w, qd, kd, sc, attns, invs = _prep_rows(
            _delta_prep, [q_ref[...], k_ref[...], v_ref[...], ge_ref[...], be_ref[...]])
        for ref, val in zip((u_ref, w_ref, qd_ref, kd_ref, sc_ref), (u, w, qd, kd, sc)):
            ref[...] = val
        for h in range(DN_HEADS):
            at_ref[h] = attns[h]
            iv_ref[h] = invs[h]

    blk = pl.BlockSpec((R, MIX_W), lambda n: (n, 0))
    hblk = pl.BlockSpec((DN_HEADS, R, DN_CHUNK), lambda n: (0, n, 0))
    wide = jax.ShapeDtypeStruct((S, MIX_W), F32)
    narrow = jax.ShapeDtypeStruct((DN_HEADS, S, DN_CHUNK), F32)
    return pl.pallas_call(
        body, grid=(S // R,), name=name, in_specs=[blk] * 5, out_specs=[blk] * 5 + [hblk] * 2,
        out_shape=[wide] * 5 + [narrow] * 2, compiler_params=_cparams("parallel"),
    )(qc, kc, vc, ge, be)


def delta_prep_bwd(qc, kc, vc, ge, be, inv, cts, name):
    S = qc.shape[0]
    R = min(S, DN_PREP_ROWS)

    def body(q_ref, k_ref, v_ref, ge_ref, be_ref, iv_ref, du_ref, dw_ref, dqd_ref, dkd_ref, dsc_ref, dat_ref, *outs):
        invs = [iv_ref[h] for h in range(DN_HEADS)]
        fn = lambda *blocks: _prep_rows(_delta_prep, list(blocks), invs)[:6]
        _, vjp = jax.vjp(fn, q_ref[...], k_ref[...], v_ref[...], ge_ref[...], be_ref[...])
        grads = vjp((du_ref[...], dw_ref[...], dqd_ref[...], dkd_ref[...], dsc_ref[...],
                     [dat_ref[h] for h in range(DN_HEADS)]))
        for ref, g in zip(outs, grads):
            ref[...] = g

    blk = pl.BlockSpec((R, MIX_W), lambda n: (n, 0))
    hblk = pl.BlockSpec((DN_HEADS, R, DN_CHUNK), lambda n: (0, n, 0))
    return pl.pallas_call(
        body, grid=(S // R,), name=name, in_specs=[blk] * 5 + [hblk] + [blk] * 5 + [hblk], out_specs=[blk] * 5,
        out_shape=[jax.ShapeDtypeStruct((S, MIX_W), F32)] * 5, compiler_params=_cparams("parallel"),
    )(qc, kc, vc, ge, be, inv, *cts)


def delta_step_fwd(prep, z, ng, name):
    S = prep[0].shape[0]
    N = S // DN_CHUNK
    C = DN_CHUNK

    def body(u_ref, w_ref, qd_ref, kd_ref, sc_ref, at_ref, gate_ref, ng_ref, o_ref, st_ref, s_ref):
        @pl.when(pl.program_id(0) == 0)
        def _():
            s_ref[...] = jnp.zeros(s_ref.shape, F32)

        states = [s_ref[h] for h in range(DN_HEADS)]
        for h in range(DN_HEADS):
            st_ref[0, h] = states[h]
        new_states, o = _delta_step(states, u_ref[...], w_ref[...], qd_ref[...], kd_ref[...], sc_ref[...],
                                    [at_ref[h] for h in range(DN_HEADS)], gate_ref[...], ng_ref[...])
        for h in range(DN_HEADS):
            s_ref[h] = new_states[h]
        o_ref[...] = o.astype(o_ref.dtype)

    blk = pl.BlockSpec((C, MIX_W), lambda n: (n, 0))
    return pl.pallas_call(
        body, grid=(N,), name=name,
        in_specs=[blk] * 5 + [pl.BlockSpec((DN_HEADS, C, C), lambda n: (0, n, 0)),
                              pl.BlockSpec((C, MIX_W), lambda n: (n, ZB_DGATE)),
                              pl.BlockSpec((1, DN_HEAD_DIM), lambda n: (0, 0))],
        out_specs=[blk, pl.BlockSpec((1, DN_HEADS, DN_HEAD_DIM, DN_HEAD_DIM), lambda n: (n, 0, 0, 0))],
        out_shape=[jax.ShapeDtypeStruct((S, MIX_W), BF16),
                   jax.ShapeDtypeStruct((N, DN_HEADS, DN_HEAD_DIM, DN_HEAD_DIM), F32)],
        scratch_shapes=[pltpu.VMEM((DN_HEADS, DN_HEAD_DIM, DN_HEAD_DIM), F32)],
        compiler_params=_cparams("arbitrary"),
    )(*prep, z, ng)


def delta_step_bwd(prep, z, ng, states, do, name):
    S = prep[0].shape[0]
    N = S // DN_CHUNK
    C = DN_CHUNK

    def body(u_ref, w_ref, qd_ref, kd_ref, sc_ref, at_ref, gate_ref, ng_ref, st_ref, do_ref,
             du_ref, dw_ref, dqd_ref, dkd_ref, dsc_ref, dat_ref, dgate_ref, dng_ref, ds_ref):
        first = pl.program_id(0) == 0

        @pl.when(first)
        def _():
            ds_ref[...] = jnp.zeros(ds_ref.shape, F32)

        args = ([st_ref[0, h] for h in range(DN_HEADS)], u_ref[...], w_ref[...], qd_ref[...], kd_ref[...],
                sc_ref[...], [at_ref[h] for h in range(DN_HEADS)], gate_ref[...].astype(F32), ng_ref[...])
        _, vjp = jax.vjp(_delta_step, *args)
        d_states, du, dw, dqd, dkd, dsc, dat, dgate, dng = vjp(([ds_ref[h] for h in range(DN_HEADS)],
                                                               do_ref[...].astype(F32)))
        for h in range(DN_HEADS):
            ds_ref[h] = d_states[h]
            dat_ref[h] = dat[h]
        for ref, g in zip((du_ref, dw_ref, dqd_ref, dkd_ref, dsc_ref), (du, dw, dqd, dkd, dsc)):
            ref[...] = g
        dgate_ref[...] = dgate.astype(dgate_ref.dtype)

        @pl.when(first)
        def _():
            dng_ref[...] = dng

        @pl.when(jnp.logical_not(first))
        def _():
            dng_ref[...] += dng

    blk = pl.BlockSpec((C, MIX_W), lambda n: (N - 1 - n, 0))
    hblk = pl.BlockSpec((DN_HEADS, C, C), lambda n: (0, N - 1 - n, 0))
    ngs = pl.BlockSpec((1, DN_HEAD_DIM), lambda n: (0, 0))
    f32o = jax.ShapeDtypeStruct((S, MIX_W), F32)
    return pl.pallas_call(
        body, grid=(N,), name=name,
        in_specs=[blk] * 5 + [hblk, pl.BlockSpec((C, MIX_W), lambda n: (N - 1 - n, ZB_DGATE)), ngs,
                              pl.BlockSpec((1, DN_HEADS, DN_HEAD_DIM, DN_HEAD_DIM), lambda n: (N - 1 - n, 0, 0, 0)),
                              blk],
        out_specs=[blk] * 5 + [hblk, blk, ngs],
        out_shape=[f32o] * 5 + [jax.ShapeDtypeStruct((DN_HEADS, S, C), F32), jax.ShapeDtypeStruct((S, MIX_W), BF16),
                                jax.ShapeDtypeStruct((1, DN_HEAD_DIM), F32)],
        scratch_shapes=[pltpu.VMEM((DN_HEADS, DN_HEAD_DIM, DN_HEAD_DIM), F32)],
        compiler_params=_cparams("arbitrary"),
    )(*prep, z, ng, states, do)


ANY = pl.BlockSpec(memory_space=pl.ANY)


def _place():
    return lax.axis_index("x"), lax.axis_index("y"), lax.axis_index("c")


def all_gather(shards, name):
    n = len(shards)

    def body(*refs):
        ins, outs = refs[:n], refs[n:2 * n]
        send_sems, recv_sems = refs[2 * n:]
        x, y, c = _place()
        me, sibling = (x, y, c), (x, y, 1 - c)
        chips = [(1 - x, y), (x, 1 - y), (1 - x, 1 - y)]

        def copy(a, k, block, to, src=None):
            row = 4 * block[0] + 2 * block[1] + block[2]
            return pltpu.make_async_remote_copy(
                src_ref=outs[a].at[row] if src is None else src, dst_ref=outs[a].at[row],
                send_sem=send_sems.at[a, k], recv_sem=recv_sems.at[a, k], device_id=to, device_id_type=MESH_ID)

        first = []
        for a in range(n):
            first.append(copy(a, 0, me, sibling, src=ins[a]))
            first += [copy(a, 1 + j, me, (*chip, c), src=ins[a]) for j, chip in enumerate(chips)]
        for cp in first:
            cp.start()
        passed = []
        for j, chip in enumerate(chips):
            for a in range(n):
                copy(a, 1 + j, (*chip, c), me).wait_recv()
                passed.append(copy(a, 4 + j, (*chip, c), sibling))
                passed[-1].start()
        for a in range(n):
            copy(a, 0, sibling, me).wait_recv()
            for j, chip in enumerate(chips):
                copy(a, 4 + j, (*chip, 1 - c), me).wait_recv()
        for cp in first + passed:
            cp.wait_send()

    return pl.pallas_call(
        body, name=name, in_specs=[ANY] * n, out_specs=[ANY] * n,
        out_shape=[jax.ShapeDtypeStruct((8,) + s.shape, s.dtype) for s in shards],
        scratch_shapes=[pltpu.SemaphoreType.DMA((n, 7)), pltpu.SemaphoreType.DMA((n, 7))],
    )(*shards)


def exchange_cores(grads, name):
    n = len(grads)

    def body(*refs):
        ins, outs = refs[:n], refs[n:2 * n]
        send_sems, recv_sems = refs[2 * n:]
        x, y, c = _place()
        copies = [pltpu.make_async_remote_copy(
            src_ref=ins[a].at[2 * k + 1 - c], dst_ref=outs[a].at[k], send_sem=send_sems.at[a, k],
            recv_sem=recv_sems.at[a, k], device_id=(x, y, 1 - c), device_id_type=MESH_ID)
            for a in range(n) for k in range(4)]
        for cp in copies:
            cp.start()
        for cp in copies:
            cp.wait()

    return pl.pallas_call(
        body, name=name, in_specs=[ANY] * n, out_specs=[ANY] * n,
        out_shape=[jax.ShapeDtypeStruct((4,) + g.shape[1:], g.dtype) for g in grads],
        scratch_shapes=[pltpu.SemaphoreType.DMA((n, 4))] * 2,
    )(*grads)


def exchange_chips(parts, name):
    n = len(parts)

    def body(*refs):
        ins, outs = refs[:n], refs[n:2 * n]
        send_sems, recv_sems = refs[2 * n:]
        x, y, c = _place()
        chips = [(1 - x, y), (x, 1 - y), (1 - x, 1 - y)]
        copies = [pltpu.make_async_remote_copy(
            src_ref=ins[a].at[2 * px + py], dst_ref=outs[a].at[j], send_sem=send_sems.at[a, j],
            recv_sem=recv_sems.at[a, j], device_id=(px, py, c), device_id_type=MESH_ID)
            for a in range(n) for j, (px, py) in enumerate(chips)]
        for cp in copies:
            cp.start()
        for cp in copies:
            cp.wait()

    return pl.pallas_call(
        body, name=name, in_specs=[ANY] * n, out_specs=[ANY] * n,
        out_shape=[jax.ShapeDtypeStruct((3,) + p.shape[1:], p.dtype) for p in parts],
        scratch_shapes=[pltpu.SemaphoreType.DMA((n, 3))] * 2,
    )(*parts)


def _as2d(a, lead=0):
    return a.reshape(a.shape[:lead] + (-1, a.shape[-1]))


def _row_tile(rows, cols, n_arrays):
    budget = VMEM_LIMIT_V7X // 2
    lanes = -(-cols // 128) * 128
    t = budget // (2 * n_arrays * lanes * 4)
    if t >= rows:
        return rows
    return max(8, t // 8 * 8)


def _index_operand(i):
    return jnp.asarray(i, jnp.int32).reshape(1)


def add_own_rows(grads, core, recv, name):
    _, R, C = grads.shape
    T = _row_tile(R, C, 3)

    def body(c_ref, g_ref, r_ref, o_ref):
        o_ref[...] = g_ref[...] + r_ref[...]

    blk = pl.BlockSpec((1, T, C), lambda k, i, c: (k, i, 0))
    return pl.pallas_call(
        body, name=name, out_shape=jax.ShapeDtypeStruct((4, R, C), grads.dtype),
        grid_spec=pltpu.PrefetchScalarGridSpec(
            num_scalar_prefetch=1, grid=(4, pl.cdiv(R, T)),
            in_specs=[pl.BlockSpec((1, T, C), lambda k, i, c: (2 * k + c[0], i, 0)), blk], out_specs=blk),
        compiler_params=_cparams("parallel", "parallel"),
    )(_index_operand(core), grads, recv)


def _adamw(w, g, m, v):
    m = ADAM_B1 * m + (1.0 - ADAM_B1) * g
    v = ADAM_B2 * v + (1.0 - ADAM_B2) * jnp.square(g)
    m_hat = m / (1.0 - ADAM_B1 ** ADAM_STEP)
    v_hat = v / (1.0 - ADAM_B2 ** ADAM_STEP)
    delta = -ADAM_LR * (m_hat / (jnp.sqrt(v_hat) + ADAM_EPS) + ADAM_WD * w)
    return delta, m, v


def adamw_sum(parts, w, m, v, name, own=None, own_row=None):
    P, R, C = parts.shape
    T = _row_tile(R, C, P + 8)
    has_own = own is not None

    def body(i_ref, *refs):
        refs = list(refs)
        own_ref = refs.pop(0) if has_own else None
        p_ref, w_ref, m_ref, v_ref, g_ref, d_ref, nm_ref, nv_ref = refs
        terms = ([own_ref[0]] if has_own else []) + [p_ref[k] for k in range(P)]
        g = terms[0]
        for t in terms[1:]:
            g = g + t
        d, nm, nv = _adamw(w_ref[...], g, m_ref[...], v_ref[...])
        g_ref[...] = g
        d_ref[...] = d
        nm_ref[...] = nm
        nv_ref[...] = nv

    blk = pl.BlockSpec((T, C), lambda i, r: (i, 0))
    in_specs = [pl.BlockSpec((P, T, C), lambda i, r: (0, i, 0)), blk, blk, blk]
    operands = [parts, w, m, v]
    if has_own:
        in_specs.insert(0, pl.BlockSpec((1, T, C), lambda i, r: (r[0], i, 0)))
        operands.insert(0, own)
    return pl.pallas_call(
        body, name=name, out_shape=[jax.ShapeDtypeStruct((R, C), F32)] * 4,
        grid_spec=pltpu.PrefetchScalarGridSpec(num_scalar_prefetch=1, grid=(pl.cdiv(R, T),), in_specs=in_specs,
                                               out_specs=[blk] * 4),
        compiler_params=_cparams("parallel"),
    )(_index_operand(0 if own_row is None else own_row), *operands)


def f_norm_res(x, g):
    return _rms(x, g), x


def _pool_taps():
    taps = np.zeros((16, MIX_W), np.float32)
    for gi, win in enumerate(POOL_WINDOWS):
        taps[16 - win:, gi * 128:(gi + 1) * 128] = 1.0
    return jnp.asarray(taps)


def loss_head(x, tgt, nf, T, name):
    S, D = x.shape

    def body(x_ref, t_ref, g_ref, l_ref, dx_ref, dg_ref):
        val, vjp = jax.vjp(f_loss, x_ref[...], t_ref[...], g_ref[...])
        dx, _, dg = vjp((jnp.ones((1, 1), F32),))
        dx_ref[...] = dx
        first = pl.program_id(0) == 0
        lv = jnp.broadcast_to(val[0], (1, 128))

        @pl.when(first)
        def _():
            l_ref[...] = lv
            dg_ref[...] = dg

        @pl.when(jnp.logical_not(first))
        def _():
            l_ref[...] += lv
            dg_ref[...] += dg

    row = pl.BlockSpec((T, D), lambda i: (i, 0))
    return pl.pallas_call(
        body, grid=(S // T,), name=name,
        in_specs=[row, row, pl.BlockSpec((1, D), lambda i: (0, 0))],
        out_specs=[pl.BlockSpec((1, 128), lambda i: (0, 0)), row, pl.BlockSpec((1, D), lambda i: (0, 0))],
        out_shape=[jax.ShapeDtypeStruct((1, 128), F32), jax.ShapeDtypeStruct((S, D), F32),
                   jax.ShapeDtypeStruct((1, D), F32)],
        compiler_params=_cparams("arbitrary"),
    )(x, tgt, nf)


def layer_fwd(x, mem, W, tag):
    S, D = x.shape
    T = min(S, 256)
    sv = {'x': x}
    (h1,) = rows_fwd(f_norm, [(x, D, 0)], [W['norm_mix']], 'p', [(D, BF16)], T, tag + 'norm_mix')
    z = matmul(h1, W['w_in'], 'nn', name=tag + 'w_in')
    (a_pre,) = rows_fwd(f_glu, [(z, 512, ZB_A1), (z, 512, ZB_A2)], [], '', [(512, F32)], T, tag + 'glu')
    a_cv = conv_fwd(a_pre, 0, W['conv_a_w'], tag + 'conv_a')
    (a,) = rows_fwd(f_lnsilu, [(a_cv, 512, 0)], [W['conv_a_b'], W['ln_a_g'], W['ln_a_b']], 'ppp', [(512, BF16)], T,
                    tag + 'ln_a')
    qc = conv_fwd(z, ZB_Q, W['dn_wq'], tag + 'conv_q')
    kc = conv_fwd(z, ZB_K, W['dn_wk'], tag + 'conv_k')
    vc = conv_fwd(z, ZB_V, W['dn_wv'], tag + 'conv_v')
    ge, be = rows_fwd(f_gbeta, [(z, 128, ZB128_BD)], [W['alp'], W['dtp']], 'pp', [(512, F32), (512, F32)], T,
                      tag + 'gbeta')
    *prep, dn_inv = delta_prep_fwd(qc, kc, vc, ge, be, tag + 'delta_prep')
    o, states = delta_step_fwd(prep, z, W['dn_norm_g'], tag + 'delta')
    gm_params = [W['gm_ln_g'], W['gm_ln_b'], W['gm_ws']] + W['gm_b']
    (c,) = rows_fwd(f_gmlp, [(z, 512, ZB_GU), (z, 512, ZB_GV)], gm_params, 'p' * 7, [(512, BF16)], T, tag + 'gmlp')
    cs = conv_fwd(z, ZB_POOL, _pool_taps(), tag + 'pool_sum')
    (p,) = rows_fwd(f_pool, [(cs, 512, 0), (z, 512, ZB_POOL)], [W['pool_w'], W['pool_scale']], 'pp', [(512, BF16)], T,
                    tag + 'pool')
    (merged,) = rows_fwd(f_merge, [(a, 512, 0), (o, 512, 0), (c, 512, 0), (p, 512, 0), (z, 4 * D, ZB_GATE)],
                         [W['w_branch']], 'w', [(D, BF16)], min(S, 128), tag + 'merge')
    x1 = matmul(merged, W['w_out'], 'nn', res=x, name=tag + 'w_out')
    (kv,) = rows_fwd(f_kv, [(mem, D, 0)], [W['norm_mem'], W['xa_wkv']], 'pw', [(2 * D, F32)], mem.shape[0],
                     tag + 'kv')
    (x2,) = rows_fwd(f_xattn, [(x1, D, 0)], [kv, W['norm_xa'], W['xa_wq'], W['xa_wo']], 'ppww', [(D, F32)], T,
                     tag + 'xattn')
    (h3,) = rows_fwd(f_norm, [(x2, D, 0)], [W['norm_mlp']], 'p', [(D, BF16)], T, tag + 'norm_mlp')
    pre, r = matmul(h3, W['mlp_w1'], 'nn', act='relu2', name=tag + 'mlp_w1')
    x3 = matmul(r, W['mlp_w2'], 'nn', res=x2, name=tag + 'mlp_w2')
    sv.update(h1=h1, z=z, a_pre=a_pre, a_cv=a_cv, a=a, qc=qc, kc=kc, vc=vc, ge=ge, be=be, prep=prep, dn_inv=dn_inv, o=o,
              states=states, c=c,
              cs=cs, p=p, merged=merged, x1=x1, kv=kv, x2=x2, h3=h3, pre=pre, r=r)
    return x3, sv


def layer_bwd(dx, mem, W, sv, tag):
    S, D = dx.shape
    T = min(S, 256)
    tag = tag + 'b_'
    G = {}
    z = sv['z']
    da = matmul(dx, W['mlp_w2'], 'nt', gate=sv['pre'], out_dtype=BF16, name=tag + 'mlp_da')
    G['mlp_w2'] = matmul(sv['r'], dx, 'tn', name=tag + 'mlp_gw2')
    dh3 = matmul(da, W['mlp_w1'], 'nt', name=tag + 'mlp_dh')
    G['mlp_w1'] = matmul(sv['h3'], da, 'tn', name=tag + 'mlp_gw1')
    (dx2,), (G['norm_mlp'],) = rows_bwd(f_norm_res, [(sv['x2'], D, 0)], [W['norm_mlp']], 'p', [dh3, dx], [F32], T,
                                        tag + 'norm_mlp')
    (dx1,), (dkv, G['norm_xa'], G['xa_wq'], G['xa_wo']) = rows_bwd(
        f_xattn, [(sv['x1'], D, 0)], [sv['kv'], W['norm_xa'], W['xa_wq'], W['xa_wo']], 'ppww', [dx2], [F32],
        min(S, 128), tag + 'xattn')
    _, (G['norm_mem'], G['xa_wkv']) = rows_bwd(f_kv, [(mem, D, 0)], [W['norm_mem'], W['xa_wkv']], 'pw', [dkv], [None],
                                               mem.shape[0], tag + 'kv')
    dmerged = matmul(dx1, W['w_out'], 'nt', out_dtype=BF16, name=tag + 'dmerged')
    G['w_out'] = matmul(sv['merged'], dx1, 'tn', name=tag + 'gw_out')
    (d_a, d_o, d_c, d_p, dz_gate), (G['w_branch'],) = rows_bwd(
        f_merge, [(sv['a'], 512, 0), (sv['o'], 512, 0), (sv['c'], 512, 0), (sv['p'], 512, 0), (z, 4 * D, ZB_GATE)],
        [W['w_branch']], 'w', [dmerged], [F32, F32, F32, F32, BF16], min(S, 128), tag + 'merge')
    (dcs, dpx), (G['pool_w'], G['pool_scale']) = rows_bwd(
        f_pool, [(sv['cs'], 512, 0), (z, 512, ZB_POOL)], [W['pool_w'], W['pool_scale']], 'pp', [d_p], [F32, F32], T,
        tag + 'pool')
    dz_pool, _ = conv_bwd(z, ZB_POOL, _pool_taps(), dcs, tag + 'pool_sum', add=dpx, out_dtype=BF16)
    gm_params = [W['gm_ln_g'], W['gm_ln_b'], W['gm_ws']] + W['gm_b']
    (dz_gu, dz_gv), gm_g = rows_bwd(f_gmlp, [(z, 512, ZB_GU), (z, 512, ZB_GV)], gm_params, 'p' * 7, [d_c],
                                    [BF16, BF16], T, tag + 'gmlp')
    G['gm_ln_g'], G['gm_ln_b'], G['gm_ws'] = gm_g[:3]
    G['gm_b'] = list(gm_g[3:])
    *d_prep, dz_dgate, G['dn_norm_g'] = delta_step_bwd(sv['prep'], z, W['dn_norm_g'], sv['states'], d_o, tag + 'delta')
    dqc, dkc, dvc, dge, dbe = delta_prep_bwd(sv['qc'], sv['kc'], sv['vc'], sv['ge'], sv['be'], sv['dn_inv'], d_prep,
                                             tag + 'delta_prep')
    (dz_bd,), (G['alp'], G['dtp']) = rows_bwd(f_gbeta, [(z, 128, ZB128_BD)], [W['alp'], W['dtp']], 'pp', [dge, dbe],
                                              [BF16], T, tag + 'gbeta')
    dz_q, G['dn_wq'] = conv_bwd(z, ZB_Q, W['dn_wq'], dqc, tag + 'conv_q', out_dtype=BF16)
    dz_k, G['dn_wk'] = conv_bwd(z, ZB_K, W['dn_wk'], dkc, tag + 'conv_k', out_dtype=BF16)
    dz_v, G['dn_wv'] = conv_bwd(z, ZB_V, W['dn_wv'], dvc, tag + 'conv_v', out_dtype=BF16)
    (da_cv,), (G['conv_a_b'], G['ln_a_g'], G['ln_a_b']) = rows_bwd(
        f_lnsilu, [(sv['a_cv'], 512, 0)], [W['conv_a_b'], W['ln_a_g'], W['ln_a_b']], 'ppp', [d_a], [F32], T,
        tag + 'ln_a')
    da_pre, G['conv_a_w'] = conv_bwd(sv['a_pre'], 0, W['conv_a_w'], da_cv, tag + 'conv_a')
    (dz_a1, dz_a2), _ = rows_bwd(f_glu, [(z, 512, ZB_A1), (z, 512, ZB_A2)], [], '', [da_pre], [BF16, BF16], T,
                                 tag + 'glu')
    dz = jnp.concatenate([dz_gate, dz_a1, dz_a2, dz_q, dz_k, dz_v, dz_dgate, dz_gu, dz_gv, dz_pool, dz_bd], axis=1)
    dh1 = matmul(dz, W['w_in'], 'nt', name=tag + 'dh1')
    G['w_in'] = matmul(sv['h1'], dz, 'tn', name=tag + 'gw_in')
    (dx0,), (G['norm_mix'],) = rows_bwd(f_norm_res, [(sv['x'], D, 0)], [W['norm_mix']], 'p', [dh1, dx1], [F32], T,
                                        tag + 'norm_mix')
    return dx0, G


def local_step(x, mem, tgt, layers, norm_f):
    saved = []
    for l, W in enumerate(layers):
        x, sv = layer_fwd(x, mem, W, f'l{l}_')
        saved.append(sv)
    loss, dx, g_nf = loss_head(x, tgt, norm_f, min(x.shape[0], 256), 'loss_head')
    grads = [None] * len(layers)
    for l in reversed(range(len(layers))):
        dx, grads[l] = layer_bwd(dx, mem, layers[l], saved[l], f'l{l}_')
    return loss, dx, grads, g_nf


def _row(v):
    return v.reshape(1, -1)


def _lane_pad(v):
    return jnp.zeros((8, 128), F32).at[0, :v.shape[0]].set(v)


def layer_weights(full, l):
    w_in = full['w_in'][l]
    cols = [w_in[:, a:b] for a, b in Z_ORDER]
    cols.append(jnp.zeros((w_in.shape[0], Z_W - sum(b - a for a, b in Z_ORDER)), w_in.dtype))
    dn_w = full['dn_conv_w'][l]
    W = {n: _row(full[n][l]) for n in ('norm_mix', 'conv_a_b', 'ln_a_g', 'ln_a_b', 'dn_norm_g', 'gm_ln_g', 'gm_ln_b',
                                       'pool_scale', 'norm_xa', 'norm_mem', 'norm_mlp')}
    W.update(w_in=jnp.concatenate(cols, axis=1), conv_a_w=full['conv_a_w'][l],
             dn_wq=dn_w[:, :MIX_W], dn_wk=dn_w[:, MIX_W:2 * MIX_W], dn_wv=dn_w[:, 2 * MIX_W:],
             alp=_lane_pad(full['dn_a_log'][l]), dtp=_lane_pad(full['dn_dt_bias'][l]),
             gm_ws=full['gm_ws'][l], gm_b=[full['gm_bs'][l][g].reshape(GM_CHUNK, 1) for g in range(GM_GROUPS)],
             pool_w=full['pool_w'][l])
    for n in ('w_branch', 'w_out', 'xa_wq', 'xa_wkv', 'xa_wo', 'mlp_w1', 'mlp_w2'):
        W[n] = full[n][l]
    return W


def layer_grads(G):
    g_in = G['w_in']
    starts = np.cumsum([0] + [b - a for a, b in Z_ORDER])
    pieces = sorted(zip(Z_ORDER, starts[:-1]))
    out = {n: G[n].reshape(-1) for n in ('norm_mix', 'conv_a_b', 'ln_a_g', 'ln_a_b', 'dn_norm_g', 'gm_ln_g', 'gm_ln_b',
                                          'pool_scale', 'norm_xa', 'norm_mem', 'norm_mlp')}
    out.update(w_in=jnp.concatenate([g_in[:, s:s + b - a] for (a, b), s in pieces], axis=1),
               conv_a_w=G['conv_a_w'], dn_conv_w=jnp.concatenate([G['dn_wq'], G['dn_wk'], G['dn_wv']], axis=1),
               dn_a_log=G['alp'][0, :DN_HEADS], dn_dt_bias=G['dtp'][0, :DN_HEADS], gm_ws=G['gm_ws'],
               gm_bs=jnp.stack([b.reshape(-1) for b in G['gm_b']]), pool_w=G['pool_w'])
    for n in ('w_branch', 'w_out', 'xa_wq', 'xa_wkv', 'xa_wo', 'mlp_w1', 'mlp_w2'):
        out[n] = G[n]
    return out


COLUMN_SHARDED = ('w_in', 'conv_a_w', 'dn_conv_w', 'w_branch', 'xa_wkv', 'mlp_w1')


def _unshard(name, g):
    if name in COLUMN_SHARDED:
        t = jnp.moveaxis(g, 0, -2)
        return t.reshape(t.shape[:-2] + (t.shape[-2] * t.shape[-1],))
    t = jnp.moveaxis(g, 0, 1)
    return t.reshape((t.shape[0], t.shape[1] * t.shape[2]) + t.shape[3:])


def _shard_rows(name, g):
    if name in COLUMN_SHARDED:
        t = g.reshape(g.shape[:-1] + (8, g.shape[-1] // 8))
        return jnp.moveaxis(t, -2, 0)
    t = g.reshape((g.shape[0], 8, g.shape[1] // 8) + g.shape[2:])
    return jnp.moveaxis(t, 1, 0)


def _pack(arrays):
    flat = jnp.concatenate([a.reshape(-1) for a in arrays])
    rows = -(-flat.shape[0] // 1024) * 8
    return jnp.pad(flat, (0, rows * 128 - flat.shape[0])).reshape(rows, 128)


def _unpack(packed, like):
    flat = packed.reshape(-1)
    out, at = [], 0
    for a in like:
        out.append(flat[at:at + a.size].reshape(a.shape))
        at += a.size
    return out


def kernel(x, mem, norm_mix, w_in, conv_a_w, conv_a_b, ln_a_g, ln_a_b, dn_conv_w, dn_a_log, dn_dt_bias, dn_norm_g, gm_ln_g, gm_ln_b, gm_ws, gm_bs, pool_w, pool_scale, w_branch, w_out, norm_xa, norm_mem, xa_wq, xa_wkv, xa_wo, norm_mlp, mlp_w1, mlp_w2, norm_f, loss_target, m_norm_mix, m_w_in, m_conv_a_w, m_conv_a_b, m_ln_a_g, m_ln_a_b, m_dn_conv_w, m_dn_a_log, m_dn_dt_bias, m_dn_norm_g, m_gm_ln_g, m_gm_ln_b, m_gm_ws, m_gm_bs, m_pool_w, m_pool_scale, m_w_branch, m_w_out, m_norm_xa, m_norm_mem, m_xa_wq, m_xa_wkv, m_xa_wo, m_norm_mlp, m_mlp_w1, m_mlp_w2, m_norm_f, v_norm_mix, v_w_in, v_conv_a_w, v_conv_a_b, v_ln_a_g, v_ln_a_b, v_dn_conv_w, v_dn_a_log, v_dn_dt_bias, v_dn_norm_g, v_gm_ln_g, v_gm_ln_b, v_gm_ws, v_gm_bs, v_pool_w, v_pool_scale, v_w_branch, v_w_out, v_norm_xa, v_norm_mem, v_xa_wq, v_xa_wkv, v_xa_wo, v_norm_mlp, v_mlp_w1, v_mlp_w2, v_norm_f):
    args = locals()
    w = {n: args[n] for n in WEIGHTS}
    m = {n: args['m_' + n] for n in WEIGHTS}
    v = {n: args['v_' + n] for n in WEIGHTS}

    px, py, pc = _place()
    gathered = all_gather([w[n].astype(BF16) if n in SENT_AS_BF16 else w[n] for n in SHARDED], 'gather_weights')
    full = {n: w[n] for n in REPLICATED}
    for n, g in zip(SHARDED, gathered):
        g = lax.dynamic_update_index_in_dim(g, w[n].astype(g.dtype), 4 * px + 2 * py + pc, 0)
        full[n] = _unshard(n, g)
    layers = [layer_weights(full, l) for l in range(DEPTH)]
    loss, grad_x, grads, g_nf = local_step(x[0], mem[0], loss_target[0], layers, _row(norm_f))
    per_layer = [layer_grads(g) for g in grads]
    gfull = {n: jnp.stack([g[n] for g in per_layer]) for n in WEIGHTS if n != 'norm_f'}
    gfull['norm_f'] = g_nf.reshape(-1)

    out = {}
    rows = [_as2d(_shard_rows(n, gfull[n]), 1) for n in SHARDED]
    from_sibling = exchange_cores(rows, 'reduce_cores')
    sums = [add_own_rows(g, pc, r, 'reduce_add_' + n) for n, g, r in zip(SHARDED, rows, from_sibling)]
    from_chips = exchange_chips(sums, 'reduce_chips')
    for n, own, p in zip(SHARDED, sums, from_chips):
        res = adamw_sum(p, _as2d(w[n]), _as2d(m[n]), _as2d(v[n]), 'adamw_' + n, own=own, own_row=2 * px + py)
        out[n] = [r.reshape(w[n].shape) for r in res]
    packed = _pack([gfull[n] for n in REPLICATED])
    (partials,) = all_gather([packed], 'gather_small_grads')
    partials = lax.dynamic_update_index_in_dim(partials, packed, 4 * px + 2 * py + pc, 0)
    res = adamw_sum(partials, _pack([w[n] for n in REPLICATED]), _pack([m[n] for n in REPLICATED]),
                    _pack([v[n] for n in REPLICATED]), 'adamw_small')
    like = [w[n] for n in REPLICATED]
    for k, r in enumerate(res):
        for n, a in zip(REPLICATED, _unpack(r, like)):
            out.setdefault(n, [None] * 4)[k] = a

    total = lax.psum(loss[0, 0], ('x', 'y', 'c'))
    return (total, grad_x[None], *[out[n][0] for n in WEIGHTS], *[out[n][1] for n in WEIGHTS],
            *[out[n][2] for n in WEIGHTS], *[out[n][3] for n in WEIGHTS])
```

```python
import functools
import math

import numpy as np
import jax
import jax.numpy as jnp
from jax import lax
from jax.experimental import pallas as pl
from jax.experimental.pallas import tpu as pltpu

F32 = jnp.float32
BF16 = jnp.bfloat16
HIGH = lax.Precision.HIGH
MESH_ID = pl.DeviceIdType.MESH
VMEM_LIMIT_V7X = 56 << 20

DEPTH = 4
MIX_W = 512
DN_HEADS = 4
DN_HEAD_DIM = 128
DN_CHUNK = 64
GM_CHUNK = 128
GM_GROUPS = 4
POOL_WINDOWS = (2, 4, 8, 16)
XA_HEADS = 4
CONV_PAD = 32

ADAM_LR = 0.001
ADAM_B1 = 0.9
ADAM_B2 = 0.999
ADAM_EPS = 1e-08
ADAM_WD = 0.01
ADAM_STEP = 10

WEIGHTS = ['norm_mix', 'w_in', 'conv_a_w', 'conv_a_b', 'ln_a_g', 'ln_a_b', 'dn_conv_w', 'dn_a_log', 'dn_dt_bias',
           'dn_norm_g', 'gm_ln_g', 'gm_ln_b', 'gm_ws', 'gm_bs', 'pool_w', 'pool_scale', 'w_branch', 'w_out',
           'norm_xa', 'norm_mem', 'xa_wq', 'xa_wkv', 'xa_wo', 'norm_mlp', 'mlp_w1', 'mlp_w2', 'norm_f']
SHARDED = ['w_in', 'conv_a_w', 'dn_conv_w', 'w_branch', 'w_out', 'xa_wq', 'xa_wkv', 'xa_wo', 'mlp_w1', 'mlp_w2']
SENT_AS_BF16 = ['w_in', 'w_branch', 'w_out', 'xa_wq', 'xa_wkv', 'xa_wo', 'mlp_w1', 'mlp_w2']
REPLICATED = [n for n in WEIGHTS if n not in SHARDED]

Z_W = 8832
Z_ORDER = ((4616, 8712), (0, 3072), (3080, 4616), (3072, 3080))
ZB_GATE, ZB_A1, ZB_A2, ZB_Q, ZB_K, ZB_V, ZB_DGATE, ZB_GU, ZB_GV, ZB_POOL = 0, 8, 9, 10, 11, 12, 13, 14, 15, 16
ZB128_BD = 68


def _cparams(*sem):
    return pltpu.CompilerParams(dimension_semantics=sem, vmem_limit_bytes=VMEM_LIMIT_V7X)


def _dot(a, b, ca, cb):
    return lax.dot_general(a.astype(BF16), b.astype(BF16), (((ca,), (cb,)), ((), ())), preferred_element_type=F32)


def _doth(a, b, ca, cb):
    return lax.dot_general(a, b, (((ca,), (cb,)), ((), ())), precision=HIGH, preferred_element_type=F32)


def _make_mm(dot):
    @jax.custom_vjp
    def nn(a, b):
        return dot(a, b, 1, 0)
    nn.defvjp(lambda a, b: (dot(a, b, 1, 0), (a, b)), lambda r, g: (dot(g, r[1], 1, 1), dot(r[0], g, 0, 0)))

    @jax.custom_vjp
    def nt(a, b):
        return dot(a, b, 1, 1)
    nt.defvjp(lambda a, b: (dot(a, b, 1, 1), (a, b)), lambda r, g: (dot(g, r[1], 1, 0), dot(g, r[0], 0, 0)))

    @jax.custom_vjp
    def tn(a, b):
        return dot(a, b, 0, 0)
    tn.defvjp(lambda a, b: (dot(a, b, 0, 0), (a, b)), lambda r, g: (dot(r[1], g, 1, 1), dot(r[0], g, 1, 0)))
    return nn, nt, tn


mm, mm_nt, mm_tn = _make_mm(_dot)
mmh, mmh_nt, mmh_tn = _make_mm(_doth)


@jax.custom_vjp
def _mmw(a, w, wz):
    return _dot(a, w, 1, 0)


_mmw.defvjp(lambda a, w, wz: (_dot(a, w, 1, 0), (a, w)),
            lambda r, g: (_dot(g, r[1], 1, 1), jnp.zeros_like(r[1]), _dot(r[0], g, 0, 0)))


def mmw(a, wpair):
    w, wz = wpair
    return _dot(a, w, 1, 0) if wz is None else _mmw(a, w, wz)


def wsel(wpair, n):
    return (wpair[0][n], None if wpair[1] is None else wpair[1][n])


def _sigmoid(x):
    return 1.0 / (1.0 + jnp.exp(-x))


def _silu(x):
    return x * _sigmoid(x)


def _rms(x, g, eps=1e-6):
    return x * lax.rsqrt(jnp.mean(x * x, axis=-1, keepdims=True) + eps) * g


def _ln(x, g, b, eps=1e-5):
    mu = jnp.mean(x, axis=-1, keepdims=True)
    d = x - mu
    return d * lax.rsqrt(jnp.mean(d * d, axis=-1, keepdims=True) + eps) * g + b


def _gelu(x):
    return 0.5 * x * (1.0 + lax.erf(x * (2.0 ** -0.5)))


def _softplus(x):
    return jnp.maximum(x, 0.0) + jnp.log(1.0 + jnp.exp(-jnp.abs(x)))


def _row_spec(T, width, cb):
    return pl.BlockSpec((T, width), lambda i: (i, cb))


def _whole_spec(p):
    nd = p.ndim
    return pl.BlockSpec(p.shape, lambda i: (0,) * nd)


def _load_params(refs, kinds, with_zeros):
    out = []
    for r, k in zip(refs, kinds):
        if k == 'w':
            out.append((r[...], jnp.zeros(r.shape, F32) if with_zeros else None))
        else:
            out.append(r[...].astype(F32))
    return out


def rows_fwd(f, rows, params, kinds, outs, T, name):
    S = rows[0][0].shape[0]
    nr, npar = len(rows), len(params)

    def body(*refs):
        r = [x[...].astype(F32) for x in refs[:nr]]
        p = _load_params(refs[nr:nr + npar], kinds, False)
        res = f(*r, *p)
        for o_ref, o in zip(refs[nr + npar:], res):
            o_ref[...] = o.astype(o_ref.dtype)

    return pl.pallas_call(
        body, grid=(S // T,), name=name,
        in_specs=[_row_spec(T, w, cb) for _, w, cb in rows] + [_whole_spec(p) for p in params],
        out_specs=[_row_spec(T, w, 0) for w, _ in outs],
        out_shape=[jax.ShapeDtypeStruct((S, w), dt) for w, dt in outs],
        compiler_params=_cparams("parallel"),
    )(*[a for a, _, _ in rows], *params)


def rows_bwd(f, rows, params, kinds, cts, row_dtypes, T, name):
    S = rows[0][0].shape[0]
    nr, npar, nc = len(rows), len(params), len(cts)
    want = [i for i, dt in enumerate(row_dtypes) if dt is not None]

    def body(*refs):
        r = [x[...].astype(F32) for x in refs[:nr]]
        p = _load_params(refs[nr:nr + npar], kinds, True)
        g = [x[...].astype(F32) for x in refs[nr + npar:nr + npar + nc]]
        d_rows = refs[nr + npar + nc:nr + npar + nc + len(want)]
        d_params = refs[nr + npar + nc + len(want):]
        _, vjp = jax.vjp(f, *r, *p)
        grads = vjp(tuple(g))
        for o_ref, i in zip(d_rows, want):
            o_ref[...] = grads[i].astype(o_ref.dtype)
        first = pl.program_id(0) == 0
        for o_ref, gp, k in zip(d_params, grads[nr:], kinds):
            gp = gp[1] if k == 'w' else gp

            @pl.when(first)
            def _():
                o_ref[...] = gp

            @pl.when(jnp.logical_not(first))
            def _():
                o_ref[...] += gp

    res = pl.pallas_call(
        body, grid=(S // T,), name=name,
        in_specs=([_row_spec(T, w, cb) for _, w, cb in rows] + [_whole_spec(p) for p in params]
                  + [_row_spec(T, c.shape[1], 0) for c in cts]),
        out_specs=[_row_spec(T, rows[i][1], 0) for i in want] + [_whole_spec(p) for p in params],
        out_shape=([jax.ShapeDtypeStruct((S, rows[i][1]), row_dtypes[i]) for i in want]
                   + [jax.ShapeDtypeStruct(p.shape, F32) for p in params]),
        compiler_params=_cparams("arbitrary"),
    )(*[a for a, _, _ in rows], *params, *cts)
    return res[:len(want)], res[len(want):]


def f_norm(x, g):
    return (_rms(x, g),)


def f_glu(a1, a2):
    return (a1 * _sigmoid(a2),)


def f_lnsilu(cv, cb, g, b):
    return (_silu(_ln(cv + cb, g, b)),)


def f_gbeta(bd, alp, dtp):
    j = lax.broadcasted_iota(jnp.int32, (128, MIX_W), 0)
    head = lax.broadcasted_iota(jnp.int32, (128, MIX_W), 1) // DN_HEAD_DIM
    e_lo = (j == head).astype(F32)
    e_hi = (j == head + DN_HEADS).astype(F32)
    beta = _sigmoid(mmh(bd, e_lo))
    a_log = jnp.sum(mmh(alp, e_lo), axis=0, keepdims=True)
    dt_bias = jnp.sum(mmh(dtp, e_lo), axis=0, keepdims=True)
    g = -jnp.exp(a_log) * _softplus(mmh(bd, e_hi) + dt_bias)
    return g, beta


def f_gmlp(u_in, v_in, lg, lb, ws, b0, b1, b2, b3):
    T = u_in.shape[0]
    u = _gelu(u_in)
    vg = _ln(_gelu(v_in), lg, lb)
    tril = (lax.broadcasted_iota(jnp.int32, (GM_CHUNK, GM_CHUNK), 0)
            >= lax.broadcasted_iota(jnp.int32, (GM_CHUNK, GM_CHUNK), 1))
    bias = (b0, b1, b2, b3)
    chunks = []
    for r in range(T // GM_CHUNK):
        vr = vg[r * GM_CHUNK:(r + 1) * GM_CHUNK]
        cols = []
        for gi in range(GM_GROUPS):
            w = jnp.where(tril, ws[gi], 0.0)
            cols.append(mm(w, vr[:, gi * 128:(gi + 1) * 128]) + bias[gi])
        chunks.append(jnp.concatenate(cols, axis=1))
    mixed = chunks[0] if len(chunks) == 1 else jnp.concatenate(chunks, axis=0)
    return (u * mixed,)


def f_pool(cs, xin, pw, scale):
    T = cs.shape[0]
    t = pl.program_id(0) * T + lax.broadcasted_iota(jnp.int32, (T, 128), 0)
    cols = []
    for gi, win in enumerate(POOL_WINDOWS):
        count = jnp.minimum(t + 1, win).astype(F32)
        sl = slice(gi * 128, (gi + 1) * 128)
        cols.append(mm(cs[:, sl] / count - xin[:, sl], pw[gi]))
    return (jnp.concatenate(cols, axis=1) * scale,)


def f_merge(a, o, c, p, gate, wb):
    D = gate.shape[1] // 4
    acc = None
    for n, br in enumerate((a, o, c, p)):
        term = _sigmoid(gate[:, n * D:(n + 1) * D]) * mmw(br, wsel(wb, n))
        acc = term if acc is None else acc + term
    return (acc,)


def f_kv(mem, nm, wkv):
    return (mmw(_rms(mem, nm), wkv),)


def f_xattn(x, kv, nx, wq, wo):
    D = x.shape[1]
    hd = D // XA_HEADS
    q = mmw(_rms(x, nx), wq)
    heads = []
    for h in range(XA_HEADS):
        s = mm_nt(q[:, h * hd:(h + 1) * hd], kv[:, h * hd:(h + 1) * hd]) * (hd ** -0.5)
        s = s - jnp.max(s, axis=-1, keepdims=True)
        e = jnp.exp(s)
        pr = e / jnp.sum(e, axis=-1, keepdims=True)
        heads.append(mm(pr, kv[:, D + h * hd:D + (h + 1) * hd]))
    return (x + mmw(jnp.concatenate(heads, axis=1), wo),)


def f_loss(x, tgt, nf):
    err = _rms(x, nf) - tgt
    return (0.5 * jnp.sum(jnp.mean(err * err, axis=-1, keepdims=True), axis=0, keepdims=True),)


def _mm_tiles(mode, M, N, K):
    wide = 512 if N % 512 == 0 else 384
    if mode == 'tn':
        return min(M, 512), wide, K
    tm = min(M, 1024)
    if K <= 1024:
        if N % 512 == 0:
            return tm, wide, K
        return min(M, 512), N // 3, K
    if K % 2048 == 0:
        return tm, min(N, 1024) if mode == 'nt' else wide, 2048
    return tm, min(N, 1024), K // 3


def matmul(a, b, mode, *, name, out_dtype=F32, res=None, act=None, gate=None):
    M, K = a.shape if mode != 'tn' else a.shape[::-1]
    tm, tn, tk = _mm_tiles(mode, M, b.shape[0] if mode == 'nt' else b.shape[1], K)
    N = b.shape[0] if mode == 'nt' else b.shape[1]
    assert M % tm == 0 and N % tn == 0 and K % tk == 0, (a.shape, b.shape, mode, tm, tn, tk)
    nk = K // tk
    size = lambda t: t.size * t.dtype.itemsize
    rows_outer = size(a) + (M // tm) * size(b) <= size(b) + (N // tn) * size(a)

    def spec(shape, index):
        if rows_outer:
            return pl.BlockSpec(shape, lambda i, j, k: index(i, j, k))
        return pl.BlockSpec(shape, lambda j, i, k: index(i, j, k))

    if mode == 'nn':
        a_spec, b_spec = spec((tm, tk), lambda i, j, k: (i, k)), spec((tk, tn), lambda i, j, k: (k, j))
        ca, cb = 1, 0
    elif mode == 'nt':
        a_spec, b_spec = spec((tm, tk), lambda i, j, k: (i, k)), spec((tn, tk), lambda i, j, k: (j, k))
        ca, cb = 1, 1
    else:
        a_spec, b_spec = spec((tk, tm), lambda i, j, k: (k, i)), spec((tk, tn), lambda i, j, k: (k, j))
        ca, cb = 0, 0
    o_spec = spec((tm, tn), lambda i, j, k: (i, j))
    extra = [e for e in (res, gate) if e is not None]

    def body(*refs):
        a_ref, b_ref = refs[:2]
        e_refs = refs[2:2 + len(extra)]
        o_refs = refs[2 + len(extra):2 + len(extra) + (2 if act else 1)]
        part = _dot(a_ref[...], b_ref[...], ca, cb)

        def finish(acc):
            if res is not None:
                acc = acc + e_refs[0][...]
            if gate is not None:
                acc = acc * (2.0 * jnp.maximum(e_refs[-1][...], 0.0))
            o_refs[0][...] = acc.astype(o_refs[0].dtype)
            if act:
                r = jnp.maximum(acc, 0.0)
                o_refs[1][...] = (r * r).astype(o_refs[1].dtype)

        if nk == 1:
            finish(part)
        else:
            acc_ref = refs[-1]
            k = pl.program_id(2)

            @pl.when(k == 0)
            def _():
                acc_ref[...] = part

            @pl.when(k > 0)
            def _():
                acc_ref[...] += part

            @pl.when(k == nk - 1)
            def _():
                finish(acc_ref[...])

    out_shape = [jax.ShapeDtypeStruct((M, N), out_dtype)]
    if act:
        out_shape.append(jax.ShapeDtypeStruct((M, N), BF16))
    res_ = pl.pallas_call(
        body, grid=(M // tm, N // tn, nk) if rows_outer else (N // tn, M // tm, nk), name=name,
        in_specs=[a_spec, b_spec] + [o_spec] * len(extra),
        out_specs=[o_spec] * len(out_shape), out_shape=out_shape,
        scratch_shapes=[pltpu.VMEM((tm, tn), F32)] if nk > 1 else [],
        compiler_params=_cparams("parallel", "parallel", "arbitrary"),
    )(a, b, *extra)
    return res_ if act else res_[0]


def _conv_rows(S):
    return min(S, 512)


def conv_fwd(x, cb0, w, name):
    S = x.shape[0]
    K = w.shape[0]
    R = _conv_rows(S)

    def body(x_ref, w_ref, y_ref, pad_ref):
        pad_ref[pl.ds(0, CONV_PAD), :] = jnp.zeros((CONV_PAD, 128), F32)
        pad_ref[pl.ds(CONV_PAD, S), :] = x_ref[...]
        wv = w_ref[...]

        def chunk(r, carry):
            r0 = pl.multiple_of(r * R, R)
            win = pad_ref[pl.ds(r0, R + CONV_PAD), :]
            acc = jnp.zeros((R, 128), F32)
            for s in range(K):
                sh = win if s == 0 else pltpu.roll(win, s, 0)
                acc = acc + sh[CONV_PAD:, :] * wv[K - 1 - s:K - s, :]
            y_ref[pl.ds(r0, R), :] = acc
            return carry

        lax.fori_loop(0, S // R, chunk, 0)

    return pl.pallas_call(
        body, grid=(4,), name=name,
        in_specs=[pl.BlockSpec((S, 128), lambda j: (0, cb0 * 4 + j)), pl.BlockSpec((K, 128), lambda j: (0, j))],
        out_specs=pl.BlockSpec((S, 128), lambda j: (0, j)),
        out_shape=jax.ShapeDtypeStruct((S, MIX_W), F32),
        scratch_shapes=[pltpu.VMEM((S + CONV_PAD, 128), F32)],
        compiler_params=_cparams("parallel"),
    )(x, w)


def conv_bwd(x, cb0, w, dy, name, add=None, out_dtype=F32):
    S = x.shape[0]
    K = w.shape[0]
    R = _conv_rows(S)
    W = R + CONV_PAD

    def body(*refs):
        x_ref, w_ref, dy_ref = refs[:3]
        add_ref = refs[3] if add is not None else None
        dx_ref, dw_ref, xpad_ref, dypad_ref = refs[-4:]
        xpad_ref[pl.ds(0, CONV_PAD), :] = jnp.zeros((CONV_PAD, 128), F32)
        xpad_ref[pl.ds(CONV_PAD, S), :] = x_ref[...]
        dypad_ref[pl.ds(S, CONV_PAD), :] = jnp.zeros((CONV_PAD, 128), F32)
        dypad_ref[pl.ds(0, S), :] = dy_ref[...].astype(F32)
        dw_ref[...] = jnp.zeros((K, 128), F32)
        wv = w_ref[...]

        def chunk(r, carry):
            r0 = pl.multiple_of(r * R, R)
            xwin = xpad_ref[pl.ds(r0, W), :]
            dwin = dypad_ref[pl.ds(r0, W), :]
            dyc = dwin[:R, :]
            acc = jnp.zeros((R, 128), F32)
            for s in range(K):
                up = dwin if s == 0 else pltpu.roll(dwin, W - s, 0)
                acc = acc + up[:R, :] * wv[K - 1 - s:K - s, :]
                xs = xwin if s == 0 else pltpu.roll(xwin, s, 0)
                dw_ref[pl.ds(K - 1 - s, 1), :] += jnp.sum(dyc * xs[CONV_PAD:, :], axis=0, keepdims=True)
            if add_ref is not None:
                acc = acc + add_ref[pl.ds(r0, R), :].astype(F32)
            dx_ref[pl.ds(r0, R), :] = acc.astype(dx_ref.dtype)
            return carry

        lax.fori_loop(0, S // R, chunk, 0)

    col = pl.BlockSpec((S, 128), lambda j: (0, j))
    ins = [x, w, dy] + ([add] if add is not None else [])
    return pl.pallas_call(
        body, grid=(4,), name=name,
        in_specs=[pl.BlockSpec((S, 128), lambda j: (0, cb0 * 4 + j)), pl.BlockSpec((K, 128), lambda j: (0, j)), col]
        + ([col] if add is not None else []),
        out_specs=[col, pl.BlockSpec((K, 128), lambda j: (0, j))],
        out_shape=[jax.ShapeDtypeStruct((S, MIX_W), out_dtype), jax.ShapeDtypeStruct((K, MIX_W), F32)],
        scratch_shapes=[pltpu.VMEM((S + CONV_PAD, 128), F32), pltpu.VMEM((S + CONV_PAD, 128), F32)],
        compiler_params=_cparams("parallel"),
    )(*ins)


DN_PREP_ROWS = 2 * DN_CHUNK


def _unit_lower_inverse(a):
    C = a.shape[0]
    eye = (lax.broadcasted_iota(jnp.int32, (C, C), 0) == lax.broadcasted_iota(jnp.int32, (C, C), 1)).astype(F32)
    inv = eye - a
    pw = a
    for _ in range(5):
        pw = mmh(pw, pw)
        inv = inv + mmh(inv, pw)
    return inv


@jax.custom_vjp
def _known_inverse(a, inv):
    return inv


_known_inverse.defvjp(lambda a, inv: (inv, inv),
                      lambda inv, g: (-mmh_nt(mmh_tn(inv, g), inv), jnp.zeros_like(inv)))


def _delta_prep(qc, kc, vc, ge, be, inv_known=None):
    C, Dh = DN_CHUNK, DN_HEAD_DIM
    ii = lax.broadcasted_iota(jnp.int32, (C, C), 0)
    jj = lax.broadcasted_iota(jnp.int32, (C, C), 1)
    causal, strict = ii >= jj, ii > jj
    tri = causal.astype(F32)
    us, ws, qds, kds, scs, attns, invs = [], [], [], [], [], [], []
    for h in range(DN_HEADS):
        sl = slice(h * Dh, (h + 1) * Dh)
        q, k, v = _silu(qc[:, sl]), _silu(kc[:, sl]), _silu(vc[:, sl])
        q = q * lax.rsqrt(jnp.sum(q * q, axis=-1, keepdims=True) + 1e-6) * (Dh ** -0.5)
        k = k * lax.rsqrt(jnp.sum(k * k, axis=-1, keepdims=True) + 1e-6)
        g, beta = ge[:, sl], be[:, sl]
        gam = mmh(tri, g)
        g_last = mmh(jnp.ones((C, C), F32), g)
        gam_col = gam[:, :C]
        gam_row = mmh_nt(jnp.full((C, Dh), 1.0 / Dh, F32), gam)
        decay = jnp.where(causal, jnp.exp(jnp.where(causal, gam_col - gam_row, 0.0)), 0.0)
        kb = k * beta
        a = jnp.where(strict, mm_nt(kb, k) * decay, 0.0)
        inv = _unit_lower_inverse(a) if inv_known is None else _known_inverse(a, inv_known[h])
        e_gam = jnp.exp(gam)
        us.append(mmh(inv, v * beta))
        ws.append(mmh(inv, kb * e_gam))
        qds.append(q * e_gam)
        kds.append(k * jnp.exp(g_last - gam))
        scs.append(jnp.exp(g_last))
        attns.append(mm_nt(q, k) * decay)
        invs.append(inv)
    cat = lambda parts: jnp.concatenate(parts, axis=1)
    return cat(us), cat(ws), cat(qds), cat(kds), cat(scs), attns, invs


def _delta_step(states, u, w, qd, kd, sc, attns, gate, ng):
    Dh = DN_HEAD_DIM
    new_states, outs = [], []
    for h in range(DN_HEADS):
        sl = slice(h * Dh, (h + 1) * Dh)
        s_in = states[h]
        v_new = u[:, sl] - mm(w[:, sl], s_in)
        o = mm(qd[:, sl], s_in) + mm(attns[h], v_new)
        new_states.append(s_in * jnp.concatenate([sc[:, sl], sc[:, sl]], axis=0) + mm_tn(kd[:, sl], v_new))
        outs.append(_rms(o, ng) * _silu(gate[:, sl]))
    return new_states, jnp.concatenate(outs, axis=1)


def _prep_rows(fn, blocks, inv_blocks=None):
    n = blocks[0].shape[0] // DN_CHUNK
    res = []
    for r in range(n):
        rows = slice(r * DN_CHUNK, (r + 1) * DN_CHUNK)
        known = None if inv_blocks is None else [iv[rows] for iv in inv_blocks]
        res.append(fn(*[b[rows] for b in blocks], known))
    if n == 1:
        return res[0]
    return jax.tree.map(lambda *parts: jnp.concatenate(parts, axis=0), *res)


def _maybe_carrying(body, in_specs, out_specs, out_shape, operands, ride, steps, name):
    if ride is None:
        res = pl.pallas_call(body, grid=(steps,), name=name, in_specs=in_specs, out_specs=out_specs,
                             out_shape=out_shape, compiler_params=_cparams("parallel"))(*operands)
        return res, None
    res = pl.pallas_call(
        carry(body, len(in_specs), len(out_specs), ride, steps), grid=(steps,), name=name,
        in_specs=in_specs + [ANY] * len(ride.inputs), out_specs=out_specs + [ANY] * len(ride.out_shape),
        out_shape=out_shape + ride.out_shape, scratch_shapes=ride.scratch, compiler_params=_cparams("arbitrary"),
    )(*operands, *ride.inputs)
    return res[:len(out_specs)], res[len(out_specs):]


def delta_prep_fwd(qc, kc, vc, ge, be, name, ride=None):
    S = qc.shape[0]
    R = min(S, DN_PREP_ROWS)

    def body(q_ref, k_ref, v_ref, ge_ref, be_ref, u_ref, w_ref, qd_ref, kd_ref, sc_ref, at_ref, iv_ref):
        u, w, qd, kd, sc, attns, invs = _prep_rows(
            _delta_prep, [q_ref[...], k_ref[...], v_ref[...], ge_ref[...], be_ref[...]])
        for ref, val in zip((u_ref, w_ref, qd_ref, kd_ref, sc_ref), (u, w, qd, kd, sc)):
            ref[...] = val
        for h in range(DN_HEADS):
            at_ref[h] = attns[h]
            iv_ref[h] = invs[h]

    blk = pl.BlockSpec((R, MIX_W), lambda n: (n, 0))
    hblk = pl.BlockSpec((DN_HEADS, R, DN_CHUNK), lambda n: (0, n, 0))
    wide = jax.ShapeDtypeStruct((S, MIX_W), F32)
    narrow = jax.ShapeDtypeStruct((DN_HEADS, S, DN_CHUNK), F32)
    return _maybe_carrying(body, [blk] * 5, [blk] * 5 + [hblk] * 2, [wide] * 5 + [narrow] * 2, [qc, kc, vc, ge, be],
                           ride, S // R, name)


def delta_prep_bwd(qc, kc, vc, ge, be, inv, cts, name, ride=None):
    S = qc.shape[0]
    R = min(S, DN_PREP_ROWS)

    def body(q_ref, k_ref, v_ref, ge_ref, be_ref, iv_ref, du_ref, dw_ref, dqd_ref, dkd_ref, dsc_ref, dat_ref, *outs):
        invs = [iv_ref[h] for h in range(DN_HEADS)]
        fn = lambda *blocks: _prep_rows(_delta_prep, list(blocks), invs)[:6]
        _, vjp = jax.vjp(fn, q_ref[...], k_ref[...], v_ref[...], ge_ref[...], be_ref[...])
        grads = vjp((du_ref[...], dw_ref[...], dqd_ref[...], dkd_ref[...], dsc_ref[...],
                     [dat_ref[h] for h in range(DN_HEADS)]))
        for ref, g in zip(outs, grads):
            ref[...] = g

    blk = pl.BlockSpec((R, MIX_W), lambda n: (n, 0))
    hblk = pl.BlockSpec((DN_HEADS, R, DN_CHUNK), lambda n: (0, n, 0))
    return _maybe_carrying(body, [blk] * 5 + [hblk] + [blk] * 5 + [hblk], [blk] * 5,
                           [jax.ShapeDtypeStruct((S, MIX_W), F32)] * 5, [qc, kc, vc, ge, be, inv, *cts], ride, S // R, name)


def delta_step_fwd(prep, z, ng, name):
    S = prep[0].shape[0]
    N = S // DN_CHUNK
    C = DN_CHUNK

    def body(u_ref, w_ref, qd_ref, kd_ref, sc_ref, at_ref, gate_ref, ng_ref, o_ref, st_ref, s_ref):
        @pl.when(pl.program_id(0) == 0)
        def _():
            s_ref[...] = jnp.zeros(s_ref.shape, F32)

        states = [s_ref[h] for h in range(DN_HEADS)]
        for h in range(DN_HEADS):
            st_ref[0, h] = states[h]
        new_states, o = _delta_step(states, u_ref[...], w_ref[...], qd_ref[...], kd_ref[...], sc_ref[...],
                                    [at_ref[h] for h in range(DN_HEADS)], gate_ref[...], ng_ref[...])
        for h in range(DN_HEADS):
            s_ref[h] = new_states[h]
        o_ref[...] = o.astype(o_ref.dtype)

    blk = pl.BlockSpec((C, MIX_W), lambda n: (n, 0))
    return pl.pallas_call(
        body, grid=(N,), name=name,
        in_specs=[blk] * 5 + [pl.BlockSpec((DN_HEADS, C, C), lambda n: (0, n, 0)),
                              pl.BlockSpec((C, MIX_W), lambda n: (n, ZB_DGATE)),
                              pl.BlockSpec((1, DN_HEAD_DIM), lambda n: (0, 0))],
        out_specs=[blk, pl.BlockSpec((1, DN_HEADS, DN_HEAD_DIM, DN_HEAD_DIM), lambda n: (n, 0, 0, 0))],
        out_shape=[jax.ShapeDtypeStruct((S, MIX_W), BF16),
                   jax.ShapeDtypeStruct((N, DN_HEADS, DN_HEAD_DIM, DN_HEAD_DIM), F32)],
        scratch_shapes=[pltpu.VMEM((DN_HEADS, DN_HEAD_DIM, DN_HEAD_DIM), F32)],
        compiler_params=_cparams("arbitrary"),
    )(*prep, z, ng)


def delta_step_bwd(prep, z, ng, states, do, name):
    S = prep[0].shape[0]
    N = S // DN_CHUNK
    C = DN_CHUNK

    def body(u_ref, w_ref, qd_ref, kd_ref, sc_ref, at_ref, gate_ref, ng_ref, st_ref, do_ref,
             du_ref, dw_ref, dqd_ref, dkd_ref, dsc_ref, dat_ref, dgate_ref, dng_ref, ds_ref):
        first = pl.program_id(0) == 0

        @pl.when(first)
        def _():
            ds_ref[...] = jnp.zeros(ds_ref.shape, F32)

        args = ([st_ref[0, h] for h in range(DN_HEADS)], u_ref[...], w_ref[...], qd_ref[...], kd_ref[...],
                sc_ref[...], [at_ref[h] for h in range(DN_HEADS)], gate_ref[...].astype(F32), ng_ref[...])
        _, vjp = jax.vjp(_delta_step, *args)
        d_states, du, dw, dqd, dkd, dsc, dat, dgate, dng = vjp(([ds_ref[h] for h in range(DN_HEADS)],
                                                               do_ref[...].astype(F32)))
        for h in range(DN_HEADS):
            ds_ref[h] = d_states[h]
            dat_ref[h] = dat[h]
        for ref, g in zip((du_ref, dw_ref, dqd_ref, dkd_ref, dsc_ref), (du, dw, dqd, dkd, dsc)):
            ref[...] = g
        dgate_ref[...] = dgate.astype(dgate_ref.dtype)

        @pl.when(first)
        def _():
            dng_ref[...] = dng

        @pl.when(jnp.logical_not(first))
        def _():
            dng_ref[...] += dng

    blk = pl.BlockSpec((C, MIX_W), lambda n: (N - 1 - n, 0))
    hblk = pl.BlockSpec((DN_HEADS, C, C), lambda n: (0, N - 1 - n, 0))
    ngs = pl.BlockSpec((1, DN_HEAD_DIM), lambda n: (0, 0))
    f32o = jax.ShapeDtypeStruct((S, MIX_W), F32)
    return pl.pallas_call(
        body, grid=(N,), name=name,
        in_specs=[blk] * 5 + [hblk, pl.BlockSpec((C, MIX_W), lambda n: (N - 1 - n, ZB_DGATE)), ngs,
                              pl.BlockSpec((1, DN_HEADS, DN_HEAD_DIM, DN_HEAD_DIM), lambda n: (N - 1 - n, 0, 0, 0)),
                              blk],
        out_specs=[blk] * 5 + [hblk, blk, ngs],
        out_shape=[f32o] * 5 + [jax.ShapeDtypeStruct((DN_HEADS, S, C), F32), jax.ShapeDtypeStruct((S, MIX_W), BF16),
                                jax.ShapeDtypeStruct((1, DN_HEAD_DIM), F32)],
        scratch_shapes=[pltpu.VMEM((DN_HEADS, DN_HEAD_DIM, DN_HEAD_DIM), F32)],
        compiler_params=_cparams("arbitrary"),
    )(*prep, z, ng, states, do)


ANY = pl.BlockSpec(memory_space=pl.ANY)


def _place():
    return lax.axis_index("x"), lax.axis_index("y"), lax.axis_index("c")


PHASES = ('start', 'forward', 'finish')


class Ride:
    def __init__(self, inputs, out_shape, scratch, run):
        self.inputs, self.out_shape, self.scratch, self.run = list(inputs), list(out_shape), list(scratch), run


def run_alone(ride, name):
    n_in, n_out = len(ride.inputs), len(ride.out_shape)

    def body(*refs):
        for phase in PHASES:
            ride.run(phase, refs[:n_in], refs[n_in:n_in + n_out], refs[n_in + n_out:])

    return pl.pallas_call(body, name=name, in_specs=[ANY] * n_in, out_specs=[ANY] * n_out, out_shape=ride.out_shape,
                          scratch_shapes=ride.scratch)(*ride.inputs)


def carry(body, n_in, n_out, ride, steps):
    r_in, r_out = len(ride.inputs), len(ride.out_shape)

    def carrying(*refs):
        b = n_in + r_in
        ride_refs = (refs[n_in:b], refs[b + n_out:b + n_out + r_out], refs[b + n_out + r_out:])
        step = pl.program_id(0)

        @pl.when(step == 0)
        def _():
            ride.run('start', *ride_refs)

        @pl.when(step == steps // 2)
        def _():
            ride.run('forward', *ride_refs)

        body(*refs[:n_in], *refs[b:b + n_out])

        @pl.when(step == steps - 1)
        def _():
            ride.run('finish', *ride_refs)

    return carrying


def gather_ride(shards):
    n = len(shards)

    def run(phase, ins, outs, sems):
        send_sems, recv_sems = sems
        x, y, c = _place()
        me, sibling = (x, y, c), (x, y, 1 - c)
        chips = [(1 - x, y), (x, 1 - y), (1 - x, 1 - y)]

        def copy(a, k, block, to, src=None):
            row = 4 * block[0] + 2 * block[1] + block[2]
            return pltpu.make_async_remote_copy(
                src_ref=outs[a].at[row] if src is None else src, dst_ref=outs[a].at[row],
                send_sem=send_sems.at[a, k], recv_sem=recv_sems.at[a, k], device_id=to, device_id_type=MESH_ID)

        def first():
            return [cp for a in range(n) for cp in
                    [copy(a, 0, me, sibling, src=ins[a])]
                    + [copy(a, 1 + j, me, (*chip, c), src=ins[a]) for j, chip in enumerate(chips)]]

        def passed():
            return [copy(a, 4 + j, (*chip, c), sibling) for j, chip in enumerate(chips) for a in range(n)]

        if phase == 'start':
            for cp in first():
                cp.start()
        elif phase == 'forward':
            for cp, (j, chip, a) in zip(passed(), [(j, chip, a) for j, chip in enumerate(chips) for a in range(n)]):
                copy(a, 1 + j, (*chip, c), me).wait_recv()
                cp.start()
        else:
            for a in range(n):
                copy(a, 0, sibling, me).wait_recv()
                for j, chip in enumerate(chips):
                    copy(a, 4 + j, (*chip, 1 - c), me).wait_recv()
            for cp in first() + passed():
                cp.wait_send()

    return Ride(shards, [jax.ShapeDtypeStruct((8,) + s.shape, s.dtype) for s in shards],
                [pltpu.SemaphoreType.DMA((n, 7)), pltpu.SemaphoreType.DMA((n, 7))], run)


def all_gather(shards, name):
    return run_alone(gather_ride(shards), name)


def exchange_cores(grads, name):
    n = len(grads)

    def body(*refs):
        ins, outs = refs[:n], refs[n:2 * n]
        send_sems, recv_sems = refs[2 * n:]
        x, y, c = _place()
        copies = [pltpu.make_async_remote_copy(
            src_ref=ins[a].at[2 * k + 1 - c], dst_ref=outs[a].at[k], send_sem=send_sems.at[a, k],
            recv_sem=recv_sems.at[a, k], device_id=(x, y, 1 - c), device_id_type=MESH_ID)
            for a in range(n) for k in range(4)]
        for cp in copies:
            cp.start()
        for cp in copies:
            cp.wait()

    return pl.pallas_call(
        body, name=name, in_specs=[ANY] * n, out_specs=[ANY] * n,
        out_shape=[jax.ShapeDtypeStruct((4,) + g.shape[1:], g.dtype) for g in grads],
        scratch_shapes=[pltpu.SemaphoreType.DMA((n, 4))] * 2,
    )(*grads)


def chips_ride(parts):
    n = len(parts)

    def run(phase, ins, outs, sems):
        send_sems, recv_sems = sems
        x, y, c = _place()
        chips = [(1 - x, y), (x, 1 - y), (1 - x, 1 - y)]
        if phase == 'forward':
            return
        copies = [pltpu.make_async_remote_copy(
            src_ref=ins[a].at[2 * px + py], dst_ref=outs[a].at[j], send_sem=send_sems.at[a, j],
            recv_sem=recv_sems.at[a, j], device_id=(px, py, c), device_id_type=MESH_ID)
            for a in range(n) for j, (px, py) in enumerate(chips)]
        for cp in copies:
            cp.start() if phase == 'start' else cp.wait()

    return Ride(parts, [jax.ShapeDtypeStruct((3,) + p.shape[1:], p.dtype) for p in parts],
                [pltpu.SemaphoreType.DMA((n, 3))] * 2, run)


def exchange_chips(parts, name):
    return run_alone(chips_ride(parts), name)


def _as2d(a, lead=0):
    return a.reshape(a.shape[:lead] + (-1, a.shape[-1]))


def _row_tile(rows, cols, n_arrays):
    budget = VMEM_LIMIT_V7X // 2
    lanes = -(-cols // 128) * 128
    t = budget // (2 * n_arrays * lanes * 4)
    if t >= rows:
        return rows
    return max(16, t // 16 * 16)


def _index_operand(i):
    return jnp.asarray(i, jnp.int32).reshape(1)


def add_own_rows(grads, core, recv, name):
    _, R, C = grads.shape
    T = _row_tile(R, C, 3)

    def body(c_ref, g_ref, r_ref, o_ref):
        o_ref[...] = (g_ref[...].astype(F32) + r_ref[...].astype(F32)).astype(o_ref.dtype)

    blk = pl.BlockSpec((1, T, C), lambda k, i, c: (k, i, 0))
    return pl.pallas_call(
        body, name=name, out_shape=jax.ShapeDtypeStruct((4, R, C), grads.dtype),
        grid_spec=pltpu.PrefetchScalarGridSpec(
            num_scalar_prefetch=1, grid=(4, pl.cdiv(R, T)),
            in_specs=[pl.BlockSpec((1, T, C), lambda k, i, c: (2 * k + c[0], i, 0)), blk], out_specs=blk),
        compiler_params=_cparams("parallel", "parallel"),
    )(_index_operand(core), grads, recv)


def _adamw(w, g, m, v):
    m = ADAM_B1 * m + (1.0 - ADAM_B1) * g
    v = ADAM_B2 * v + (1.0 - ADAM_B2) * jnp.square(g)
    m_hat = m / (1.0 - ADAM_B1 ** ADAM_STEP)
    v_hat = v / (1.0 - ADAM_B2 ** ADAM_STEP)
    delta = -ADAM_LR * (m_hat / (jnp.sqrt(v_hat) + ADAM_EPS) + ADAM_WD * w)
    return delta, m, v


def adamw_sum(parts, w, m, v, name, own=None, own_row=None):
    P, R, C = parts.shape
    T = _row_tile(R, C, P + 8)
    has_own = own is not None

    def body(i_ref, *refs):
        refs = list(refs)
        own_ref = refs.pop(0) if has_own else None
        p_ref, w_ref, m_ref, v_ref, g_ref, d_ref, nm_ref, nv_ref = refs
        terms = ([own_ref[0]] if has_own else []) + [p_ref[k] for k in range(P)]
        terms = [t.astype(F32) for t in terms]
        g = terms[0]
        for t in terms[1:]:
            g = g + t
        d, nm, nv = _adamw(w_ref[...], g, m_ref[...], v_ref[...])
        g_ref[...] = g
        d_ref[...] = d
        nm_ref[...] = nm
        nv_ref[...] = nv

    blk = pl.BlockSpec((T, C), lambda i, r: (i, 0))
    in_specs = [pl.BlockSpec((P, T, C), lambda i, r: (0, i, 0)), blk, blk, blk]
    operands = [parts, w, m, v]
    if has_own:
        in_specs.insert(0, pl.BlockSpec((1, T, C), lambda i, r: (r[0], i, 0)))
        operands.insert(0, own)
    return pl.pallas_call(
        body, name=name, out_shape=[jax.ShapeDtypeStruct((R, C), F32)] * 4,
        grid_spec=pltpu.PrefetchScalarGridSpec(num_scalar_prefetch=1, grid=(pl.cdiv(R, T),), in_specs=in_specs,
                                               out_specs=[blk] * 4),
        compiler_params=_cparams("parallel"),
    )(_index_operand(0 if own_row is None else own_row), *operands)


def f_norm_res(x, g):
    return _rms(x, g), x


def _pool_taps():
    taps = np.zeros((16, MIX_W), np.float32)
    for gi, win in enumerate(POOL_WINDOWS):
        taps[16 - win:, gi * 128:(gi + 1) * 128] = 1.0
    return jnp.asarray(taps)


def loss_head(x, tgt, nf, T, name):
    S, D = x.shape

    def body(x_ref, t_ref, g_ref, l_ref, dx_ref, dg_ref):
        val, vjp = jax.vjp(f_loss, x_ref[...], t_ref[...], g_ref[...])
        dx, _, dg = vjp((jnp.ones((1, 1), F32),))
        dx_ref[...] = dx
        first = pl.program_id(0) == 0
        lv = jnp.broadcast_to(val[0], (1, 128))

        @pl.when(first)
        def _():
            l_ref[...] = lv
            dg_ref[...] = dg

        @pl.when(jnp.logical_not(first))
        def _():
            l_ref[...] += lv
            dg_ref[...] += dg

    row = pl.BlockSpec((T, D), lambda i: (i, 0))
    return pl.pallas_call(
        body, grid=(S // T,), name=name,
        in_specs=[row, row, pl.BlockSpec((1, D), lambda i: (0, 0))],
        out_specs=[pl.BlockSpec((1, 128), lambda i: (0, 0)), row, pl.BlockSpec((1, D), lambda i: (0, 0))],
        out_shape=[jax.ShapeDtypeStruct((1, 128), F32), jax.ShapeDtypeStruct((S, D), F32),
                   jax.ShapeDtypeStruct((1, D), F32)],
        compiler_params=_cparams("arbitrary"),
    )(x, tgt, nf)


def layer_fwd(x, mem, W, tag, ride=None):
    S, D = x.shape
    T = min(S, 256)
    sv = {'x': x}
    (h1,) = rows_fwd(f_norm, [(x, D, 0)], [W['norm_mix']], 'p', [(D, BF16)], T, tag + 'norm_mix')
    z = matmul(h1, W['w_in'], 'nn', name=tag + 'w_in')
    (a_pre,) = rows_fwd(f_glu, [(z, 512, ZB_A1), (z, 512, ZB_A2)], [], '', [(512, F32)], T, tag + 'glu')
    a_cv = conv_fwd(a_pre, 0, W['conv_a_w'], tag + 'conv_a')
    (a,) = rows_fwd(f_lnsilu, [(a_cv, 512, 0)], [W['conv_a_b'], W['ln_a_g'], W['ln_a_b']], 'ppp', [(512, BF16)], T,
                    tag + 'ln_a')
    qc = conv_fwd(z, ZB_Q, W['dn_wq'], tag + 'conv_q')
    kc = conv_fwd(z, ZB_K, W['dn_wk'], tag + 'conv_k')
    vc = conv_fwd(z, ZB_V, W['dn_wv'], tag + 'conv_v')
    ge, be = rows_fwd(f_gbeta, [(z, 128, ZB128_BD)], [W['alp'], W['dtp']], 'pp', [(512, F32), (512, F32)], T,
                      tag + 'gbeta')
    (*prep, dn_inv), rode = delta_prep_fwd(qc, kc, vc, ge, be, tag + 'delta_prep', ride)
    o, states = delta_step_fwd(prep, z, W['dn_norm_g'], tag + 'delta')
    gm_params = [W['gm_ln_g'], W['gm_ln_b'], W['gm_ws']] + W['gm_b']
    (c,) = rows_fwd(f_gmlp, [(z, 512, ZB_GU), (z, 512, ZB_GV)], gm_params, 'p' * 7, [(512, BF16)], T, tag + 'gmlp')
    cs = conv_fwd(z, ZB_POOL, _pool_taps(), tag + 'pool_sum')
    (p,) = rows_fwd(f_pool, [(cs, 512, 0), (z, 512, ZB_POOL)], [W['pool_w'], W['pool_scale']], 'pp', [(512, BF16)], T,
                    tag + 'pool')
    (merged,) = rows_fwd(f_merge, [(a, 512, 0), (o, 512, 0), (c, 512, 0), (p, 512, 0), (z, 4 * D, ZB_GATE)],
                         [W['w_branch']], 'w', [(D, BF16)], min(S, 128), tag + 'merge')
    x1 = matmul(merged, W['w_out'], 'nn', res=x, name=tag + 'w_out')
    (kv,) = rows_fwd(f_kv, [(mem, D, 0)], [W['norm_mem'], W['xa_wkv']], 'pw', [(2 * D, F32)], mem.shape[0],
                     tag + 'kv')
    (x2,) = rows_fwd(f_xattn, [(x1, D, 0)], [kv, W['norm_xa'], W['xa_wq'], W['xa_wo']], 'ppww', [(D, F32)], T,
                     tag + 'xattn')
    (h3,) = rows_fwd(f_norm, [(x2, D, 0)], [W['norm_mlp']], 'p', [(D, BF16)], T, tag + 'norm_mlp')
    pre, r = matmul(h3, W['mlp_w1'], 'nn', act='relu2', name=tag + 'mlp_w1')
    x3 = matmul(r, W['mlp_w2'], 'nn', res=x2, name=tag + 'mlp_w2')
    sv.update(h1=h1, z=z, a_pre=a_pre, a_cv=a_cv, a=a, qc=qc, kc=kc, vc=vc, ge=ge, be=be, prep=prep, dn_inv=dn_inv, o=o,
              states=states, c=c,
              cs=cs, p=p, merged=merged, x1=x1, kv=kv, x2=x2, h3=h3, pre=pre, r=r)
    return x3, sv, rode


def layer_bwd(dx, mem, W, sv, tag, ride=None):
    S, D = dx.shape
    T = min(S, 256)
    tag = tag + 'b_'
    G = {}
    z = sv['z']
    da = matmul(dx, W['mlp_w2'], 'nt', gate=sv['pre'], out_dtype=BF16, name=tag + 'mlp_da')
    G['mlp_w2'] = matmul(sv['r'], dx, 'tn', name=tag + 'mlp_gw2')
    dh3 = matmul(da, W['mlp_w1'], 'nt', name=tag + 'mlp_dh')
    G['mlp_w1'] = matmul(sv['h3'], da, 'tn', name=tag + 'mlp_gw1')
    (dx2,), (G['norm_mlp'],) = rows_bwd(f_norm_res, [(sv['x2'], D, 0)], [W['norm_mlp']], 'p', [dh3, dx], [F32], T,
                                        tag + 'norm_mlp')
    (dx1,), (dkv, G['norm_xa'], G['xa_wq'], G['xa_wo']) = rows_bwd(
        f_xattn, [(sv['x1'], D, 0)], [sv['kv'], W['norm_xa'], W['xa_wq'], W['xa_wo']], 'ppww', [dx2], [F32],
        min(S, 128), tag + 'xattn')
    _, (G['norm_mem'], G['xa_wkv']) = rows_bwd(f_kv, [(mem, D, 0)], [W['norm_mem'], W['xa_wkv']], 'pw', [dkv], [None],
                                               mem.shape[0], tag + 'kv')
    dmerged = matmul(dx1, W['w_out'], 'nt', out_dtype=BF16, name=tag + 'dmerged')
    G['w_out'] = matmul(sv['merged'], dx1, 'tn', name=tag + 'gw_out')
    (d_a, d_o, d_c, d_p, dz_gate), (G['w_branch'],) = rows_bwd(
        f_merge, [(sv['a'], 512, 0), (sv['o'], 512, 0), (sv['c'], 512, 0), (sv['p'], 512, 0), (z, 4 * D, ZB_GATE)],
        [W['w_branch']], 'w', [dmerged], [F32, F32, F32, F32, BF16], min(S, 128), tag + 'merge')
    (dcs, dpx), (G['pool_w'], G['pool_scale']) = rows_bwd(
        f_pool, [(sv['cs'], 512, 0), (z, 512, ZB_POOL)], [W['pool_w'], W['pool_scale']], 'pp', [d_p], [F32, F32], T,
        tag + 'pool')
    dz_pool, _ = conv_bwd(z, ZB_POOL, _pool_taps(), dcs, tag + 'pool_sum', add=dpx, out_dtype=BF16)
    gm_params = [W['gm_ln_g'], W['gm_ln_b'], W['gm_ws']] + W['gm_b']
    (dz_gu, dz_gv), gm_g = rows_bwd(f_gmlp, [(z, 512, ZB_GU), (z, 512, ZB_GV)], gm_params, 'p' * 7, [d_c],
                                    [BF16, BF16], T, tag + 'gmlp')
    G['gm_ln_g'], G['gm_ln_b'], G['gm_ws'] = gm_g[:3]
    G['gm_b'] = list(gm_g[3:])
    *d_prep, dz_dgate, G['dn_norm_g'] = delta_step_bwd(sv['prep'], z, W['dn_norm_g'], sv['states'], d_o, tag + 'delta')
    (dqc, dkc, dvc, dge, dbe), rode = delta_prep_bwd(sv['qc'], sv['kc'], sv['vc'], sv['ge'], sv['be'], sv['dn_inv'],
                                                     d_prep, tag + 'delta_prep', ride)
    (dz_bd,), (G['alp'], G['dtp']) = rows_bwd(f_gbeta, [(z, 128, ZB128_BD)], [W['alp'], W['dtp']], 'pp', [dge, dbe],
                                              [BF16], T, tag + 'gbeta')
    dz_q, G['dn_wq'] = conv_bwd(z, ZB_Q, W['dn_wq'], dqc, tag + 'conv_q', out_dtype=BF16)
    dz_k, G['dn_wk'] = conv_bwd(z, ZB_K, W['dn_wk'], dkc, tag + 'conv_k', out_dtype=BF16)
    dz_v, G['dn_wv'] = conv_bwd(z, ZB_V, W['dn_wv'], dvc, tag + 'conv_v', out_dtype=BF16)
    (da_cv,), (G['conv_a_b'], G['ln_a_g'], G['ln_a_b']) = rows_bwd(
        f_lnsilu, [(sv['a_cv'], 512, 0)], [W['conv_a_b'], W['ln_a_g'], W['ln_a_b']], 'ppp', [d_a], [F32], T,
        tag + 'ln_a')
    da_pre, G['conv_a_w'] = conv_bwd(sv['a_pre'], 0, W['conv_a_w'], da_cv, tag + 'conv_a')
    (dz_a1, dz_a2), _ = rows_bwd(f_glu, [(z, 512, ZB_A1), (z, 512, ZB_A2)], [], '', [da_pre], [BF16, BF16], T,
                                 tag + 'glu')
    dz = jnp.concatenate([dz_gate, dz_a1, dz_a2, dz_q, dz_k, dz_v, dz_dgate, dz_gu, dz_gv, dz_pool, dz_bd], axis=1)
    dh1 = matmul(dz, W['w_in'], 'nt', name=tag + 'dh1')
    G['w_in'] = matmul(sv['h1'], dz, 'tn', name=tag + 'gw_in')
    (dx0,), (G['norm_mix'],) = rows_bwd(f_norm_res, [(sv['x'], D, 0)], [W['norm_mix']], 'p', [dh1, dx1], [F32], T,
                                        tag + 'norm_mix')
    return dx0, G, rode


def local_step(x, mem, tgt, norm_f, n_layers, weights_of, fwd_ride=None, bwd_ride=None):
    saved, weights, carried = [], [], None
    for l in range(n_layers):
        weights.append(weights_of(l, carried))
        x, sv, carried = layer_fwd(x, mem, weights[l], f'l{l}_', fwd_ride(l) if fwd_ride else None)
        saved.append(sv)
    loss, dx, g_nf = loss_head(x, tgt, norm_f, min(x.shape[0], 256), 'loss_head')
    grads, rode, ride = [None] * n_layers, {}, None
    for l in reversed(range(n_layers)):
        dx, grads[l], res = layer_bwd(dx, mem, weights[l], saved[l], f'l{l}_', ride)
        if ride is not None:
            rode[l + 1] = res
        ride = bwd_ride(l, grads[l]) if bwd_ride else None
    return loss, dx, grads, g_nf, rode, ride


def _row(v):
    return v.reshape(1, -1)


def _lane_pad(v):
    return jnp.zeros((8, 128), F32).at[0, :v.shape[0]].set(v)


def layer_weights(full, l):
    w_in = full['w_in'][l]
    cols = [w_in[:, a:b] for a, b in Z_ORDER]
    cols.append(jnp.zeros((w_in.shape[0], Z_W - sum(b - a for a, b in Z_ORDER)), w_in.dtype))
    dn_w = full['dn_conv_w'][l]
    W = {n: _row(full[n][l]) for n in ('norm_mix', 'conv_a_b', 'ln_a_g', 'ln_a_b', 'dn_norm_g', 'gm_ln_g', 'gm_ln_b',
                                       'pool_scale', 'norm_xa', 'norm_mem', 'norm_mlp')}
    W.update(w_in=jnp.concatenate(cols, axis=1), conv_a_w=full['conv_a_w'][l],
             dn_wq=dn_w[:, :MIX_W], dn_wk=dn_w[:, MIX_W:2 * MIX_W], dn_wv=dn_w[:, 2 * MIX_W:],
             alp=_lane_pad(full['dn_a_log'][l]), dtp=_lane_pad(full['dn_dt_bias'][l]),
             gm_ws=full['gm_ws'][l], gm_b=[full['gm_bs'][l][g].reshape(GM_CHUNK, 1) for g in range(GM_GROUPS)],
             pool_w=full['pool_w'][l])
    for n in ('w_branch', 'w_out', 'xa_wq', 'xa_wkv', 'xa_wo', 'mlp_w1', 'mlp_w2'):
        W[n] = full[n][l]
    return W


def layer_grads(G):
    g_in = G['w_in']
    starts = np.cumsum([0] + [b - a for a, b in Z_ORDER])
    pieces = sorted(zip(Z_ORDER, starts[:-1]))
    out = {n: G[n].reshape(-1) for n in ('norm_mix', 'conv_a_b', 'ln_a_g', 'ln_a_b', 'dn_norm_g', 'gm_ln_g', 'gm_ln_b',
                                          'pool_scale', 'norm_xa', 'norm_mem', 'norm_mlp')}
    out.update(w_in=jnp.concatenate([g_in[:, s:s + b - a] for (a, b), s in pieces], axis=1),
               conv_a_w=G['conv_a_w'], dn_conv_w=jnp.concatenate([G['dn_wq'], G['dn_wk'], G['dn_wv']], axis=1),
               dn_a_log=G['alp'][0, :DN_HEADS], dn_dt_bias=G['dtp'][0, :DN_HEADS], gm_ws=G['gm_ws'],
               gm_bs=jnp.stack([b.reshape(-1) for b in G['gm_b']]), pool_w=G['pool_w'])
    for n in ('w_branch', 'w_out', 'xa_wq', 'xa_wkv', 'xa_wo', 'mlp_w1', 'mlp_w2'):
        out[n] = G[n]
    return out


COLUMN_SHARDED = ('w_in', 'conv_a_w', 'dn_conv_w', 'w_branch', 'xa_wkv', 'mlp_w1')


def _unshard(name, g):
    if name in COLUMN_SHARDED:
        t = jnp.moveaxis(g, 0, -2)
        return t.reshape(t.shape[:-2] + (t.shape[-2] * t.shape[-1],))
    t = jnp.moveaxis(g, 0, 1)
    return t.reshape((t.shape[0], t.shape[1] * t.shape[2]) + t.shape[3:])


def _shard_rows(name, g):
    if name in COLUMN_SHARDED:
        t = g.reshape(g.shape[:-1] + (8, g.shape[-1] // 8))
        return jnp.moveaxis(t, -2, 0)
    t = g.reshape((g.shape[0], 8, g.shape[1] // 8) + g.shape[2:])
    return jnp.moveaxis(t, 1, 0)


def _packed_rows(a):
    return -(-a.size // 1024) * 8


def _pack(arrays):
    tiles = [jnp.pad(a.reshape(-1), (0, _packed_rows(a) * 128 - a.size)).reshape(-1, 128) for a in arrays]
    return jnp.concatenate(tiles, axis=0)


def _unpack(packed, like):
    out, at = [], 0
    for a in like:
        rows = _packed_rows(a)
        out.append(packed[at:at + rows].reshape(-1)[:a.size].reshape(a.shape))
        at += rows
    return out


def kernel(x, mem, norm_mix, w_in, conv_a_w, conv_a_b, ln_a_g, ln_a_b, dn_conv_w, dn_a_log, dn_dt_bias, dn_norm_g, gm_ln_g, gm_ln_b, gm_ws, gm_bs, pool_w, pool_scale, w_branch, w_out, norm_xa, norm_mem, xa_wq, xa_wkv, xa_wo, norm_mlp, mlp_w1, mlp_w2, norm_f, loss_target, m_norm_mix, m_w_in, m_conv_a_w, m_conv_a_b, m_ln_a_g, m_ln_a_b, m_dn_conv_w, m_dn_a_log, m_dn_dt_bias, m_dn_norm_g, m_gm_ln_g, m_gm_ln_b, m_gm_ws, m_gm_bs, m_pool_w, m_pool_scale, m_w_branch, m_w_out, m_norm_xa, m_norm_mem, m_xa_wq, m_xa_wkv, m_xa_wo, m_norm_mlp, m_mlp_w1, m_mlp_w2, m_norm_f, v_norm_mix, v_w_in, v_conv_a_w, v_conv_a_b, v_ln_a_g, v_ln_a_b, v_dn_conv_w, v_dn_a_log, v_dn_dt_bias, v_dn_norm_g, v_gm_ln_g, v_gm_ln_b, v_gm_ws, v_gm_bs, v_pool_w, v_pool_scale, v_w_branch, v_w_out, v_norm_xa, v_norm_mem, v_xa_wq, v_xa_wkv, v_xa_wo, v_norm_mlp, v_mlp_w1, v_mlp_w2, v_norm_f):
    args = locals()
    w = {n: args[n] for n in WEIGHTS}
    m = {n: args['m_' + n] for n in WEIGHTS}
    v = {n: args['v_' + n] for n in WEIGHTS}

    px, py, pc = _place()
    me = 4 * px + 2 * py + pc

    def shards_of(l):
        return [w[n][l].astype(BF16) if n in SENT_AS_BF16 else w[n][l] for n in SHARDED]

    def weights_of(l, gathered):
        if gathered is None:
            gathered = all_gather(shards_of(l), 'gather_weights')
        full = {n: w[n][l:l + 1] for n in REPLICATED if n != 'norm_f'}
        for n, g, own in zip(SHARDED, gathered, shards_of(l)):
            full[n] = _unshard(n, lax.dynamic_update_index_in_dim(g, own, me, 0)[:, None])
        return layer_weights(full, 0)

    def fwd_ride(l):
        return gather_ride(shards_of(l + 1)) if l + 1 < DEPTH else None

    per_layer, sums = [None] * DEPTH, [None] * DEPTH

    def bwd_ride(l, G):
        per_layer[l] = layer_grads(G)
        rows = [_as2d(_shard_rows(n, per_layer[l][n][None]), 1).astype(BF16 if n in SENT_AS_BF16 else F32)
                for n in SHARDED]
        from_sibling = exchange_cores(rows, f'reduce_cores_l{l}')
        sums[l] = [add_own_rows(g, pc, r, f'reduce_add_l{l}_{n}') for n, g, r in zip(SHARDED, rows, from_sibling)]
        return chips_ride(sums[l])

    loss, grad_x, grads, g_nf, from_chips, last = local_step(x[0], mem[0], loss_target[0], _row(norm_f), DEPTH,
                                                             weights_of, fwd_ride, bwd_ride)
    from_chips[0] = run_alone(last, 'reduce_chips_l0')
    gfull = {n: jnp.stack([g[n] for g in per_layer]) for n in REPLICATED if n != 'norm_f'}
    gfull['norm_f'] = g_nf.reshape(-1)

    out = {}
    for k, n in enumerate(SHARDED):
        own = jnp.concatenate([lax.dynamic_index_in_dim(sums[l][k], 2 * px + py, 0) for l in range(DEPTH)], axis=1)
        parts = jnp.concatenate([from_chips[l][k] for l in range(DEPTH)], axis=1)
        res = adamw_sum(parts, _as2d(w[n]), _as2d(m[n]), _as2d(v[n]), 'adamw_' + n, own=own, own_row=0)
        out[n] = [r.reshape(w[n].shape) for r in res]
    packed = _pack([gfull[n] for n in REPLICATED])
    (partials,) = all_gather([packed], 'gather_small_grads')
    partials = lax.dynamic_update_index_in_dim(partials, packed, 4 * px + 2 * py + pc, 0)
    res = adamw_sum(partials, _pack([w[n] for n in REPLICATED]), _pack([m[n] for n in REPLICATED]),
                    _pack([v[n] for n in REPLICATED]), 'adamw_small')
    like = [w[n] for n in REPLICATED]
    for k, r in enumerate(res):
        for n, a in zip(REPLICATED, _unpack(r, like)):
            out.setdefault(n, [None] * 4)[k] = a

    total = lax.psum(loss[0, 0], ('x', 'y', 'c'))
    return (total, grad_x[None], *[out[n][0] for n in WEIGHTS], *[out[n][1] for n in WEIGHTS],
            *[out[n][2] for n in WEIGHTS], *[out[n][3] for n in WEIGHTS])
```

```python
import functools
import math

import numpy as np
import jax
import jax.numpy as jnp
from jax import lax
from jax.experimental import pallas as pl
from jax.experimental.pallas import tpu as pltpu

F32 = jnp.float32
BF16 = jnp.bfloat16
HIGH = lax.Precision.HIGH
MESH_ID = pl.DeviceIdType.MESH
VMEM_LIMIT_V7X = 56 << 20

DEPTH = 4
MIX_W = 512
DN_HEADS = 4
DN_HEAD_DIM = 128
DN_CHUNK = 64
GM_CHUNK = 128
GM_GROUPS = 4
POOL_WINDOWS = (2, 4, 8, 16)
XA_HEADS = 4
CONV_PAD = 32

ADAM_LR = 0.001
ADAM_B1 = 0.9
ADAM_B2 = 0.999
ADAM_EPS = 1e-08
ADAM_WD = 0.01
ADAM_STEP = 10

WEIGHTS = ['norm_mix', 'w_in', 'conv_a_w', 'conv_a_b', 'ln_a_g', 'ln_a_b', 'dn_conv_w', 'dn_a_log', 'dn_dt_bias',
           'dn_norm_g', 'gm_ln_g', 'gm_ln_b', 'gm_ws', 'gm_bs', 'pool_w', 'pool_scale', 'w_branch', 'w_out',
           'norm_xa', 'norm_mem', 'xa_wq', 'xa_wkv', 'xa_wo', 'norm_mlp', 'mlp_w1', 'mlp_w2', 'norm_f']
SHARDED = ['w_in', 'conv_a_w', 'dn_conv_w', 'w_branch', 'w_out', 'xa_wq', 'xa_wkv', 'xa_wo', 'mlp_w1', 'mlp_w2']
SENT_AS_BF16 = ['w_in', 'w_branch', 'w_out', 'xa_wq', 'xa_wkv', 'xa_wo', 'mlp_w1', 'mlp_w2']
REPLICATED = [n for n in WEIGHTS if n not in SHARDED]

Z_W = 8832
Z_ORDER = ((4616, 8712), (0, 3072), (3080, 4616), (3072, 3080))
ZB_GATE, ZB_A1, ZB_A2, ZB_Q, ZB_K, ZB_V, ZB_DGATE, ZB_GU, ZB_GV, ZB_POOL = 0, 8, 9, 10, 11, 12, 13, 14, 15, 16
ZB128_BD = 68


def _cparams(*sem):
    return pltpu.CompilerParams(dimension_semantics=sem, vmem_limit_bytes=VMEM_LIMIT_V7X)


def _dot(a, b, ca, cb):
    return lax.dot_general(a.astype(BF16), b.astype(BF16), (((ca,), (cb,)), ((), ())), preferred_element_type=F32)


def _doth(a, b, ca, cb):
    return lax.dot_general(a, b, (((ca,), (cb,)), ((), ())), precision=HIGH, preferred_element_type=F32)


def _make_mm(dot):
    @jax.custom_vjp
    def nn(a, b):
        return dot(a, b, 1, 0)
    nn.defvjp(lambda a, b: (dot(a, b, 1, 0), (a, b)), lambda r, g: (dot(g, r[1], 1, 1), dot(r[0], g, 0, 0)))

    @jax.custom_vjp
    def nt(a, b):
        return dot(a, b, 1, 1)
    nt.defvjp(lambda a, b: (dot(a, b, 1, 1), (a, b)), lambda r, g: (dot(g, r[1], 1, 0), dot(g, r[0], 0, 0)))

    @jax.custom_vjp
    def tn(a, b):
        return dot(a, b, 0, 0)
    tn.defvjp(lambda a, b: (dot(a, b, 0, 0), (a, b)), lambda r, g: (dot(r[1], g, 1, 1), dot(r[0], g, 1, 0)))
    return nn, nt, tn


mm, mm_nt, mm_tn = _make_mm(_dot)
mmh, mmh_nt, mmh_tn = _make_mm(_doth)


@jax.custom_vjp
def _mmw(a, w, wz):
    return _dot(a, w, 1, 0)


_mmw.defvjp(lambda a, w, wz: (_dot(a, w, 1, 0), (a, w)),
            lambda r, g: (_dot(g, r[1], 1, 1), jnp.zeros_like(r[1]), _dot(r[0], g, 0, 0)))


def mmw(a, wpair):
    w, wz = wpair
    return _dot(a, w, 1, 0) if wz is None else _mmw(a, w, wz)


def wsel(wpair, n):
    return (wpair[0][n], None if wpair[1] is None else wpair[1][n])


def _sigmoid(x):
    return 1.0 / (1.0 + jnp.exp(-x))


def _silu(x):
    return x * _sigmoid(x)


def _rms(x, g, eps=1e-6):
    return x * lax.rsqrt(jnp.mean(x * x, axis=-1, keepdims=True) + eps) * g


def _ln(x, g, b, eps=1e-5):
    mu = jnp.mean(x, axis=-1, keepdims=True)
    d = x - mu
    return d * lax.rsqrt(jnp.mean(d * d, axis=-1, keepdims=True) + eps) * g + b


def _gelu(x):
    return 0.5 * x * (1.0 + lax.erf(x * (2.0 ** -0.5)))


def _softplus(x):
    return jnp.maximum(x, 0.0) + jnp.log(1.0 + jnp.exp(-jnp.abs(x)))


def _row_spec(T, width, cb):
    return pl.BlockSpec((T, width), lambda i: (i, cb))


def _whole_spec(p):
    nd = p.ndim
    return pl.BlockSpec(p.shape, lambda i: (0,) * nd)


def _load_params(refs, kinds, with_zeros):
    out = []
    for r, k in zip(refs, kinds):
        if k == 'w':
            out.append((r[...], jnp.zeros(r.shape, F32) if with_zeros else None))
        else:
            out.append(r[...].astype(F32))
    return out


def rows_fwd(f, rows, params, kinds, outs, T, name):
    S = rows[0][0].shape[0]
    nr, npar = len(rows), len(params)

    def body(*refs):
        r = [x[...].astype(F32) for x in refs[:nr]]
        p = _load_params(refs[nr:nr + npar], kinds, False)
        res = f(*r, *p)
        for o_ref, o in zip(refs[nr + npar:], res):
            o_ref[...] = o.astype(o_ref.dtype)

    return pl.pallas_call(
        body, grid=(S // T,), name=name,
        in_specs=[_row_spec(T, w, cb) for _, w, cb in rows] + [_whole_spec(p) for p in params],
        out_specs=[_row_spec(T, w, 0) for w, _ in outs],
        out_shape=[jax.ShapeDtypeStruct((S, w), dt) for w, dt in outs],
        compiler_params=_cparams("parallel"),
    )(*[a for a, _, _ in rows], *params)


def rows_bwd(f, rows, params, kinds, cts, row_dtypes, T, name):
    S = rows[0][0].shape[0]
    nr, npar, nc = len(rows), len(params), len(cts)
    want = [i for i, dt in enumerate(row_dtypes) if dt is not None]

    def body(*refs):
        r = [x[...].astype(F32) for x in refs[:nr]]
        p = _load_params(refs[nr:nr + npar], kinds, True)
        g = [x[...].astype(F32) for x in refs[nr + npar:nr + npar + nc]]
        d_rows = refs[nr + npar + nc:nr + npar + nc + len(want)]
        d_params = refs[nr + npar + nc + len(want):]
        _, vjp = jax.vjp(f, *r, *p)
        grads = vjp(tuple(g))
        for o_ref, i in zip(d_rows, want):
            o_ref[...] = grads[i].astype(o_ref.dtype)
        first = pl.program_id(0) == 0
        for o_ref, gp, k in zip(d_params, grads[nr:], kinds):
            gp = gp[1] if k == 'w' else gp

            @pl.when(first)
            def _():
                o_ref[...] = gp

            @pl.when(jnp.logical_not(first))
            def _():
                o_ref[...] += gp

    res = pl.pallas_call(
        body, grid=(S // T,), name=name,
        in_specs=([_row_spec(T, w, cb) for _, w, cb in rows] + [_whole_spec(p) for p in params]
                  + [_row_spec(T, c.shape[1], 0) for c in cts]),
        out_specs=[_row_spec(T, rows[i][1], 0) for i in want] + [_whole_spec(p) for p in params],
        out_shape=([jax.ShapeDtypeStruct((S, rows[i][1]), row_dtypes[i]) for i in want]
                   + [jax.ShapeDtypeStruct(p.shape, F32) for p in params]),
        compiler_params=_cparams("arbitrary"),
    )(*[a for a, _, _ in rows], *params, *cts)
    return res[:len(want)], res[len(want):]


def f_norm(x, g):
    return (_rms(x, g),)


def f_glu(a1, a2):
    return (a1 * _sigmoid(a2),)


def f_lnsilu(cv, cb, g, b):
    return (_silu(_ln(cv + cb, g, b)),)


def f_gbeta(bd, alp, dtp):
    j = lax.broadcasted_iota(jnp.int32, (128, MIX_W), 0)
    head = lax.broadcasted_iota(jnp.int32, (128, MIX_W), 1) // DN_HEAD_DIM
    e_lo = (j == head).astype(F32)
    e_hi = (j == head + DN_HEADS).astype(F32)
    beta = _sigmoid(mmh(bd, e_lo))
    a_log = jnp.sum(mmh(alp, e_lo), axis=0, keepdims=True)
    dt_bias = jnp.sum(mmh(dtp, e_lo), axis=0, keepdims=True)
    g = -jnp.exp(a_log) * _softplus(mmh(bd, e_hi) + dt_bias)
    return g, beta


def f_gmlp(u_in, v_in, lg, lb, ws, b0, b1, b2, b3):
    T = u_in.shape[0]
    u = _gelu(u_in)
    vg = _ln(_gelu(v_in), lg, lb)
    tril = (lax.broadcasted_iota(jnp.int32, (GM_CHUNK, GM_CHUNK), 0)
            >= lax.broadcasted_iota(jnp.int32, (GM_CHUNK, GM_CHUNK), 1))
    bias = (b0, b1, b2, b3)
    chunks = []
    for r in range(T // GM_CHUNK):
        vr = vg[r * GM_CHUNK:(r + 1) * GM_CHUNK]
        cols = []
        for gi in range(GM_GROUPS):
            w = jnp.where(tril, ws[gi], 0.0)
            cols.append(mm(w, vr[:, gi * 128:(gi + 1) * 128]) + bias[gi])
        chunks.append(jnp.concatenate(cols, axis=1))
    mixed = chunks[0] if len(chunks) == 1 else jnp.concatenate(chunks, axis=0)
    return (u * mixed,)


def f_pool(cs, xin, pw, scale):
    T = cs.shape[0]
    t = pl.program_id(0) * T + lax.broadcasted_iota(jnp.int32, (T, 128), 0)
    cols = []
    for gi, win in enumerate(POOL_WINDOWS):
        count = jnp.minimum(t + 1, win).astype(F32)
        sl = slice(gi * 128, (gi + 1) * 128)
        cols.append(mm(cs[:, sl] / count - xin[:, sl], pw[gi]))
    return (jnp.concatenate(cols, axis=1) * scale,)


def f_merge(a, o, c, p, gate, wb):
    D = gate.shape[1] // 4
    acc = None
    for n, br in enumerate((a, o, c, p)):
        term = _sigmoid(gate[:, n * D:(n + 1) * D]) * mmw(br, wsel(wb, n))
        acc = term if acc is None else acc + term
    return (acc,)


def f_kv(mem, nm, wkv):
    return (mmw(_rms(mem, nm), wkv),)


def f_xattn(x, kv, nx, wq, wo):
    D = x.shape[1]
    hd = D // XA_HEADS
    q = mmw(_rms(x, nx), wq)
    heads = []
    for h in range(XA_HEADS):
        s = mm_nt(q[:, h * hd:(h + 1) * hd], kv[:, h * hd:(h + 1) * hd]) * (hd ** -0.5)
        s = s - jnp.max(s, axis=-1, keepdims=True)
        e = jnp.exp(s)
        pr = e / jnp.sum(e, axis=-1, keepdims=True)
        heads.append(mm(pr, kv[:, D + h * hd:D + (h + 1) * hd]))
    return (x + mmw(jnp.concatenate(heads, axis=1), wo),)


def f_loss(x, tgt, nf):
    err = _rms(x, nf) - tgt
    return (0.5 * jnp.sum(jnp.mean(err * err, axis=-1, keepdims=True), axis=0, keepdims=True),)


def _mm_tiles(mode, M, N, K):
    wide = 512 if N % 512 == 0 else 384
    if mode == 'tn':
        return min(M, 512), wide, K
    tm = min(M, 1024)
    if K <= 1024:
        if N % 512 == 0:
            return tm, wide, K
        return min(M, 512), N // 3, K
    if K % 2048 == 0:
        return tm, min(N, 1024) if mode == 'nt' else wide, 2048
    return tm, min(N, 1024), K // 3


def matmul(a, b, mode, *, name, out_dtype=F32, res=None, act=None, gate=None):
    M, K = a.shape if mode != 'tn' else a.shape[::-1]
    tm, tn, tk = _mm_tiles(mode, M, b.shape[0] if mode == 'nt' else b.shape[1], K)
    N = b.shape[0] if mode == 'nt' else b.shape[1]
    assert M % tm == 0 and N % tn == 0 and K % tk == 0, (a.shape, b.shape, mode, tm, tn, tk)
    nk = K // tk
    size = lambda t: t.size * t.dtype.itemsize
    rows_outer = size(a) + (M // tm) * size(b) <= size(b) + (N // tn) * size(a)

    def spec(shape, index):
        if rows_outer:
            return pl.BlockSpec(shape, lambda i, j, k: index(i, j, k))
        return pl.BlockSpec(shape, lambda j, i, k: index(i, j, k))

    if mode == 'nn':
        a_spec, b_spec = spec((tm, tk), lambda i, j, k: (i, k)), spec((tk, tn), lambda i, j, k: (k, j))
        ca, cb = 1, 0
    elif mode == 'nt':
        a_spec, b_spec = spec((tm, tk), lambda i, j, k: (i, k)), spec((tn, tk), lambda i, j, k: (j, k))
        ca, cb = 1, 1
    else:
        a_spec, b_spec = spec((tk, tm), lambda i, j, k: (k, i)), spec((tk, tn), lambda i, j, k: (k, j))
        ca, cb = 0, 0
    o_spec = spec((tm, tn), lambda i, j, k: (i, j))
    extra = [e for e in (res, gate) if e is not None]

    def body(*refs):
        a_ref, b_ref = refs[:2]
        e_refs = refs[2:2 + len(extra)]
        o_refs = refs[2 + len(extra):2 + len(extra) + (2 if act else 1)]
        part = _dot(a_ref[...], b_ref[...], ca, cb)

        def finish(acc):
            if res is not None:
                acc = acc + e_refs[0][...]
            if gate is not None:
                acc = acc * (2.0 * jnp.maximum(e_refs[-1][...], 0.0))
            o_refs[0][...] = acc.astype(o_refs[0].dtype)
            if act:
                r = jnp.maximum(acc, 0.0)
                o_refs[1][...] = (r * r).astype(o_refs[1].dtype)

        if nk == 1:
            finish(part)
        else:
            acc_ref = refs[-1]
            k = pl.program_id(2)

            @pl.when(k == 0)
            def _():
                acc_ref[...] = part

            @pl.when(k > 0)
            def _():
                acc_ref[...] += part

            @pl.when(k == nk - 1)
            def _():
                finish(acc_ref[...])

    out_shape = [jax.ShapeDtypeStruct((M, N), out_dtype)]
    if act:
        out_shape.append(jax.ShapeDtypeStruct((M, N), BF16))
    res_ = pl.pallas_call(
        body, grid=(M // tm, N // tn, nk) if rows_outer else (N // tn, M // tm, nk), name=name,
        in_specs=[a_spec, b_spec] + [o_spec] * len(extra),
        out_specs=[o_spec] * len(out_shape), out_shape=out_shape,
        scratch_shapes=[pltpu.VMEM((tm, tn), F32)] if nk > 1 else [],
        compiler_params=_cparams("parallel", "parallel", "arbitrary"),
    )(a, b, *extra)
    return res_ if act else res_[0]


def _conv_rows(S):
    return min(S, 512)


def conv_fwd(x, cb0, w, name):
    S = x.shape[0]
    K = w.shape[0]
    R = _conv_rows(S)

    def body(x_ref, w_ref, y_ref, pad_ref):
        pad_ref[pl.ds(0, CONV_PAD), :] = jnp.zeros((CONV_PAD, 128), F32)
        pad_ref[pl.ds(CONV_PAD, S), :] = x_ref[...]
        wv = w_ref[...]

        def chunk(r, carry):
            r0 = pl.multiple_of(r * R, R)
            win = pad_ref[pl.ds(r0, R + CONV_PAD), :]
            acc = jnp.zeros((R, 128), F32)
            for s in range(K):
                sh = win if s == 0 else pltpu.roll(win, s, 0)
                acc = acc + sh[CONV_PAD:, :] * wv[K - 1 - s:K - s, :]
            y_ref[pl.ds(r0, R), :] = acc
            return carry

        lax.fori_loop(0, S // R, chunk, 0)

    return pl.pallas_call(
        body, grid=(4,), name=name,
        in_specs=[pl.BlockSpec((S, 128), lambda j: (0, cb0 * 4 + j)), pl.BlockSpec((K, 128), lambda j: (0, j))],
        out_specs=pl.BlockSpec((S, 128), lambda j: (0, j)),
        out_shape=jax.ShapeDtypeStruct((S, MIX_W), F32),
        scratch_shapes=[pltpu.VMEM((S + CONV_PAD, 128), F32)],
        compiler_params=_cparams("parallel"),
    )(x, w)


def conv_bwd(x, cb0, w, dy, name, add=None, out_dtype=F32):
    S = x.shape[0]
    K = w.shape[0]
    R = _conv_rows(S)
    W = R + CONV_PAD

    def body(*refs):
        x_ref, w_ref, dy_ref = refs[:3]
        add_ref = refs[3] if add is not None else None
        dx_ref, dw_ref, xpad_ref, dypad_ref = refs[-4:]
        xpad_ref[pl.ds(0, CONV_PAD), :] = jnp.zeros((CONV_PAD, 128), F32)
        xpad_ref[pl.ds(CONV_PAD, S), :] = x_ref[...]
        dypad_ref[pl.ds(S, CONV_PAD), :] = jnp.zeros((CONV_PAD, 128), F32)
        dypad_ref[pl.ds(0, S), :] = dy_ref[...].astype(F32)
        dw_ref[...] = jnp.zeros((K, 128), F32)
        wv = w_ref[...]

        def chunk(r, carry):
            r0 = pl.multiple_of(r * R, R)
            xwin = xpad_ref[pl.ds(r0, W), :]
            dwin = dypad_ref[pl.ds(r0, W), :]
            dyc = dwin[:R, :]
            acc = jnp.zeros((R, 128), F32)
            for s in range(K):
                up = dwin if s == 0 else pltpu.roll(dwin, W - s, 0)
                acc = acc + up[:R, :] * wv[K - 1 - s:K - s, :]
                xs = xwin if s == 0 else pltpu.roll(xwin, s, 0)
                dw_ref[pl.ds(K - 1 - s, 1), :] += jnp.sum(dyc * xs[CONV_PAD:, :], axis=0, keepdims=True)
            if add_ref is not None:
                acc = acc + add_ref[pl.ds(r0, R), :].astype(F32)
            dx_ref[pl.ds(r0, R), :] = acc.astype(dx_ref.dtype)
            return carry

        lax.fori_loop(0, S // R, chunk, 0)

    col = pl.BlockSpec((S, 128), lambda j: (0, j))
    ins = [x, w, dy] + ([add] if add is not None else [])
    return pl.pallas_call(
        body, grid=(4,), name=name,
        in_specs=[pl.BlockSpec((S, 128), lambda j: (0, cb0 * 4 + j)), pl.BlockSpec((K, 128), lambda j: (0, j)), col]
        + ([col] if add is not None else []),
        out_specs=[col, pl.BlockSpec((K, 128), lambda j: (0, j))],
        out_shape=[jax.ShapeDtypeStruct((S, MIX_W), out_dtype), jax.ShapeDtypeStruct((K, MIX_W), F32)],
        scratch_shapes=[pltpu.VMEM((S + CONV_PAD, 128), F32), pltpu.VMEM((S + CONV_PAD, 128), F32)],
        compiler_params=_cparams("parallel"),
    )(*ins)


DN_PREP_ROWS = 2 * DN_CHUNK


def _unit_lower_inverse(a):
    C = a.shape[0]
    eye = (lax.broadcasted_iota(jnp.int32, (C, C), 0) == lax.broadcasted_iota(jnp.int32, (C, C), 1)).astype(F32)
    inv = eye - a
    pw = mmh(a, a)
    for k in range(5):
        both = mmh(jnp.concatenate([inv, pw], axis=0), pw)
        inv = inv + both[:C]
        pw = both[C:]
    return inv


@jax.custom_vjp
def _known_inverse(a, inv):
    return inv


_known_inverse.defvjp(lambda a, inv: (inv, inv),
                      lambda inv, g: (-mmh_nt(mmh_tn(inv, g), inv), jnp.zeros_like(inv)))


def _delta_prep(qc, kc, vc, ge, be, inv_known=None):
    C, Dh = DN_CHUNK, DN_HEAD_DIM
    ii = lax.broadcasted_iota(jnp.int32, (C, C), 0)
    jj = lax.broadcasted_iota(jnp.int32, (C, C), 1)
    causal, strict = ii >= jj, ii > jj
    sum_lhs = jnp.concatenate([causal.astype(F32), jnp.ones((C, C), F32)], axis=0)
    us, ws, qds, kds, scs, attns, invs = [], [], [], [], [], [], []
    for h in range(DN_HEADS):
        sl = slice(h * Dh, (h + 1) * Dh)
        q, k, v = _silu(qc[:, sl]), _silu(kc[:, sl]), _silu(vc[:, sl])
        q = q * lax.rsqrt(jnp.sum(q * q, axis=-1, keepdims=True) + 1e-6) * (Dh ** -0.5)
        k = k * lax.rsqrt(jnp.sum(k * k, axis=-1, keepdims=True) + 1e-6)
        g, beta = ge[:, sl], be[:, sl]
        sums = mmh(sum_lhs, g)
        gam, g_last = sums[:C], sums[C:]
        gam_col = gam[:, :C]
        gam_row = mmh_nt(jnp.full((C, Dh), 1.0 / Dh, F32), gam)
        decay = jnp.where(causal, jnp.exp(jnp.where(causal, gam_col - gam_row, 0.0)), 0.0)
        kb = k * beta
        scores = mm_nt(jnp.concatenate([kb, q], axis=0), k)
        a = jnp.where(strict, scores[:C] * decay, 0.0)
        inv = _unit_lower_inverse(a) if inv_known is None else _known_inverse(a, inv_known[h])
        e_gam = jnp.exp(gam)
        uw = mmh(inv, jnp.concatenate([v * beta, kb * e_gam], axis=1))
        us.append(uw[:, :Dh])
        ws.append(uw[:, Dh:])
        qds.append(q * e_gam)
        kds.append(k * jnp.exp(g_last - gam))
        scs.append(jnp.exp(g_last))
        attns.append(scores[C:] * decay)
        invs.append(inv)
    cat = lambda parts: jnp.concatenate(parts, axis=1)
    return cat(us), cat(ws), cat(qds), cat(kds), cat(scs), attns, invs


def _delta_step(states, u, w, qd, kd, sc, attns, gate, ng):
    Dh = DN_HEAD_DIM
    new_states, outs = [], []
    for h in range(DN_HEADS):
        sl = slice(h * Dh, (h + 1) * Dh)
        s_in = states[h]
        on_state = mm(jnp.concatenate([w[:, sl], qd[:, sl]], axis=0), s_in)
        v_new = u[:, sl] - on_state[:DN_CHUNK]
        o = on_state[DN_CHUNK:] + mm(attns[h], v_new)
        new_states.append(s_in * jnp.concatenate([sc[:, sl], sc[:, sl]], axis=0) + mm_tn(kd[:, sl], v_new))
        outs.append(_rms(o, ng) * _silu(gate[:, sl]))
    return new_states, jnp.concatenate(outs, axis=1)


def _prep_rows(fn, blocks, inv_blocks=None):
    n = blocks[0].shape[0] // DN_CHUNK
    res = []
    for r in range(n):
        rows = slice(r * DN_CHUNK, (r + 1) * DN_CHUNK)
        known = None if inv_blocks is None else [iv[rows] for iv in inv_blocks]
        res.append(fn(*[b[rows] for b in blocks], known))
    if n == 1:
        return res[0]
    return jax.tree.map(lambda *parts: jnp.concatenate(parts, axis=0), *res)


def _maybe_carrying(body, in_specs, out_specs, out_shape, operands, ride, steps, name):
    if ride is None:
        res = pl.pallas_call(body, grid=(steps,), name=name, in_specs=in_specs, out_specs=out_specs,
                             out_shape=out_shape, compiler_params=_cparams("parallel"))(*operands)
        return res, None
    res = pl.pallas_call(
        carry(body, len(in_specs), len(out_specs), ride, steps), grid=(steps,), name=name,
        in_specs=in_specs + [ANY] * len(ride.inputs), out_specs=out_specs + [ANY] * len(ride.out_shape),
        out_shape=out_shape + ride.out_shape, scratch_shapes=ride.scratch, compiler_params=_cparams("arbitrary"),
    )(*operands, *ride.inputs)
    return res[:len(out_specs)], res[len(out_specs):]


def delta_prep_fwd(qc, kc, vc, ge, be, name, ride=None):
    S = qc.shape[0]
    R = min(S, DN_PREP_ROWS)

    def body(q_ref, k_ref, v_ref, ge_ref, be_ref, u_ref, w_ref, qd_ref, kd_ref, sc_ref, at_ref, iv_ref):
        u, w, qd, kd, sc, attns, invs = _prep_rows(
            _delta_prep, [q_ref[...], k_ref[...], v_ref[...], ge_ref[...], be_ref[...]])
        for ref, val in zip((u_ref, w_ref, qd_ref, kd_ref, sc_ref), (u, w, qd, kd, sc)):
            ref[...] = val
        for h in range(DN_HEADS):
            at_ref[h] = attns[h]
            iv_ref[h] = invs[h]

    blk = pl.BlockSpec((R, MIX_W), lambda n: (n, 0))
    hblk = pl.BlockSpec((DN_HEADS, R, DN_CHUNK), lambda n: (0, n, 0))
    wide = jax.ShapeDtypeStruct((S, MIX_W), F32)
    narrow = jax.ShapeDtypeStruct((DN_HEADS, S, DN_CHUNK), F32)
    return _maybe_carrying(body, [blk] * 5, [blk] * 5 + [hblk] * 2, [wide] * 5 + [narrow] * 2, [qc, kc, vc, ge, be],
                           ride, S // R, name)


def delta_prep_bwd(qc, kc, vc, ge, be, inv, cts, name, ride=None):
    S = qc.shape[0]
    R = min(S, DN_PREP_ROWS)

    def body(q_ref, k_ref, v_ref, ge_ref, be_ref, iv_ref, du_ref, dw_ref, dqd_ref, dkd_ref, dsc_ref, dat_ref, *outs):
        invs = [iv_ref[h] for h in range(DN_HEADS)]
        fn = lambda *blocks: _prep_rows(_delta_prep, list(blocks), invs)[:6]
        _, vjp = jax.vjp(fn, q_ref[...], k_ref[...], v_ref[...], ge_ref[...], be_ref[...])
        grads = vjp((du_ref[...], dw_ref[...], dqd_ref[...], dkd_ref[...], dsc_ref[...],
                     [dat_ref[h] for h in range(DN_HEADS)]))
        for ref, g in zip(outs, grads):
            ref[...] = g

    blk = pl.BlockSpec((R, MIX_W), lambda n: (n, 0))
    hblk = pl.BlockSpec((DN_HEADS, R, DN_CHUNK), lambda n: (0, n, 0))
    return _maybe_carrying(body, [blk] * 5 + [hblk] + [blk] * 5 + [hblk], [blk] * 5,
                           [jax.ShapeDtypeStruct((S, MIX_W), F32)] * 5, [qc, kc, vc, ge, be, inv, *cts], ride, S // R, name)


def delta_step_fwd(prep, z, ng, name):
    S = prep[0].shape[0]
    N = S // DN_CHUNK
    C = DN_CHUNK

    def body(u_ref, w_ref, qd_ref, kd_ref, sc_ref, at_ref, gate_ref, ng_ref, o_ref, st_ref, s_ref):
        @pl.when(pl.program_id(0) == 0)
        def _():
            s_ref[...] = jnp.zeros(s_ref.shape, F32)

        states = [s_ref[h] for h in range(DN_HEADS)]
        for h in range(DN_HEADS):
            st_ref[0, h] = states[h]
        new_states, o = _delta_step(states, u_ref[...], w_ref[...], qd_ref[...], kd_ref[...], sc_ref[...],
                                    [at_ref[h] for h in range(DN_HEADS)], gate_ref[...], ng_ref[...])
        for h in range(DN_HEADS):
            s_ref[h] = new_states[h]
        o_ref[...] = o.astype(o_ref.dtype)

    blk = pl.BlockSpec((C, MIX_W), lambda n: (n, 0))
    return pl.pallas_call(
        body, grid=(N,), name=name,
        in_specs=[blk] * 5 + [pl.BlockSpec((DN_HEADS, C, C), lambda n: (0, n, 0)),
                              pl.BlockSpec((C, MIX_W), lambda n: (n, ZB_DGATE)),
                              pl.BlockSpec((1, DN_HEAD_DIM), lambda n: (0, 0))],
        out_specs=[blk, pl.BlockSpec((1, DN_HEADS, DN_HEAD_DIM, DN_HEAD_DIM), lambda n: (n, 0, 0, 0))],
        out_shape=[jax.ShapeDtypeStruct((S, MIX_W), BF16),
                   jax.ShapeDtypeStruct((N, DN_HEADS, DN_HEAD_DIM, DN_HEAD_DIM), F32)],
        scratch_shapes=[pltpu.VMEM((DN_HEADS, DN_HEAD_DIM, DN_HEAD_DIM), F32)],
        compiler_params=_cparams("arbitrary"),
    )(*prep, z, ng)


def delta_step_bwd(prep, z, ng, states, do, name):
    S = prep[0].shape[0]
    N = S // DN_CHUNK
    C = DN_CHUNK

    def body(u_ref, w_ref, qd_ref, kd_ref, sc_ref, at_ref, gate_ref, ng_ref, st_ref, do_ref,
             du_ref, dw_ref, dqd_ref, dkd_ref, dsc_ref, dat_ref, dgate_ref, dng_ref, ds_ref):
        first = pl.program_id(0) == 0

        @pl.when(first)
        def _():
            ds_ref[...] = jnp.zeros(ds_ref.shape, F32)

        args = ([st_ref[0, h] for h in range(DN_HEADS)], u_ref[...], w_ref[...], qd_ref[...], kd_ref[...],
                sc_ref[...], [at_ref[h] for h in range(DN_HEADS)], gate_ref[...].astype(F32), ng_ref[...])
        _, vjp = jax.vjp(_delta_step, *args)
        d_states, du, dw, dqd, dkd, dsc, dat, dgate, dng = vjp(([ds_ref[h] for h in range(DN_HEADS)],
                                                               do_ref[...].astype(F32)))
        for h in range(DN_HEADS):
            ds_ref[h] = d_states[h]
            dat_ref[h] = dat[h]
        for ref, g in zip((du_ref, dw_ref, dqd_ref, dkd_ref, dsc_ref), (du, dw, dqd, dkd, dsc)):
            ref[...] = g
        dgate_ref[...] = dgate.astype(dgate_ref.dtype)

        @pl.when(first)
        def _():
            dng_ref[...] = dng

        @pl.when(jnp.logical_not(first))
        def _():
            dng_ref[...] += dng

    blk = pl.BlockSpec((C, MIX_W), lambda n: (N - 1 - n, 0))
    hblk = pl.BlockSpec((DN_HEADS, C, C), lambda n: (0, N - 1 - n, 0))
    ngs = pl.BlockSpec((1, DN_HEAD_DIM), lambda n: (0, 0))
    f32o = jax.ShapeDtypeStruct((S, MIX_W), F32)
    return pl.pallas_call(
        body, grid=(N,), name=name,
        in_specs=[blk] * 5 + [hblk, pl.BlockSpec((C, MIX_W), lambda n: (N - 1 - n, ZB_DGATE)), ngs,
                              pl.BlockSpec((1, DN_HEADS, DN_HEAD_DIM, DN_HEAD_DIM), lambda n: (N - 1 - n, 0, 0, 0)),
                              blk],
        out_specs=[blk] * 5 + [hblk, blk, ngs],
        out_shape=[f32o] * 5 + [jax.ShapeDtypeStruct((DN_HEADS, S, C), F32), jax.ShapeDtypeStruct((S, MIX_W), BF16),
                                jax.ShapeDtypeStruct((1, DN_HEAD_DIM), F32)],
        scratch_shapes=[pltpu.VMEM((DN_HEADS, DN_HEAD_DIM, DN_HEAD_DIM), F32)],
        compiler_params=_cparams("arbitrary"),
    )(*prep, z, ng, states, do)


ANY = pl.BlockSpec(memory_space=pl.ANY)


def _place():
    return lax.axis_index("x"), lax.axis_index("y"), lax.axis_index("c")


PHASES = ('start', 'forward', 'finish')


class Ride:
    def __init__(self, inputs, out_shape, scratch, run):
        self.inputs, self.out_shape, self.scratch, self.run = list(inputs), list(out_shape), list(scratch), run


def run_alone(ride, name):
    n_in, n_out = len(ride.inputs), len(ride.out_shape)

    def body(*refs):
        for phase in PHASES:
            ride.run(phase, refs[:n_in], refs[n_in:n_in + n_out], refs[n_in + n_out:])

    return pl.pallas_call(body, name=name, in_specs=[ANY] * n_in, out_specs=[ANY] * n_out, out_shape=ride.out_shape,
                          scratch_shapes=ride.scratch)(*ride.inputs)


def carry(body, n_in, n_out, ride, steps):
    r_in, r_out = len(ride.inputs), len(ride.out_shape)
    late = (7 * steps) // 8

    def carrying(*refs):
        b = n_in + r_in
        ride_refs = (refs[n_in:b], refs[b + n_out:b + n_out + r_out], refs[b + n_out + r_out:])
        step = pl.program_id(0)

        @pl.when(step == 0)
        def _():
            ride.run('start', *ride_refs)

        @pl.when(step == late)
        def _():
            ride.run('forward', *ride_refs)

        body(*refs[:n_in], *refs[b:b + n_out])

        @pl.when(step == steps - 1)
        def _():
            ride.run('finish', *ride_refs)

    return carrying


def gather_ride(shards):
    n = len(shards)

    def run(phase, ins, outs, sems):
        send_sems, recv_sems = sems
        x, y, c = _place()
        me, sibling = (x, y, c), (x, y, 1 - c)
        chips = [(1 - x, y), (x, 1 - y), (1 - x, 1 - y)]

        def copy(a, k, block, to, src=None):
            row = 4 * block[0] + 2 * block[1] + block[2]
            return pltpu.make_async_remote_copy(
                src_ref=outs[a].at[row] if src is None else src, dst_ref=outs[a].at[row],
                send_sem=send_sems.at[a, k], recv_sem=recv_sems.at[a, k], device_id=to, device_id_type=MESH_ID)

        def first():
            return [cp for a in range(n) for cp in
                    [copy(a, 0, me, sibling, src=ins[a])]
                    + [copy(a, 1 + j, me, (*chip, c), src=ins[a]) for j, chip in enumerate(chips)]]

        def passed():
            return [copy(a, 4 + j, (*chip, c), sibling) for j, chip in enumerate(chips) for a in range(n)]

        if phase == 'start':
            for cp in first():
                cp.start()
        elif phase == 'forward':
            for cp, (j, chip, a) in zip(passed(), [(j, chip, a) for j, chip in enumerate(chips) for a in range(n)]):
                copy(a, 1 + j, (*chip, c), me).wait_recv()
                cp.start()
        else:
            for a in range(n):
                copy(a, 0, sibling, me).wait_recv()
                for j, chip in enumerate(chips):
                    copy(a, 4 + j, (*chip, 1 - c), me).wait_recv()
            for cp in first() + passed():
                cp.wait_send()

    return Ride(shards, [jax.ShapeDtypeStruct((8,) + s.shape, s.dtype) for s in shards],
                [pltpu.SemaphoreType.DMA((n, 7)), pltpu.SemaphoreType.DMA((n, 7))], run)


def all_gather(shards, name):
    return run_alone(gather_ride(shards), name)


def exchange_cores(grads, name):
    n = len(grads)

    def body(*refs):
        ins, outs = refs[:n], refs[n:2 * n]
        send_sems, recv_sems = refs[2 * n:]
        x, y, c = _place()
        copies = [pltpu.make_async_remote_copy(
            src_ref=ins[a].at[2 * k + 1 - c], dst_ref=outs[a].at[k], send_sem=send_sems.at[a, k],
            recv_sem=recv_sems.at[a, k], device_id=(x, y, 1 - c), device_id_type=MESH_ID)
            for a in range(n) for k in range(4)]
        for cp in copies:
            cp.start()
        for cp in copies:
            cp.wait()

    return pl.pallas_call(
        body, name=name, in_specs=[ANY] * n, out_specs=[ANY] * n,
        out_shape=[jax.ShapeDtypeStruct((4,) + g.shape[1:], g.dtype) for g in grads],
        scratch_shapes=[pltpu.SemaphoreType.DMA((n, 4))] * 2,
    )(*grads)


def chips_ride(parts):
    n = len(parts)

    def run(phase, ins, outs, sems):
        send_sems, recv_sems = sems
        x, y, c = _place()
        chips = [(1 - x, y), (x, 1 - y), (1 - x, 1 - y)]
        if phase == 'forward':
            return
        copies = [pltpu.make_async_remote_copy(
            src_ref=ins[a].at[2 * px + py], dst_ref=outs[a].at[j], send_sem=send_sems.at[a, j],
            recv_sem=recv_sems.at[a, j], device_id=(px, py, c), device_id_type=MESH_ID)
            for a in range(n) for j, (px, py) in enumerate(chips)]
        for cp in copies:
            cp.start() if phase == 'start' else cp.wait()

    return Ride(parts, [jax.ShapeDtypeStruct((3,) + p.shape[1:], p.dtype) for p in parts],
                [pltpu.SemaphoreType.DMA((n, 3))] * 2, run)


def exchange_chips(parts, name):
    return run_alone(chips_ride(parts), name)


def _as2d(a, lead=0):
    return a.reshape(a.shape[:lead] + (-1, a.shape[-1]))


def _row_tile(rows, cols, n_arrays):
    budget = VMEM_LIMIT_V7X // 2
    lanes = -(-cols // 128) * 128
    t = budget // (2 * n_arrays * lanes * 4)
    if t >= rows:
        return rows
    return max(16, t // 16 * 16)


def _index_operand(i):
    return jnp.asarray(i, jnp.int32).reshape(1)


def add_own_rows(grads, core, recv, name):
    _, R, C = grads.shape
    T = _row_tile(R, C, 3)

    def body(c_ref, g_ref, r_ref, o_ref):
        o_ref[...] = (g_ref[...].astype(F32) + r_ref[...].astype(F32)).astype(o_ref.dtype)

    blk = pl.BlockSpec((1, T, C), lambda k, i, c: (k, i, 0))
    return pl.pallas_call(
        body, name=name, out_shape=jax.ShapeDtypeStruct((4, R, C), grads.dtype),
        grid_spec=pltpu.PrefetchScalarGridSpec(
            num_scalar_prefetch=1, grid=(4, pl.cdiv(R, T)),
            in_specs=[pl.BlockSpec((1, T, C), lambda k, i, c: (2 * k + c[0], i, 0)), blk], out_specs=blk),
        compiler_params=_cparams("parallel", "parallel"),
    )(_index_operand(core), grads, recv)


def _adamw(w, g, m, v):
    m = ADAM_B1 * m + (1.0 - ADAM_B1) * g
    v = ADAM_B2 * v + (1.0 - ADAM_B2) * jnp.square(g)
    m_hat = m / (1.0 - ADAM_B1 ** ADAM_STEP)
    v_hat = v / (1.0 - ADAM_B2 ** ADAM_STEP)
    delta = -ADAM_LR * (m_hat / (jnp.sqrt(v_hat) + ADAM_EPS) + ADAM_WD * w)
    return delta, m, v


def adamw_sum(parts, w, m, v, name, own=None, own_row=None):
    P, R, C = parts.shape
    T = _row_tile(R, C, P + 8)
    has_own = own is not None

    def body(i_ref, *refs):
        refs = list(refs)
        own_ref = refs.pop(0) if has_own else None
        p_ref, w_ref, m_ref, v_ref, g_ref, d_ref, nm_ref, nv_ref = refs
        terms = ([own_ref[0]] if has_own else []) + [p_ref[k] for k in range(P)]
        terms = [t.astype(F32) for t in terms]
        g = terms[0]
        for t in terms[1:]:
            g = g + t
        d, nm, nv = _adamw(w_ref[...], g, m_ref[...], v_ref[...])
        g_ref[...] = g
        d_ref[...] = d
        nm_ref[...] = nm
        nv_ref[...] = nv

    blk = pl.BlockSpec((T, C), lambda i, r: (i, 0))
    in_specs = [pl.BlockSpec((P, T, C), lambda i, r: (0, i, 0)), blk, blk, blk]
    operands = [parts, w, m, v]
    if has_own:
        in_specs.insert(0, pl.BlockSpec((1, T, C), lambda i, r: (r[0], i, 0)))
        operands.insert(0, own)
    return pl.pallas_call(
        body, name=name, out_shape=[jax.ShapeDtypeStruct((R, C), F32)] * 4,
        grid_spec=pltpu.PrefetchScalarGridSpec(num_scalar_prefetch=1, grid=(pl.cdiv(R, T),), in_specs=in_specs,
                                               out_specs=[blk] * 4),
        compiler_params=_cparams("parallel"),
    )(_index_operand(0 if own_row is None else own_row), *operands)


def f_norm_res(x, g):
    return _rms(x, g), x


def _pool_taps():
    taps = np.zeros((16, MIX_W), np.float32)
    for gi, win in enumerate(POOL_WINDOWS):
        taps[16 - win:, gi * 128:(gi + 1) * 128] = 1.0
    return jnp.asarray(taps)


def loss_head(x, tgt, nf, T, name):
    S, D = x.shape

    def body(x_ref, t_ref, g_ref, l_ref, dx_ref, dg_ref):
        val, vjp = jax.vjp(f_loss, x_ref[...], t_ref[...], g_ref[...])
        dx, _, dg = vjp((jnp.ones((1, 1), F32),))
        dx_ref[...] = dx
        first = pl.program_id(0) == 0
        lv = jnp.broadcast_to(val[0], (1, 128))

        @pl.when(first)
        def _():
            l_ref[...] = lv
            dg_ref[...] = dg

        @pl.when(jnp.logical_not(first))
        def _():
            l_ref[...] += lv
            dg_ref[...] += dg

    row = pl.BlockSpec((T, D), lambda i: (i, 0))
    return pl.pallas_call(
        body, grid=(S // T,), name=name,
        in_specs=[row, row, pl.BlockSpec((1, D), lambda i: (0, 0))],
        out_specs=[pl.BlockSpec((1, 128), lambda i: (0, 0)), row, pl.BlockSpec((1, D), lambda i: (0, 0))],
        out_shape=[jax.ShapeDtypeStruct((1, 128), F32), jax.ShapeDtypeStruct((S, D), F32),
                   jax.ShapeDtypeStruct((1, D), F32)],
        compiler_params=_cparams("arbitrary"),
    )(x, tgt, nf)


def layer_fwd(x, mem, W, tag, ride=None):
    S, D = x.shape
    T = min(S, 256)
    sv = {'x': x}
    (h1,) = rows_fwd(f_norm, [(x, D, 0)], [W['norm_mix']], 'p', [(D, BF16)], T, tag + 'norm_mix')
    z = matmul(h1, W['w_in'], 'nn', name=tag + 'w_in')
    (a_pre,) = rows_fwd(f_glu, [(z, 512, ZB_A1), (z, 512, ZB_A2)], [], '', [(512, F32)], T, tag + 'glu')
    a_cv = conv_fwd(a_pre, 0, W['conv_a_w'], tag + 'conv_a')
    (a,) = rows_fwd(f_lnsilu, [(a_cv, 512, 0)], [W['conv_a_b'], W['ln_a_g'], W['ln_a_b']], 'ppp', [(512, BF16)], T,
                    tag + 'ln_a')
    qc = conv_fwd(z, ZB_Q, W['dn_wq'], tag + 'conv_q')
    kc = conv_fwd(z, ZB_K, W['dn_wk'], tag + 'conv_k')
    vc = conv_fwd(z, ZB_V, W['dn_wv'], tag + 'conv_v')
    ge, be = rows_fwd(f_gbeta, [(z, 128, ZB128_BD)], [W['alp'], W['dtp']], 'pp', [(512, F32), (512, F32)], T,
                      tag + 'gbeta')
    (*prep, dn_inv), rode = delta_prep_fwd(qc, kc, vc, ge, be, tag + 'delta_prep', ride)
    o, states = delta_step_fwd(prep, z, W['dn_norm_g'], tag + 'delta')
    gm_params = [W['gm_ln_g'], W['gm_ln_b'], W['gm_ws']] + W['gm_b']
    (c,) = rows_fwd(f_gmlp, [(z, 512, ZB_GU), (z, 512, ZB_GV)], gm_params, 'p' * 7, [(512, BF16)], T, tag + 'gmlp')
    cs = conv_fwd(z, ZB_POOL, _pool_taps(), tag + 'pool_sum')
    (p,) = rows_fwd(f_pool, [(cs, 512, 0), (z, 512, ZB_POOL)], [W['pool_w'], W['pool_scale']], 'pp', [(512, BF16)], T,
                    tag + 'pool')
    (merged,) = rows_fwd(f_merge, [(a, 512, 0), (o, 512, 0), (c, 512, 0), (p, 512, 0), (z, 4 * D, ZB_GATE)],
                         [W['w_branch']], 'w', [(D, BF16)], min(S, 128), tag + 'merge')
    x1 = matmul(merged, W['w_out'], 'nn', res=x, name=tag + 'w_out')
    (kv,) = rows_fwd(f_kv, [(mem, D, 0)], [W['norm_mem'], W['xa_wkv']], 'pw', [(2 * D, F32)], mem.shape[0],
                     tag + 'kv')
    (x2,) = rows_fwd(f_xattn, [(x1, D, 0)], [kv, W['norm_xa'], W['xa_wq'], W['xa_wo']], 'ppww', [(D, F32)], T,
                     tag + 'xattn')
    (h3,) = rows_fwd(f_norm, [(x2, D, 0)], [W['norm_mlp']], 'p', [(D, BF16)], T, tag + 'norm_mlp')
    pre, r = matmul(h3, W['mlp_w1'], 'nn', act='relu2', name=tag + 'mlp_w1')
    x3 = matmul(r, W['mlp_w2'], 'nn', res=x2, name=tag + 'mlp_w2')
    sv.update(h1=h1, z=z, a_pre=a_pre, a_cv=a_cv, a=a, qc=qc, kc=kc, vc=vc, ge=ge, be=be, prep=prep, dn_inv=dn_inv, o=o,
              states=states, c=c,
              cs=cs, p=p, merged=merged, x1=x1, kv=kv, x2=x2, h3=h3, pre=pre, r=r)
    return x3, sv, rode


def layer_bwd(dx, mem, W, sv, tag, ride=None):
    S, D = dx.shape
    T = min(S, 256)
    tag = tag + 'b_'
    G = {}
    z = sv['z']
    da = matmul(dx, W['mlp_w2'], 'nt', gate=sv['pre'], out_dtype=BF16, name=tag + 'mlp_da')
    G['mlp_w2'] = matmul(sv['r'], dx, 'tn', name=tag + 'mlp_gw2')
    dh3 = matmul(da, W['mlp_w1'], 'nt', name=tag + 'mlp_dh')
    G['mlp_w1'] = matmul(sv['h3'], da, 'tn', name=tag + 'mlp_gw1')
    (dx2,), (G['norm_mlp'],) = rows_bwd(f_norm_res, [(sv['x2'], D, 0)], [W['norm_mlp']], 'p', [dh3, dx], [F32], T,
                                        tag + 'norm_mlp')
    (dx1,), (dkv, G['norm_xa'], G['xa_wq'], G['xa_wo']) = rows_bwd(
        f_xattn, [(sv['x1'], D, 0)], [sv['kv'], W['norm_xa'], W['xa_wq'], W['xa_wo']], 'ppww', [dx2], [F32],
        min(S, 128), tag + 'xattn')
    _, (G['norm_mem'], G['xa_wkv']) = rows_bwd(f_kv, [(mem, D, 0)], [W['norm_mem'], W['xa_wkv']], 'pw', [dkv], [None],
                                               mem.shape[0], tag + 'kv')
    dmerged = matmul(dx1, W['w_out'], 'nt', out_dtype=BF16, name=tag + 'dmerged')
    G['w_out'] = matmul(sv['merged'], dx1, 'tn', name=tag + 'gw_out')
    (d_a, d_o, d_c, d_p, dz_gate), (G['w_branch'],) = rows_bwd(
        f_merge, [(sv['a'], 512, 0), (sv['o'], 512, 0), (sv['c'], 512, 0), (sv['p'], 512, 0), (z, 4 * D, ZB_GATE)],
        [W['w_branch']], 'w', [dmerged], [F32, F32, F32, F32, BF16], min(S, 128), tag + 'merge')
    (dcs, dpx), (G['pool_w'], G['pool_scale']) = rows_bwd(
        f_pool, [(sv['cs'], 512, 0), (z, 512, ZB_POOL)], [W['pool_w'], W['pool_scale']], 'pp', [d_p], [F32, F32], T,
        tag + 'pool')
    dz_pool, _ = conv_bwd(z, ZB_POOL, _pool_taps(), dcs, tag + 'pool_sum', add=dpx, out_dtype=BF16)
    gm_params = [W['gm_ln_g'], W['gm_ln_b'], W['gm_ws']] + W['gm_b']
    (dz_gu, dz_gv), gm_g = rows_bwd(f_gmlp, [(z, 512, ZB_GU), (z, 512, ZB_GV)], gm_params, 'p' * 7, [d_c],
                                    [BF16, BF16], T, tag + 'gmlp')
    G['gm_ln_g'], G['gm_ln_b'], G['gm_ws'] = gm_g[:3]
    G['gm_b'] = list(gm_g[3:])
    *d_prep, dz_dgate, G['dn_norm_g'] = delta_step_bwd(sv['prep'], z, W['dn_norm_g'], sv['states'], d_o, tag + 'delta')
    (dqc, dkc, dvc, dge, dbe), rode = delta_prep_bwd(sv['qc'], sv['kc'], sv['vc'], sv['ge'], sv['be'], sv['dn_inv'],
                                                     d_prep, tag + 'delta_prep', ride)
    (dz_bd,), (G['alp'], G['dtp']) = rows_bwd(f_gbeta, [(z, 128, ZB128_BD)], [W['alp'], W['dtp']], 'pp', [dge, dbe],
                                              [BF16], T, tag + 'gbeta')
    dz_q, G['dn_wq'] = conv_bwd(z, ZB_Q, W['dn_wq'], dqc, tag + 'conv_q', out_dtype=BF16)
    dz_k, G['dn_wk'] = conv_bwd(z, ZB_K, W['dn_wk'], dkc, tag + 'conv_k', out_dtype=BF16)
    dz_v, G['dn_wv'] = conv_bwd(z, ZB_V, W['dn_wv'], dvc, tag + 'conv_v', out_dtype=BF16)
    (da_cv,), (G['conv_a_b'], G['ln_a_g'], G['ln_a_b']) = rows_bwd(
        f_lnsilu, [(sv['a_cv'], 512, 0)], [W['conv_a_b'], W['ln_a_g'], W['ln_a_b']], 'ppp', [d_a], [F32], T,
        tag + 'ln_a')
    da_pre, G['conv_a_w'] = conv_bwd(sv['a_pre'], 0, W['conv_a_w'], da_cv, tag + 'conv_a')
    (dz_a1, dz_a2), _ = rows_bwd(f_glu, [(z, 512, ZB_A1), (z, 512, ZB_A2)], [], '', [da_pre], [BF16, BF16], T,
                                 tag + 'glu')
    dz = jnp.concatenate([dz_gate, dz_a1, dz_a2, dz_q, dz_k, dz_v, dz_dgate, dz_gu, dz_gv, dz_pool, dz_bd], axis=1)
    dh1 = matmul(dz, W['w_in'], 'nt', name=tag + 'dh1')
    G['w_in'] = matmul(sv['h1'], dz, 'tn', name=tag + 'gw_in')
    (dx0,), (G['norm_mix'],) = rows_bwd(f_norm_res, [(sv['x'], D, 0)], [W['norm_mix']], 'p', [dh1, dx1], [F32], T,
                                        tag + 'norm_mix')
    return dx0, G, rode


def local_step(x, mem, tgt, norm_f, n_layers, weights_of, fwd_ride=None, bwd_ride=None):
    saved, weights, carried = [], [], None
    for l in range(n_layers):
        weights.append(weights_of(l, carried))
        x, sv, carried = layer_fwd(x, mem, weights[l], f'l{l}_', fwd_ride(l) if fwd_ride else None)
        saved.append(sv)
    loss, dx, g_nf = loss_head(x, tgt, norm_f, min(x.shape[0], 256), 'loss_head')
    grads, rode, ride = [None] * n_layers, {}, None
    for l in reversed(range(n_layers)):
        dx, grads[l], res = layer_bwd(dx, mem, weights[l], saved[l], f'l{l}_', ride)
        if ride is not None:
            rode[l + 1] = res
        ride = bwd_ride(l, grads[l]) if bwd_ride else None
    return loss, dx, grads, g_nf, rode, ride


def _row(v):
    return v.reshape(1, -1)


def _lane_pad(v):
    return jnp.zeros((8, 128), F32).at[0, :v.shape[0]].set(v)


def layer_weights(full, l):
    w_in = full['w_in'][l]
    cols = [w_in[:, a:b] for a, b in Z_ORDER]
    cols.append(jnp.zeros((w_in.shape[0], Z_W - sum(b - a for a, b in Z_ORDER)), w_in.dtype))
    dn_w = full['dn_conv_w'][l]
    W = {n: _row(full[n][l]) for n in ('norm_mix', 'conv_a_b', 'ln_a_g', 'ln_a_b', 'dn_norm_g', 'gm_ln_g', 'gm_ln_b',
                                       'pool_scale', 'norm_xa', 'norm_mem', 'norm_mlp')}
    W.update(w_in=jnp.concatenate(cols, axis=1), conv_a_w=full['conv_a_w'][l],
             dn_wq=dn_w[:, :MIX_W], dn_wk=dn_w[:, MIX_W:2 * MIX_W], dn_wv=dn_w[:, 2 * MIX_W:],
             alp=_lane_pad(full['dn_a_log'][l]), dtp=_lane_pad(full['dn_dt_bias'][l]),
             gm_ws=full['gm_ws'][l], gm_b=[full['gm_bs'][l][g].reshape(GM_CHUNK, 1) for g in range(GM_GROUPS)],
             pool_w=full['pool_w'][l])
    for n in ('w_branch', 'w_out', 'xa_wq', 'xa_wkv', 'xa_wo', 'mlp_w1', 'mlp_w2'):
        W[n] = full[n][l]
    return W


def layer_grads(G):
    g_in = G['w_in']
    starts = np.cumsum([0] + [b - a for a, b in Z_ORDER])
    pieces = sorted(zip(Z_ORDER, starts[:-1]))
    out = {n: G[n].reshape(-1) for n in ('norm_mix', 'conv_a_b', 'ln_a_g', 'ln_a_b', 'dn_norm_g', 'gm_ln_g', 'gm_ln_b',
                                          'pool_scale', 'norm_xa', 'norm_mem', 'norm_mlp')}
    out.update(w_in=jnp.concatenate([g_in[:, s:s + b - a] for (a, b), s in pieces], axis=1),
               conv_a_w=G['conv_a_w'], dn_conv_w=jnp.concatenate([G['dn_wq'], G['dn_wk'], G['dn_wv']], axis=1),
               dn_a_log=G['alp'][0, :DN_HEADS], dn_dt_bias=G['dtp'][0, :DN_HEADS], gm_ws=G['gm_ws'],
               gm_bs=jnp.stack([b.reshape(-1) for b in G['gm_b']]), pool_w=G['pool_w'])
    for n in ('w_branch', 'w_out', 'xa_wq', 'xa_wkv', 'xa_wo', 'mlp_w1', 'mlp_w2'):
        out[n] = G[n]
    return out


COLUMN_SHARDED = ('w_in', 'conv_a_w', 'dn_conv_w', 'w_branch', 'xa_wkv', 'mlp_w1')


def _unshard(name, g):
    if name in COLUMN_SHARDED:
        t = jnp.moveaxis(g, 0, -2)
        return t.reshape(t.shape[:-2] + (t.shape[-2] * t.shape[-1],))
    t = jnp.moveaxis(g, 0, 1)
    return t.reshape((t.shape[0], t.shape[1] * t.shape[2]) + t.shape[3:])


def _shard_rows(name, g):
    if name in COLUMN_SHARDED:
        t = g.reshape(g.shape[:-1] + (8, g.shape[-1] // 8))
        return jnp.moveaxis(t, -2, 0)
    t = g.reshape((g.shape[0], 8, g.shape[1] // 8) + g.shape[2:])
    return jnp.moveaxis(t, 1, 0)


def _packed_rows(a):
    return -(-a.size // 1024) * 8


def _pack(arrays):
    tiles = [jnp.pad(a.reshape(-1), (0, _packed_rows(a) * 128 - a.size)).reshape(-1, 128) for a in arrays]
    return jnp.concatenate(tiles, axis=0)


def _unpack(packed, like):
    out, at = [], 0
    for a in like:
        rows = _packed_rows(a)
        out.append(packed[at:at + rows].reshape(-1)[:a.size].reshape(a.shape))
        at += rows
    return out


def kernel(x, mem, norm_mix, w_in, conv_a_w, conv_a_b, ln_a_g, ln_a_b, dn_conv_w, dn_a_log, dn_dt_bias, dn_norm_g, gm_ln_g, gm_ln_b, gm_ws, gm_bs, pool_w, pool_scale, w_branch, w_out, norm_xa, norm_mem, xa_wq, xa_wkv, xa_wo, norm_mlp, mlp_w1, mlp_w2, norm_f, loss_target, m_norm_mix, m_w_in, m_conv_a_w, m_conv_a_b, m_ln_a_g, m_ln_a_b, m_dn_conv_w, m_dn_a_log, m_dn_dt_bias, m_dn_norm_g, m_gm_ln_g, m_gm_ln_b, m_gm_ws, m_gm_bs, m_pool_w, m_pool_scale, m_w_branch, m_w_out, m_norm_xa, m_norm_mem, m_xa_wq, m_xa_wkv, m_xa_wo, m_norm_mlp, m_mlp_w1, m_mlp_w2, m_norm_f, v_norm_mix, v_w_in, v_conv_a_w, v_conv_a_b, v_ln_a_g, v_ln_a_b, v_dn_conv_w, v_dn_a_log, v_dn_dt_bias, v_dn_norm_g, v_gm_ln_g, v_gm_ln_b, v_gm_ws, v_gm_bs, v_pool_w, v_pool_scale, v_w_branch, v_w_out, v_norm_xa, v_norm_mem, v_xa_wq, v_xa_wkv, v_xa_wo, v_norm_mlp, v_mlp_w1, v_mlp_w2, v_norm_f):
    args = locals()
    w = {n: args[n] for n in WEIGHTS}
    m = {n: args['m_' + n] for n in WEIGHTS}
    v = {n: args['v_' + n] for n in WEIGHTS}

    px, py, pc = _place()
    me = 4 * px + 2 * py + pc

    def shards_of(l):
        return [w[n][l].astype(BF16) if n in SENT_AS_BF16 else w[n][l] for n in SHARDED]

    def weights_of(l, gathered):
        if gathered is None:
            gathered = all_gather(shards_of(l), 'gather_weights')
        full = {n: w[n][l:l + 1] for n in REPLICATED if n != 'norm_f'}
        for n, g, own in zip(SHARDED, gathered, shards_of(l)):
            full[n] = _unshard(n, lax.dynamic_update_index_in_dim(g, own, me, 0)[:, None])
        return layer_weights(full, 0)

    def fwd_ride(l):
        return gather_ride(shards_of(l + 1)) if l + 1 < DEPTH else None

    per_layer, sums = [None] * DEPTH, [None] * DEPTH

    def bwd_ride(l, G):
        per_layer[l] = layer_grads(G)
        rows = [_as2d(_shard_rows(n, per_layer[l][n][None]), 1).astype(BF16 if n in SENT_AS_BF16 else F32)
                for n in SHARDED]
        from_sibling = exchange_cores(rows, f'reduce_cores_l{l}')
        sums[l] = [add_own_rows(g, pc, r, f'reduce_add_l{l}_{n}') for n, g, r in zip(SHARDED, rows, from_sibling)]
        return chips_ride(sums[l])

    loss, grad_x, grads, g_nf, from_chips, last = local_step(x[0], mem[0], loss_target[0], _row(norm_f), DEPTH,
                                                             weights_of, fwd_ride, bwd_ride)
    from_chips[0] = run_alone(last, 'reduce_chips_l0')
    gfull = {n: jnp.stack([g[n] for g in per_layer]) for n in REPLICATED if n != 'norm_f'}
    gfull['norm_f'] = g_nf.reshape(-1)

    out = {}
    for k, n in enumerate(SHARDED):
        own = jnp.concatenate([lax.dynamic_index_in_dim(sums[l][k], 2 * px + py, 0) for l in range(DEPTH)], axis=1)
        parts = jnp.concatenate([from_chips[l][k] for l in range(DEPTH)], axis=1)
        res = adamw_sum(parts, _as2d(w[n]), _as2d(m[n]), _as2d(v[n]), 'adamw_' + n, own=own, own_row=0)
        out[n] = [r.reshape(w[n].shape) for r in res]
    packed = _pack([gfull[n] for n in REPLICATED])
    (partials,) = all_gather([packed], 'gather_small_grads')
    partials = lax.dynamic_update_index_in_dim(partials, packed, 4 * px + 2 * py + pc, 0)
    res = adamw_sum(partials, _pack([w[n] for n in REPLICATED]), _pack([m[n] for n in REPLICATED]),
                    _pack([v[n] for n in REPLICATED]), 'adamw_small')
    like = [w[n] for n in REPLICATED]
    for k, r in enumerate(res):
        for n, a in zip(REPLICATED, _unpack(r, like)):
            out.setdefault(n, [None] * 4)[k] = a

    total = lax.psum(loss[0, 0], ('x', 'y', 'c'))
    return (total, grad_x[None], *[out[n][0] for n in WEIGHTS], *[out[n][1] for n in WEIGHTS],
            *[out[n][2] for n in WEIGHTS], *[out[n][3] for n in WEIGHTS])
```

```python
import functools
import math

import numpy as np
import jax
import jax.numpy as jnp
from jax import lax
from jax.experimental import pallas as pl
from jax.experimental.pallas import tpu as pltpu

F32 = jnp.float32
BF16 = jnp.bfloat16
HIGH = lax.Precision.HIGH
MESH_ID = pl.DeviceIdType.MESH
VMEM_LIMIT_V7X = 56 << 20

DEPTH = 4
MIX_W = 512
DN_HEADS = 4
DN_HEAD_DIM = 128
DN_CHUNK = 64
GM_CHUNK = 128
GM_GROUPS = 4
POOL_WINDOWS = (2, 4, 8, 16)
XA_HEADS = 4
CONV_PAD = 32

ADAM_LR = 0.001
ADAM_B1 = 0.9
ADAM_B2 = 0.999
ADAM_EPS = 1e-08
ADAM_WD = 0.01
ADAM_STEP = 10

WEIGHTS = ['norm_mix', 'w_in', 'conv_a_w', 'conv_a_b', 'ln_a_g', 'ln_a_b', 'dn_conv_w', 'dn_a_log', 'dn_dt_bias',
           'dn_norm_g', 'gm_ln_g', 'gm_ln_b', 'gm_ws', 'gm_bs', 'pool_w', 'pool_scale', 'w_branch', 'w_out',
           'norm_xa', 'norm_mem', 'xa_wq', 'xa_wkv', 'xa_wo', 'norm_mlp', 'mlp_w1', 'mlp_w2', 'norm_f']
SHARDED = ['w_in', 'conv_a_w', 'dn_conv_w', 'w_branch', 'w_out', 'xa_wq', 'xa_wkv', 'xa_wo', 'mlp_w1', 'mlp_w2']
SENT_AS_BF16 = ['w_in', 'w_branch', 'w_out', 'xa_wq', 'xa_wkv', 'xa_wo', 'mlp_w1', 'mlp_w2']
REPLICATED = [n for n in WEIGHTS if n not in SHARDED]

Z_W = 8832
Z_ORDER = ((4616, 8712), (0, 3072), (3080, 4616), (3072, 3080))
ZB_GATE, ZB_A1, ZB_A2, ZB_Q, ZB_K, ZB_V, ZB_DGATE, ZB_GU, ZB_GV, ZB_POOL = 0, 8, 9, 10, 11, 12, 13, 14, 15, 16
ZB128_BD = 68


def _cparams(*sem):
    return pltpu.CompilerParams(dimension_semantics=sem, vmem_limit_bytes=VMEM_LIMIT_V7X)


def _dot(a, b, ca, cb):
    return lax.dot_general(a.astype(BF16), b.astype(BF16), (((ca,), (cb,)), ((), ())), preferred_element_type=F32)


def _doth(a, b, ca, cb):
    return lax.dot_general(a, b, (((ca,), (cb,)), ((), ())), precision=HIGH, preferred_element_type=F32)


def _make_mm(dot):
    @jax.custom_vjp
    def nn(a, b):
        return dot(a, b, 1, 0)
    nn.defvjp(lambda a, b: (dot(a, b, 1, 0), (a, b)), lambda r, g: (dot(g, r[1], 1, 1), dot(r[0], g, 0, 0)))

    @jax.custom_vjp
    def nt(a, b):
        return dot(a, b, 1, 1)
    nt.defvjp(lambda a, b: (dot(a, b, 1, 1), (a, b)), lambda r, g: (dot(g, r[1], 1, 0), dot(g, r[0], 0, 0)))

    @jax.custom_vjp
    def tn(a, b):
        return dot(a, b, 0, 0)
    tn.defvjp(lambda a, b: (dot(a, b, 0, 0), (a, b)), lambda r, g: (dot(r[1], g, 1, 1), dot(r[0], g, 1, 0)))
    return nn, nt, tn


mm, mm_nt, mm_tn = _make_mm(_dot)
mmh, mmh_nt, mmh_tn = _make_mm(_doth)


@jax.custom_vjp
def _mmw(a, w, wz):
    return _dot(a, w, 1, 0)


_mmw.defvjp(lambda a, w, wz: (_dot(a, w, 1, 0), (a, w)),
            lambda r, g: (_dot(g, r[1], 1, 1), jnp.zeros_like(r[1]), _dot(r[0], g, 0, 0)))


def mmw(a, wpair):
    w, wz = wpair
    return _dot(a, w, 1, 0) if wz is None else _mmw(a, w, wz)


def wsel(wpair, n):
    return (wpair[0][n], None if wpair[1] is None else wpair[1][n])


def _sigmoid(x):
    return 1.0 / (1.0 + jnp.exp(-x))


def _silu(x):
    return x * _sigmoid(x)


def _rms(x, g, eps=1e-6):
    return x * lax.rsqrt(jnp.mean(x * x, axis=-1, keepdims=True) + eps) * g


def _ln(x, g, b, eps=1e-5):
    mu = jnp.mean(x, axis=-1, keepdims=True)
    d = x - mu
    return d * lax.rsqrt(jnp.mean(d * d, axis=-1, keepdims=True) + eps) * g + b


def _gelu(x):
    return 0.5 * x * (1.0 + lax.erf(x * (2.0 ** -0.5)))


def _softplus(x):
    return jnp.maximum(x, 0.0) + jnp.log(1.0 + jnp.exp(-jnp.abs(x)))


def _row_spec(T, width, cb):
    return pl.BlockSpec((T, width), lambda i: (i, cb))


def _whole_spec(p):
    nd = p.ndim
    return pl.BlockSpec(p.shape, lambda i: (0,) * nd)


def _load_params(refs, kinds, with_zeros):
    out = []
    for r, k in zip(refs, kinds):
        if k == 'w':
            out.append((r[...], jnp.zeros(r.shape, F32) if with_zeros else None))
        else:
            out.append(r[...].astype(F32))
    return out


def rows_fwd(f, rows, params, kinds, outs, T, name):
    S = rows[0][0].shape[0]
    nr, npar = len(rows), len(params)

    def body(*refs):
        r = [x[...].astype(F32) for x in refs[:nr]]
        p = _load_params(refs[nr:nr + npar], kinds, False)
        res = f(*r, *p)
        for o_ref, o in zip(refs[nr + npar:], res):
            o_ref[...] = o.astype(o_ref.dtype)

    return pl.pallas_call(
        body, grid=(S // T,), name=name,
        in_specs=[_row_spec(T, w, cb) for _, w, cb in rows] + [_whole_spec(p) for p in params],
        out_specs=[_row_spec(T, w, 0) for w, _ in outs],
        out_shape=[jax.ShapeDtypeStruct((S, w), dt) for w, dt in outs],
        compiler_params=_cparams("parallel"),
    )(*[a for a, _, _ in rows], *params)


def rows_bwd(f, rows, params, kinds, cts, row_dtypes, T, name):
    S = rows[0][0].shape[0]
    nr, npar, nc = len(rows), len(params), len(cts)
    want = [i for i, dt in enumerate(row_dtypes) if dt is not None]

    def body(*refs):
        r = [x[...].astype(F32) for x in refs[:nr]]
        p = _load_params(refs[nr:nr + npar], kinds, True)
        g = [x[...].astype(F32) for x in refs[nr + npar:nr + npar + nc]]
        d_rows = refs[nr + npar + nc:nr + npar + nc + len(want)]
        d_params = refs[nr + npar + nc + len(want):]
        _, vjp = jax.vjp(f, *r, *p)
        grads = vjp(tuple(g))
        for o_ref, i in zip(d_rows, want):
            o_ref[...] = grads[i].astype(o_ref.dtype)
        first = pl.program_id(0) == 0
        for o_ref, gp, k in zip(d_params, grads[nr:], kinds):
            gp = gp[1] if k == 'w' else gp

            @pl.when(first)
            def _():
                o_ref[...] = gp

            @pl.when(jnp.logical_not(first))
            def _():
                o_ref[...] += gp

    res = pl.pallas_call(
        body, grid=(S // T,), name=name,
        in_specs=([_row_spec(T, w, cb) for _, w, cb in rows] + [_whole_spec(p) for p in params]
                  + [_row_spec(T, c.shape[1], 0) for c in cts]),
        out_specs=[_row_spec(T, rows[i][1], 0) for i in want] + [_whole_spec(p) for p in params],
        out_shape=([jax.ShapeDtypeStruct((S, rows[i][1]), row_dtypes[i]) for i in want]
                   + [jax.ShapeDtypeStruct(p.shape, F32) for p in params]),
        compiler_params=_cparams("arbitrary"),
    )(*[a for a, _, _ in rows], *params, *cts)
    return res[:len(want)], res[len(want):]


def f_norm(x, g):
    return (_rms(x, g),)


def f_glu(a1, a2):
    return (a1 * _sigmoid(a2),)


def f_lnsilu(cv, cb, g, b):
    return (_silu(_ln(cv + cb, g, b)),)


def f_gbeta(bd, alp, dtp):
    j = lax.broadcasted_iota(jnp.int32, (128, MIX_W), 0)
    head = lax.broadcasted_iota(jnp.int32, (128, MIX_W), 1) // DN_HEAD_DIM
    e_lo = (j == head).astype(F32)
    e_hi = (j == head + DN_HEADS).astype(F32)
    beta = _sigmoid(mmh(bd, e_lo))
    a_log = jnp.sum(mmh(alp, e_lo), axis=0, keepdims=True)
    dt_bias = jnp.sum(mmh(dtp, e_lo), axis=0, keepdims=True)
    g = -jnp.exp(a_log) * _softplus(mmh(bd, e_hi) + dt_bias)
    return g, beta


def f_gmlp(u_in, v_in, lg, lb, ws, b0, b1, b2, b3):
    T = u_in.shape[0]
    u = _gelu(u_in)
    vg = _ln(_gelu(v_in), lg, lb)
    tril = (lax.broadcasted_iota(jnp.int32, (GM_CHUNK, GM_CHUNK), 0)
            >= lax.broadcasted_iota(jnp.int32, (GM_CHUNK, GM_CHUNK), 1))
    bias = (b0, b1, b2, b3)
    n = T // GM_CHUNK
    w = [jnp.where(tril, ws[gi], 0.0) for gi in range(GM_GROUPS)]
    mixed = [[mm(w[gi], vg[r * GM_CHUNK:(r + 1) * GM_CHUNK, gi * 128:(gi + 1) * 128]) for gi in range(GM_GROUPS)]
             for r in range(n)]
    chunks = [jnp.concatenate([mixed[r][gi] + bias[gi] for gi in range(GM_GROUPS)], axis=1) for r in range(n)]
    return (u * (chunks[0] if n == 1 else jnp.concatenate(chunks, axis=0)),)


def f_pool(cs, xin, pw, scale):
    T = cs.shape[0]
    t = pl.program_id(0) * T + lax.broadcasted_iota(jnp.int32, (T, 128), 0)
    pooled = [cs[:, gi * 128:(gi + 1) * 128] / jnp.minimum(t + 1, win).astype(F32) - xin[:, gi * 128:(gi + 1) * 128]
              for gi, win in enumerate(POOL_WINDOWS)]
    cols = [mm(pooled[gi], pw[gi]) for gi in range(len(POOL_WINDOWS))]
    return (jnp.concatenate(cols, axis=1) * scale,)


def f_merge(a, o, c, p, gate, wb):
    D = gate.shape[1] // 4
    proj = [mmw(br, wsel(wb, n)) for n, br in enumerate((a, o, c, p))]
    terms = [_sigmoid(gate[:, n * D:(n + 1) * D]) * proj[n] for n in range(4)]
    return (terms[0] + terms[1] + terms[2] + terms[3],)


def f_kv(mem, nm, wkv):
    return (mmw(_rms(mem, nm), wkv),)


def f_xattn(x, kv, nx, wq, wo):
    D = x.shape[1]
    hd = D // XA_HEADS
    q = mmw(_rms(x, nx), wq)
    hs = range(XA_HEADS)
    s = [mm_nt(q[:, h * hd:(h + 1) * hd], kv[:, h * hd:(h + 1) * hd]) * (hd ** -0.5) for h in hs]
    e = [jnp.exp(t - jnp.max(t, axis=-1, keepdims=True)) for t in s]
    pr = [t / jnp.sum(t, axis=-1, keepdims=True) for t in e]
    heads = [mm(pr[h], kv[:, D + h * hd:D + (h + 1) * hd]) for h in hs]
    return (x + mmw(jnp.concatenate(heads, axis=1), wo),)


def f_loss(x, tgt, nf):
    err = _rms(x, nf) - tgt
    return (0.5 * jnp.sum(jnp.mean(err * err, axis=-1, keepdims=True), axis=0, keepdims=True),)


def _mm_tiles(mode, M, N, K):
    wide = 512 if N % 512 == 0 else 384
    if mode == 'tn':
        return min(M, 512), wide, K
    tm = min(M, 1024)
    if K <= 1024:
        if N % 512 == 0:
            return tm, wide, K
        return min(M, 512), N // 3, K
    if K % 2048 == 0:
        return tm, min(N, 1024) if mode == 'nt' else wide, 2048
    return tm, min(N, 1024), K // 3


def matmul(a, b, mode, *, name, out_dtype=F32, res=None, act=None, gate=None):
    M, K = a.shape if mode != 'tn' else a.shape[::-1]
    tm, tn, tk = _mm_tiles(mode, M, b.shape[0] if mode == 'nt' else b.shape[1], K)
    N = b.shape[0] if mode == 'nt' else b.shape[1]
    assert M % tm == 0 and N % tn == 0 and K % tk == 0, (a.shape, b.shape, mode, tm, tn, tk)
    nk = K // tk
    size = lambda t: t.size * t.dtype.itemsize
    rows_outer = size(a) + (M // tm) * size(b) <= size(b) + (N // tn) * size(a)

    def spec(shape, index):
        if rows_outer:
            return pl.BlockSpec(shape, lambda i, j, k: index(i, j, k))
        return pl.BlockSpec(shape, lambda j, i, k: index(i, j, k))

    if mode == 'nn':
        a_spec, b_spec = spec((tm, tk), lambda i, j, k: (i, k)), spec((tk, tn), lambda i, j, k: (k, j))
        ca, cb = 1, 0
    elif mode == 'nt':
        a_spec, b_spec = spec((tm, tk), lambda i, j, k: (i, k)), spec((tn, tk), lambda i, j, k: (j, k))
        ca, cb = 1, 1
    else:
        a_spec, b_spec = spec((tk, tm), lambda i, j, k: (k, i)), spec((tk, tn), lambda i, j, k: (k, j))
        ca, cb = 0, 0
    o_spec = spec((tm, tn), lambda i, j, k: (i, j))
    extra = [e for e in (res, gate) if e is not None]

    def body(*refs):
        a_ref, b_ref = refs[:2]
        e_refs = refs[2:2 + len(extra)]
        o_refs = refs[2 + len(extra):2 + len(extra) + (2 if act else 1)]
        part = _dot(a_ref[...], b_ref[...], ca, cb)

        def finish(acc):
            if res is not None:
                acc = acc + e_refs[0][...]
            if gate is not None:
                acc = acc * (2.0 * jnp.maximum(e_refs[-1][...], 0.0))
            o_refs[0][...] = acc.astype(o_refs[0].dtype)
            if act:
                r = jnp.maximum(acc, 0.0)
                o_refs[1][...] = (r * r).astype(o_refs[1].dtype)

        if nk == 1:
            finish(part)
        else:
            acc_ref = refs[-1]
            k = pl.program_id(2)

            @pl.when(k == 0)
            def _():
                acc_ref[...] = part

            @pl.when(k > 0)
            def _():
                acc_ref[...] += part

            @pl.when(k == nk - 1)
            def _():
                finish(acc_ref[...])

    out_shape = [jax.ShapeDtypeStruct((M, N), out_dtype)]
    if act:
        out_shape.append(jax.ShapeDtypeStruct((M, N), BF16))
    res_ = pl.pallas_call(
        body, grid=(M // tm, N // tn, nk) if rows_outer else (N // tn, M // tm, nk), name=name,
        in_specs=[a_spec, b_spec] + [o_spec] * len(extra),
        out_specs=[o_spec] * len(out_shape), out_shape=out_shape,
        scratch_shapes=[pltpu.VMEM((tm, tn), F32)] if nk > 1 else [],
        compiler_params=_cparams("parallel", "parallel", "arbitrary"),
    )(a, b, *extra)
    return res_ if act else res_[0]


def _conv_rows(S):
    return min(S, 512)


def conv_fwd(x, cb0, w, name):
    S = x.shape[0]
    K = w.shape[0]
    R = _conv_rows(S)

    def body(x_ref, w_ref, y_ref, pad_ref):
        pad_ref[pl.ds(0, CONV_PAD), :] = jnp.zeros((CONV_PAD, 128), F32)
        pad_ref[pl.ds(CONV_PAD, S), :] = x_ref[...]
        wv = w_ref[...]

        def chunk(r, carry):
            r0 = pl.multiple_of(r * R, R)
            win = pad_ref[pl.ds(r0, R + CONV_PAD), :]
            acc = jnp.zeros((R, 128), F32)
            for s in range(K):
                sh = win if s == 0 else pltpu.roll(win, s, 0)
                acc = acc + sh[CONV_PAD:, :] * wv[K - 1 - s:K - s, :]
            y_ref[pl.ds(r0, R), :] = acc
            return carry

        lax.fori_loop(0, S // R, chunk, 0)

    return pl.pallas_call(
        body, grid=(4,), name=name,
        in_specs=[pl.BlockSpec((S, 128), lambda j: (0, cb0 * 4 + j)), pl.BlockSpec((K, 128), lambda j: (0, j))],
        out_specs=pl.BlockSpec((S, 128), lambda j: (0, j)),
        out_shape=jax.ShapeDtypeStruct((S, MIX_W), F32),
        scratch_shapes=[pltpu.VMEM((S + CONV_PAD, 128), F32)],
        compiler_params=_cparams("parallel"),
    )(x, w)


def conv_bwd(x, cb0, w, dy, name, add=None, out_dtype=F32):
    S = x.shape[0]
    K = w.shape[0]
    R = _conv_rows(S)
    W = R + CONV_PAD

    def body(*refs):
        x_ref, w_ref, dy_ref = refs[:3]
        add_ref = refs[3] if add is not None else None
        dx_ref, dw_ref, xpad_ref, dypad_ref = refs[-4:]
        xpad_ref[pl.ds(0, CONV_PAD), :] = jnp.zeros((CONV_PAD, 128), F32)
        xpad_ref[pl.ds(CONV_PAD, S), :] = x_ref[...]
        dypad_ref[pl.ds(S, CONV_PAD), :] = jnp.zeros((CONV_PAD, 128), F32)
        dypad_ref[pl.ds(0, S), :] = dy_ref[...].astype(F32)
        dw_ref[...] = jnp.zeros((K, 128), F32)
        wv = w_ref[...]

        def chunk(r, carry):
            r0 = pl.multiple_of(r * R, R)
            xwin = xpad_ref[pl.ds(r0, W), :]
            dwin = dypad_ref[pl.ds(r0, W), :]
            dyc = dwin[:R, :]
            acc = jnp.zeros((R, 128), F32)
            for s in range(K):
                up = dwin if s == 0 else pltpu.roll(dwin, W - s, 0)
                acc = acc + up[:R, :] * wv[K - 1 - s:K - s, :]
                xs = xwin if s == 0 else pltpu.roll(xwin, s, 0)
                dw_ref[pl.ds(K - 1 - s, 1), :] += jnp.sum(dyc * xs[CONV_PAD:, :], axis=0, keepdims=True)
            if add_ref is not None:
                acc = acc + add_ref[pl.ds(r0, R), :].astype(F32)
            dx_ref[pl.ds(r0, R), :] = acc.astype(dx_ref.dtype)
            return carry

        lax.fori_loop(0, S // R, chunk, 0)

    col = pl.BlockSpec((S, 128), lambda j: (0, j))
    ins = [x, w, dy] + ([add] if add is not None else [])
    return pl.pallas_call(
        body, grid=(4,), name=name,
        in_specs=[pl.BlockSpec((S, 128), lambda j: (0, cb0 * 4 + j)), pl.BlockSpec((K, 128), lambda j: (0, j)), col]
        + ([col] if add is not None else []),
        out_specs=[col, pl.BlockSpec((K, 128), lambda j: (0, j))],
        out_shape=[jax.ShapeDtypeStruct((S, MIX_W), out_dtype), jax.ShapeDtypeStruct((K, MIX_W), F32)],
        scratch_shapes=[pltpu.VMEM((S + CONV_PAD, 128), F32), pltpu.VMEM((S + CONV_PAD, 128), F32)],
        compiler_params=_cparams("parallel"),
    )(*ins)


DN_PREP_ROWS = 2 * DN_CHUNK


def _unit_lower_inverses(mats):
    C = mats[0].shape[0]
    eye = (lax.broadcasted_iota(jnp.int32, (C, C), 0) == lax.broadcasted_iota(jnp.int32, (C, C), 1)).astype(F32)
    invs = [eye - a for a in mats]
    pws = [mmh(a, a) for a in mats]
    for _ in range(5):
        both = [mmh(jnp.concatenate([inv, pw], axis=0), pw) for inv, pw in zip(invs, pws)]
        invs = [inv + b[:C] for inv, b in zip(invs, both)]
        pws = [b[C:] for b in both]
    return invs


@jax.custom_vjp
def _known_inverse(a, inv):
    return inv


_known_inverse.defvjp(lambda a, inv: (inv, inv),
                      lambda inv, g: (-mmh_nt(mmh_tn(inv, g), inv), jnp.zeros_like(inv)))


def _delta_prep(qc, kc, vc, ge, be, inv_known=None):
    C, Dh = DN_CHUNK, DN_HEAD_DIM
    n = qc.shape[0] // C
    ii = lax.broadcasted_iota(jnp.int32, (C, C), 0)
    jj = lax.broadcasted_iota(jnp.int32, (C, C), 1)
    causal, strict = ii >= jj, ii > jj
    sum_lhs = jnp.concatenate([causal.astype(F32), jnp.ones((C, C), F32)], axis=0)
    mean_lanes = jnp.full((C, Dh), 1.0 / Dh, F32)
    pairs = [(r, h) for r in range(n) for h in range(DN_HEADS)]
    pick = lambda t, w: [t[r * C:(r + 1) * C, h * w:(h + 1) * w] for r, h in pairs]
    q, k, v = [[_silu(t) for t in pick(b, Dh)] for b in (qc, kc, vc)]
    q = [t * lax.rsqrt(jnp.sum(t * t, axis=-1, keepdims=True) + 1e-6) * (Dh ** -0.5) for t in q]
    k = [t * lax.rsqrt(jnp.sum(t * t, axis=-1, keepdims=True) + 1e-6) for t in k]
    beta = pick(be, Dh)
    sums = [mmh(sum_lhs, g) for g in pick(ge, Dh)]
    gam, g_last = [s[:C] for s in sums], [s[C:] for s in sums]
    gam_row = [mmh_nt(mean_lanes, t) for t in gam]
    decay = [jnp.where(causal, jnp.exp(jnp.where(causal, gc[:, :C] - gr, 0.0)), 0.0) for gc, gr in zip(gam, gam_row)]
    kb = [a * b for a, b in zip(k, beta)]
    scores = [mm_nt(jnp.concatenate([a, b], axis=0), c) for a, b, c in zip(kb, q, k)]
    a = [jnp.where(strict, s[:C] * d, 0.0) for s, d in zip(scores, decay)]
    if inv_known is None:
        inv = _unit_lower_inverses(a)
    else:
        inv = [_known_inverse(m, inv_known[h][r * C:(r + 1) * C]) for m, (r, h) in zip(a, pairs)]
    e_gam = [jnp.exp(t) for t in gam]
    uw = [mmh(i, jnp.concatenate([vv * b, kk * e], axis=1)) for i, vv, b, kk, e in zip(inv, v, beta, kb, e_gam)]

    def wide(parts):
        rows = [jnp.concatenate(parts[r * DN_HEADS:(r + 1) * DN_HEADS], axis=1) for r in range(n)]
        return rows[0] if n == 1 else jnp.concatenate(rows, axis=0)

    def narrow(parts):
        per_head = [[parts[r * DN_HEADS + h] for r in range(n)] for h in range(DN_HEADS)]
        return [p[0] if n == 1 else jnp.concatenate(p, axis=0) for p in per_head]

    return (wide([t[:, :Dh] for t in uw]), wide([t[:, Dh:] for t in uw]), wide([a_ * b_ for a_, b_ in zip(q, e_gam)]),
            wide([kk * jnp.exp(gl - gm) for kk, gl, gm in zip(k, g_last, gam)]), wide([jnp.exp(t) for t in g_last]),
            narrow([s[C:] * d for s, d in zip(scores, decay)]), narrow(inv))


def _delta_step(states, u, w, qd, kd, sc, attns, gate, ng):
    C, Dh = DN_CHUNK, DN_HEAD_DIM
    heads = range(DN_HEADS)
    cut = lambda t: [t[:, h * Dh:(h + 1) * Dh] for h in heads]
    u, w, qd, kd, sc, gate = cut(u), cut(w), cut(qd), cut(kd), cut(sc), cut(gate)
    on_state = [mm(jnp.concatenate([w[h], qd[h]], axis=0), states[h]) for h in heads]
    v_new = [u[h] - on_state[h][:C] for h in heads]
    o = [on_state[h][C:] + mm(attns[h], v_new[h]) for h in heads]
    new_states = [states[h] * jnp.concatenate([sc[h], sc[h]], axis=0) + mm_tn(kd[h], v_new[h]) for h in heads]
    outs = [_rms(o[h], ng) * _silu(gate[h]) for h in heads]
    return new_states, jnp.concatenate(outs, axis=1)


def _maybe_carrying(body, in_specs, out_specs, out_shape, operands, ride, steps, name):
    if ride is None:
        res = pl.pallas_call(body, grid=(steps,), name=name, in_specs=in_specs, out_specs=out_specs,
                             out_shape=out_shape, compiler_params=_cparams("parallel"))(*operands)
        return res, None
    res = pl.pallas_call(
        carry(body, len(in_specs), len(out_specs), ride, steps), grid=(steps,), name=name,
        in_specs=in_specs + [ANY] * len(ride.inputs), out_specs=out_specs + [ANY] * len(ride.out_shape),
        out_shape=out_shape + ride.out_shape, scratch_shapes=ride.scratch, compiler_params=_cparams("arbitrary"),
    )(*operands, *ride.inputs)
    return res[:len(out_specs)], res[len(out_specs):]


def delta_prep_fwd(qc, kc, vc, ge, be, name, ride=None):
    S = qc.shape[0]
    R = min(S, DN_PREP_ROWS)

    def body(q_ref, k_ref, v_ref, ge_ref, be_ref, u_ref, w_ref, qd_ref, kd_ref, sc_ref, at_ref, iv_ref):
        u, w, qd, kd, sc, attns, invs = _delta_prep(q_ref[...], k_ref[...], v_ref[...], ge_ref[...], be_ref[...])
        for ref, val in zip((u_ref, w_ref, qd_ref, kd_ref, sc_ref), (u, w, qd, kd, sc)):
            ref[...] = val
        for h in range(DN_HEADS):
            at_ref[h] = attns[h]
            iv_ref[h] = invs[h]

    blk = pl.BlockSpec((R, MIX_W), lambda n: (n, 0))
    hblk = pl.BlockSpec((DN_HEADS, R, DN_CHUNK), lambda n: (0, n, 0))
    wide = jax.ShapeDtypeStruct((S, MIX_W), F32)
    narrow = jax.ShapeDtypeStruct((DN_HEADS, S, DN_CHUNK), F32)
    return _maybe_carrying(body, [blk] * 5, [blk] * 5 + [hblk] * 2, [wide] * 5 + [narrow] * 2, [qc, kc, vc, ge, be],
                           ride, S // R, name)


def delta_prep_bwd(qc, kc, vc, ge, be, inv, cts, name, ride=None):
    S = qc.shape[0]
    R = min(S, DN_PREP_ROWS)

    def body(q_ref, k_ref, v_ref, ge_ref, be_ref, iv_ref, du_ref, dw_ref, dqd_ref, dkd_ref, dsc_ref, dat_ref, *outs):
        invs = [iv_ref[h] for h in range(DN_HEADS)]
        fn = lambda *blocks: _delta_prep(*blocks, invs)[:6]
        _, vjp = jax.vjp(fn, q_ref[...], k_ref[...], v_ref[...], ge_ref[...], be_ref[...])
        grads = vjp((du_ref[...], dw_ref[...], dqd_ref[...], dkd_ref[...], dsc_ref[...],
                     [dat_ref[h] for h in range(DN_HEADS)]))
        for ref, g in zip(outs, grads):
            ref[...] = g

    blk = pl.BlockSpec((R, MIX_W), lambda n: (n, 0))
    hblk = pl.BlockSpec((DN_HEADS, R, DN_CHUNK), lambda n: (0, n, 0))
    return _maybe_carrying(body, [blk] * 5 + [hblk] + [blk] * 5 + [hblk], [blk] * 5,
                           [jax.ShapeDtypeStruct((S, MIX_W), F32)] * 5, [qc, kc, vc, ge, be, inv, *cts], ride, S // R, name)


def delta_step_fwd(prep, z, ng, name):
    S = prep[0].shape[0]
    N = S // DN_CHUNK
    C = DN_CHUNK

    def body(u_ref, w_ref, qd_ref, kd_ref, sc_ref, at_ref, gate_ref, ng_ref, o_ref, st_ref, s_ref):
        @pl.when(pl.program_id(0) == 0)
        def _():
            s_ref[...] = jnp.zeros(s_ref.shape, F32)

        states = [s_ref[h] for h in range(DN_HEADS)]
        for h in range(DN_HEADS):
            st_ref[0, h] = states[h]
        new_states, o = _delta_step(states, u_ref[...], w_ref[...], qd_ref[...], kd_ref[...], sc_ref[...],
                                    [at_ref[h] for h in range(DN_HEADS)], gate_ref[...], ng_ref[...])
        for h in range(DN_HEADS):
            s_ref[h] = new_states[h]
        o_ref[...] = o.astype(o_ref.dtype)

    blk = pl.BlockSpec((C, MIX_W), lambda n: (n, 0))
    return pl.pallas_call(
        body, grid=(N,), name=name,
        in_specs=[blk] * 5 + [pl.BlockSpec((DN_HEADS, C, C), lambda n: (0, n, 0)),
                              pl.BlockSpec((C, MIX_W), lambda n: (n, ZB_DGATE)),
                              pl.BlockSpec((1, DN_HEAD_DIM), lambda n: (0, 0))],
        out_specs=[blk, pl.BlockSpec((1, DN_HEADS, DN_HEAD_DIM, DN_HEAD_DIM), lambda n: (n, 0, 0, 0))],
        out_shape=[jax.ShapeDtypeStruct((S, MIX_W), BF16),
                   jax.ShapeDtypeStruct((N, DN_HEADS, DN_HEAD_DIM, DN_HEAD_DIM), F32)],
        scratch_shapes=[pltpu.VMEM((DN_HEADS, DN_HEAD_DIM, DN_HEAD_DIM), F32)],
        compiler_params=_cparams("arbitrary"),
    )(*prep, z, ng)


def delta_step_bwd(prep, z, ng, states, do, name):
    S = prep[0].shape[0]
    N = S // DN_CHUNK
    C = DN_CHUNK

    def body(u_ref, w_ref, qd_ref, kd_ref, sc_ref, at_ref, gate_ref, ng_ref, st_ref, do_ref,
             du_ref, dw_ref, dqd_ref, dkd_ref, dsc_ref, dat_ref, dgate_ref, dng_ref, ds_ref):
        first = pl.program_id(0) == 0

        @pl.when(first)
        def _():
            ds_ref[...] = jnp.zeros(ds_ref.shape, F32)

        args = ([st_ref[0, h] for h in range(DN_HEADS)], u_ref[...], w_ref[...], qd_ref[...], kd_ref[...],
                sc_ref[...], [at_ref[h] for h in range(DN_HEADS)], gate_ref[...].astype(F32), ng_ref[...])
        _, vjp = jax.vjp(_delta_step, *args)
        d_states, du, dw, dqd, dkd, dsc, dat, dgate, dng = vjp(([ds_ref[h] for h in range(DN_HEADS)],
                                                               do_ref[...].astype(F32)))
        for h in range(DN_HEADS):
            ds_ref[h] = d_states[h]
            dat_ref[h] = dat[h]
        for ref, g in zip((du_ref, dw_ref, dqd_ref, dkd_ref, dsc_ref), (du, dw, dqd, dkd, dsc)):
            ref[...] = g
        dgate_ref[...] = dgate.astype(dgate_ref.dtype)

        @pl.when(first)
        def _():
            dng_ref[...] = dng

        @pl.when(jnp.logical_not(first))
        def _():
            dng_ref[...] += dng

    blk = pl.BlockSpec((C, MIX_W), lambda n: (N - 1 - n, 0))
    hblk = pl.BlockSpec((DN_HEADS, C, C), lambda n: (0, N - 1 - n, 0))
    ngs = pl.BlockSpec((1, DN_HEAD_DIM), lambda n: (0, 0))
    f32o = jax.ShapeDtypeStruct((S, MIX_W), F32)
    return pl.pallas_call(
        body, grid=(N,), name=name,
        in_specs=[blk] * 5 + [hblk, pl.BlockSpec((C, MIX_W), lambda n: (N - 1 - n, ZB_DGATE)), ngs,
                              pl.BlockSpec((1, DN_HEADS, DN_HEAD_DIM, DN_HEAD_DIM), lambda n: (N - 1 - n, 0, 0, 0)),
                              blk],
        out_specs=[blk] * 5 + [hblk, blk, ngs],
        out_shape=[f32o] * 5 + [jax.ShapeDtypeStruct((DN_HEADS, S, C), F32), jax.ShapeDtypeStruct((S, MIX_W), BF16),
                                jax.ShapeDtypeStruct((1, DN_HEAD_DIM), F32)],
        scratch_shapes=[pltpu.VMEM((DN_HEADS, DN_HEAD_DIM, DN_HEAD_DIM), F32)],
        compiler_params=_cparams("arbitrary"),
    )(*prep, z, ng, states, do)


ANY = pl.BlockSpec(memory_space=pl.ANY)


def _place():
    return lax.axis_index("x"), lax.axis_index("y"), lax.axis_index("c")


PHASES = ('start', 'forward', 'finish')


class Ride:
    def __init__(self, inputs, out_shape, scratch, run):
        self.inputs, self.out_shape, self.scratch, self.run = list(inputs), list(out_shape), list(scratch), run


def run_alone(ride, name):
    n_in, n_out = len(ride.inputs), len(ride.out_shape)

    def body(*refs):
        for phase in PHASES:
            ride.run(phase, refs[:n_in], refs[n_in:n_in + n_out], refs[n_in + n_out:])

    return pl.pallas_call(body, name=name, in_specs=[ANY] * n_in, out_specs=[ANY] * n_out, out_shape=ride.out_shape,
                          scratch_shapes=ride.scratch)(*ride.inputs)


def carry(body, n_in, n_out, ride, steps):
    r_in, r_out = len(ride.inputs), len(ride.out_shape)
    late = (7 * steps) // 8

    def carrying(*refs):
        b = n_in + r_in
        ride_refs = (refs[n_in:b], refs[b + n_out:b + n_out + r_out], refs[b + n_out + r_out:])
        step = pl.program_id(0)

        @pl.when(step == 0)
        def _():
            ride.run('start', *ride_refs)

        @pl.when(step == late)
        def _():
            ride.run('forward', *ride_refs)

        body(*refs[:n_in], *refs[b:b + n_out])

        @pl.when(step == steps - 1)
        def _():
            ride.run('finish', *ride_refs)

    return carrying


def gather_ride(shards):
    n = len(shards)

    def run(phase, ins, outs, sems):
        send_sems, recv_sems = sems
        x, y, c = _place()
        me, sibling = (x, y, c), (x, y, 1 - c)
        chips = [(1 - x, y), (x, 1 - y), (1 - x, 1 - y)]

        def copy(a, k, block, to, src=None):
            row = 4 * block[0] + 2 * block[1] + block[2]
            return pltpu.make_async_remote_copy(
                src_ref=outs[a].at[row] if src is None else src, dst_ref=outs[a].at[row],
                send_sem=send_sems.at[a, k], recv_sem=recv_sems.at[a, k], device_id=to, device_id_type=MESH_ID)

        def first():
            return [cp for a in range(n) for cp in
                    [copy(a, 0, me, sibling, src=ins[a])]
                    + [copy(a, 1 + j, me, (*chip, c), src=ins[a]) for j, chip in enumerate(chips)]]

        def passed():
            return [copy(a, 4 + j, (*chip, c), sibling) for j, chip in enumerate(chips) for a in range(n)]

        if phase == 'start':
            for cp in first():
                cp.start()
        elif phase == 'forward':
            for cp, (j, chip, a) in zip(passed(), [(j, chip, a) for j, chip in enumerate(chips) for a in range(n)]):
                copy(a, 1 + j, (*chip, c), me).wait_recv()
                cp.start()
        else:
            for a in range(n):
                copy(a, 0, sibling, me).wait_recv()
                for j, chip in enumerate(chips):
                    copy(a, 4 + j, (*chip, 1 - c), me).wait_recv()
            for cp in first() + passed():
                cp.wait_send()

    return Ride(shards, [jax.ShapeDtypeStruct((8,) + s.shape, s.dtype) for s in shards],
                [pltpu.SemaphoreType.DMA((n, 7)), pltpu.SemaphoreType.DMA((n, 7))], run)


def all_gather(shards, name):
    return run_alone(gather_ride(shards), name)


def exchange_cores(grads, name):
    n = len(grads)

    def body(*refs):
        ins, outs = refs[:n], refs[n:2 * n]
        send_sems, recv_sems = refs[2 * n:]
        x, y, c = _place()
        copies = [pltpu.make_async_remote_copy(
            src_ref=ins[a].at[2 * k + 1 - c], dst_ref=outs[a].at[k], send_sem=send_sems.at[a, k],
            recv_sem=recv_sems.at[a, k], device_id=(x, y, 1 - c), device_id_type=MESH_ID)
            for a in range(n) for k in range(4)]
        for cp in copies:
            cp.start()
        for cp in copies:
            cp.wait()

    return pl.pallas_call(
        body, name=name, in_specs=[ANY] * n, out_specs=[ANY] * n,
        out_shape=[jax.ShapeDtypeStruct((4,) + g.shape[1:], g.dtype) for g in grads],
        scratch_shapes=[pltpu.SemaphoreType.DMA((n, 4))] * 2,
    )(*grads)


def chips_ride(parts):
    n = len(parts)

    def run(phase, ins, outs, sems):
        send_sems, recv_sems = sems
        x, y, c = _place()
        chips = [(1 - x, y), (x, 1 - y), (1 - x, 1 - y)]
        if phase == 'forward':
            return
        copies = [pltpu.make_async_remote_copy(
            src_ref=ins[a].at[2 * px + py], dst_ref=outs[a].at[j], send_sem=send_sems.at[a, j],
            recv_sem=recv_sems.at[a, j], device_id=(px, py, c), device_id_type=MESH_ID)
            for a in range(n) for j, (px, py) in enumerate(chips)]
        for cp in copies:
            cp.start() if phase == 'start' else cp.wait()

    return Ride(parts, [jax.ShapeDtypeStruct((3,) + p.shape[1:], p.dtype) for p in parts],
                [pltpu.SemaphoreType.DMA((n, 3))] * 2, run)


def exchange_chips(parts, name):
    return run_alone(chips_ride(parts), name)


def _as2d(a, lead=0):
    return a.reshape(a.shape[:lead] + (-1, a.shape[-1]))


def _row_tile(rows, cols, n_arrays):
    budget = VMEM_LIMIT_V7X // 2
    lanes = -(-cols // 128) * 128
    t = budget // (2 * n_arrays * lanes * 4)
    if t >= rows:
        return rows
    return max(16, t // 16 * 16)


def _index_operand(i):
    return jnp.asarray(i, jnp.int32).reshape(1)


def add_own_rows(grads, core, recv, name):
    _, R, C = grads.shape
    T = _row_tile(R, C, 3)

    def body(c_ref, g_ref, r_ref, o_ref):
        o_ref[...] = (g_ref[...].astype(F32) + r_ref[...].astype(F32)).astype(o_ref.dtype)

    blk = pl.BlockSpec((1, T, C), lambda k, i, c: (k, i, 0))
    return pl.pallas_call(
        body, name=name, out_shape=jax.ShapeDtypeStruct((4, R, C), grads.dtype),
        grid_spec=pltpu.PrefetchScalarGridSpec(
            num_scalar_prefetch=1, grid=(4, pl.cdiv(R, T)),
            in_specs=[pl.BlockSpec((1, T, C), lambda k, i, c: (2 * k + c[0], i, 0)), blk], out_specs=blk),
        compiler_params=_cparams("parallel", "parallel"),
    )(_index_operand(core), grads, recv)


def _adamw(w, g, m, v):
    m = ADAM_B1 * m + (1.0 - ADAM_B1) * g
    v = ADAM_B2 * v + (1.0 - ADAM_B2) * jnp.square(g)
    m_hat = m / (1.0 - ADAM_B1 ** ADAM_STEP)
    v_hat = v / (1.0 - ADAM_B2 ** ADAM_STEP)
    delta = -ADAM_LR * (m_hat / (jnp.sqrt(v_hat) + ADAM_EPS) + ADAM_WD * w)
    return delta, m, v


def adamw_sum(parts, w, m, v, name, own=None, own_row=None):
    P, R, C = parts.shape
    T = _row_tile(R, C, P + 8)
    has_own = own is not None

    def body(i_ref, *refs):
        refs = list(refs)
        own_ref = refs.pop(0) if has_own else None
        p_ref, w_ref, m_ref, v_ref, g_ref, d_ref, nm_ref, nv_ref = refs
        terms = ([own_ref[0]] if has_own else []) + [p_ref[k] for k in range(P)]
        terms = [t.astype(F32) for t in terms]
        g = terms[0]
        for t in terms[1:]:
            g = g + t
        d, nm, nv = _adamw(w_ref[...], g, m_ref[...], v_ref[...])
        g_ref[...] = g
        d_ref[...] = d
        nm_ref[...] = nm
        nv_ref[...] = nv

    blk = pl.BlockSpec((T, C), lambda i, r: (i, 0))
    in_specs = [pl.BlockSpec((P, T, C), lambda i, r: (0, i, 0)), blk, blk, blk]
    operands = [parts, w, m, v]
    if has_own:
        in_specs.insert(0, pl.BlockSpec((1, T, C), lambda i, r: (r[0], i, 0)))
        operands.insert(0, own)
    return pl.pallas_call(
        body, name=name, out_shape=[jax.ShapeDtypeStruct((R, C), F32)] * 4,
        grid_spec=pltpu.PrefetchScalarGridSpec(num_scalar_prefetch=1, grid=(pl.cdiv(R, T),), in_specs=in_specs,
                                               out_specs=[blk] * 4),
        compiler_params=_cparams("parallel"),
    )(_index_operand(0 if own_row is None else own_row), *operands)


def f_norm_res(x, g):
    return _rms(x, g), x


def _pool_taps():
    taps = np.zeros((16, MIX_W), np.float32)
    for gi, win in enumerate(POOL_WINDOWS):
        taps[16 - win:, gi * 128:(gi + 1) * 128] = 1.0
    return jnp.asarray(taps)


def loss_head(x, tgt, nf, T, name):
    S, D = x.shape

    def body(x_ref, t_ref, g_ref, l_ref, dx_ref, dg_ref):
        val, vjp = jax.vjp(f_loss, x_ref[...], t_ref[...], g_ref[...])
        dx, _, dg = vjp((jnp.ones((1, 1), F32),))
        dx_ref[...] = dx
        first = pl.program_id(0) == 0
        lv = jnp.broadcast_to(val[0], (1, 128))

        @pl.when(first)
        def _():
            l_ref[...] = lv
            dg_ref[...] = dg

        @pl.when(jnp.logical_not(first))
        def _():
            l_ref[...] += lv
            dg_ref[...] += dg

    row = pl.BlockSpec((T, D), lambda i: (i, 0))
    return pl.pallas_call(
        body, grid=(S // T,), name=name,
        in_specs=[row, row, pl.BlockSpec((1, D), lambda i: (0, 0))],
        out_specs=[pl.BlockSpec((1, 128), lambda i: (0, 0)), row, pl.BlockSpec((1, D), lambda i: (0, 0))],
        out_shape=[jax.ShapeDtypeStruct((1, 128), F32), jax.ShapeDtypeStruct((S, D), F32),
                   jax.ShapeDtypeStruct((1, D), F32)],
        compiler_params=_cparams("arbitrary"),
    )(x, tgt, nf)


def layer_fwd(x, mem, W, tag, ride=None):
    S, D = x.shape
    T = min(S, 256)
    sv = {'x': x}
    (h1,) = rows_fwd(f_norm, [(x, D, 0)], [W['norm_mix']], 'p', [(D, BF16)], T, tag + 'norm_mix')
    z = matmul(h1, W['w_in'], 'nn', name=tag + 'w_in')
    (a_pre,) = rows_fwd(f_glu, [(z, 512, ZB_A1), (z, 512, ZB_A2)], [], '', [(512, F32)], T, tag + 'glu')
    a_cv = conv_fwd(a_pre, 0, W['conv_a_w'], tag + 'conv_a')
    (a,) = rows_fwd(f_lnsilu, [(a_cv, 512, 0)], [W['conv_a_b'], W['ln_a_g'], W['ln_a_b']], 'ppp', [(512, BF16)], T,
                    tag + 'ln_a')
    qc = conv_fwd(z, ZB_Q, W['dn_wq'], tag + 'conv_q')
    kc = conv_fwd(z, ZB_K, W['dn_wk'], tag + 'conv_k')
    vc = conv_fwd(z, ZB_V, W['dn_wv'], tag + 'conv_v')
    ge, be = rows_fwd(f_gbeta, [(z, 128, ZB128_BD)], [W['alp'], W['dtp']], 'pp', [(512, F32), (512, F32)], T,
                      tag + 'gbeta')
    (*prep, dn_inv), rode = delta_prep_fwd(qc, kc, vc, ge, be, tag + 'delta_prep', ride)
    o, states = delta_step_fwd(prep, z, W['dn_norm_g'], tag + 'delta')
    gm_params = [W['gm_ln_g'], W['gm_ln_b'], W['gm_ws']] + W['gm_b']
    (c,) = rows_fwd(f_gmlp, [(z, 512, ZB_GU), (z, 512, ZB_GV)], gm_params, 'p' * 7, [(512, BF16)], T, tag + 'gmlp')
    cs = conv_fwd(z, ZB_POOL, _pool_taps(), tag + 'pool_sum')
    (p,) = rows_fwd(f_pool, [(cs, 512, 0), (z, 512, ZB_POOL)], [W['pool_w'], W['pool_scale']], 'pp', [(512, BF16)], T,
                    tag + 'pool')
    (merged,) = rows_fwd(f_merge, [(a, 512, 0), (o, 512, 0), (c, 512, 0), (p, 512, 0), (z, 4 * D, ZB_GATE)],
                         [W['w_branch']], 'w', [(D, BF16)], min(S, 128), tag + 'merge')
    x1 = matmul(merged, W['w_out'], 'nn', res=x, name=tag + 'w_out')
    (kv,) = rows_fwd(f_kv, [(mem, D, 0)], [W['norm_mem'], W['xa_wkv']], 'pw', [(2 * D, F32)], mem.shape[0],
                     tag + 'kv')
    (x2,) = rows_fwd(f_xattn, [(x1, D, 0)], [kv, W['norm_xa'], W['xa_wq'], W['xa_wo']], 'ppww', [(D, F32)], T,
                     tag + 'xattn')
    (h3,) = rows_fwd(f_norm, [(x2, D, 0)], [W['norm_mlp']], 'p', [(D, BF16)], T, tag + 'norm_mlp')
    pre, r = matmul(h3, W['mlp_w1'], 'nn', act='relu2', name=tag + 'mlp_w1')
    x3 = matmul(r, W['mlp_w2'], 'nn', res=x2, name=tag + 'mlp_w2')
    sv.update(h1=h1, z=z, a_pre=a_pre, a_cv=a_cv, a=a, qc=qc, kc=kc, vc=vc, ge=ge, be=be, prep=prep, dn_inv=dn_inv, o=o,
              states=states, c=c,
              cs=cs, p=p, merged=merged, x1=x1, kv=kv, x2=x2, h3=h3, pre=pre, r=r)
    return x3, sv, rode


def layer_bwd(dx, mem, W, sv, tag, ride=None):
    S, D = dx.shape
    T = min(S, 256)
    tag = tag + 'b_'
    G = {}
    z = sv['z']
    da = matmul(dx, W['mlp_w2'], 'nt', gate=sv['pre'], out_dtype=BF16, name=tag + 'mlp_da')
    G['mlp_w2'] = matmul(sv['r'], dx, 'tn', name=tag + 'mlp_gw2')
    dh3 = matmul(da, W['mlp_w1'], 'nt', name=tag + 'mlp_dh')
    G['mlp_w1'] = matmul(sv['h3'], da, 'tn', name=tag + 'mlp_gw1')
    (dx2,), (G['norm_mlp'],) = rows_bwd(f_norm_res, [(sv['x2'], D, 0)], [W['norm_mlp']], 'p', [dh3, dx], [F32], T,
                                        tag + 'norm_mlp')
    (dx1,), (dkv, G['norm_xa'], G['xa_wq'], G['xa_wo']) = rows_bwd(
        f_xattn, [(sv['x1'], D, 0)], [sv['kv'], W['norm_xa'], W['xa_wq'], W['xa_wo']], 'ppww', [dx2], [F32],
        min(S, 128), tag + 'xattn')
    _, (G['norm_mem'], G['xa_wkv']) = rows_bwd(f_kv, [(mem, D, 0)], [W['norm_mem'], W['xa_wkv']], 'pw', [dkv], [None],
                                               mem.shape[0], tag + 'kv')
    dmerged = matmul(dx1, W['w_out'], 'nt', out_dtype=BF16, name=tag + 'dmerged')
    G['w_out'] = matmul(sv['merged'], dx1, 'tn', name=tag + 'gw_out')
    (d_a, d_o, d_c, d_p, dz_gate), (G['w_branch'],) = rows_bwd(
        f_merge, [(sv['a'], 512, 0), (sv['o'], 512, 0), (sv['c'], 512, 0), (sv['p'], 512, 0), (z, 4 * D, ZB_GATE)],
        [W['w_branch']], 'w', [dmerged], [F32, F32, F32, F32, BF16], min(S, 128), tag + 'merge')
    (dcs, dpx), (G['pool_w'], G['pool_scale']) = rows_bwd(
        f_pool, [(sv['cs'], 512, 0), (z, 512, ZB_POOL)], [W['pool_w'], W['pool_scale']], 'pp', [d_p], [F32, F32], T,
        tag + 'pool')
    dz_pool, _ = conv_bwd(z, ZB_POOL, _pool_taps(), dcs, tag + 'pool_sum', add=dpx, out_dtype=BF16)
    gm_params = [W['gm_ln_g'], W['gm_ln_b'], W['gm_ws']] + W['gm_b']
    (dz_gu, dz_gv), gm_g = rows_bwd(f_gmlp, [(z, 512, ZB_GU), (z, 512, ZB_GV)], gm_params, 'p' * 7, [d_c],
                                    [BF16, BF16], T, tag + 'gmlp')
    G['gm_ln_g'], G['gm_ln_b'], G['gm_ws'] = gm_g[:3]
    G['gm_b'] = list(gm_g[3:])
    *d_prep, dz_dgate, G['dn_norm_g'] = delta_step_bwd(sv['prep'], z, W['dn_norm_g'], sv['states'], d_o, tag + 'delta')
    (dqc, dkc, dvc, dge, dbe), rode = delta_prep_bwd(sv['qc'], sv['kc'], sv['vc'], sv['ge'], sv['be'], sv['dn_inv'],
                                                     d_prep, tag + 'delta_prep', ride)
    (dz_bd,), (G['alp'], G['dtp']) = rows_bwd(f_gbeta, [(z, 128, ZB128_BD)], [W['alp'], W['dtp']], 'pp', [dge, dbe],
                                              [BF16], T, tag + 'gbeta')
    dz_q, G['dn_wq'] = conv_bwd(z, ZB_Q, W['dn_wq'], dqc, tag + 'conv_q', out_dtype=BF16)
    dz_k, G['dn_wk'] = conv_bwd(z, ZB_K, W['dn_wk'], dkc, tag + 'conv_k', out_dtype=BF16)
    dz_v, G['dn_wv'] = conv_bwd(z, ZB_V, W['dn_wv'], dvc, tag + 'conv_v', out_dtype=BF16)
    (da_cv,), (G['conv_a_b'], G['ln_a_g'], G['ln_a_b']) = rows_bwd(
        f_lnsilu, [(sv['a_cv'], 512, 0)], [W['conv_a_b'], W['ln_a_g'], W['ln_a_b']], 'ppp', [d_a], [F32], T,
        tag + 'ln_a')
    da_pre, G['conv_a_w'] = conv_bwd(sv['a_pre'], 0, W['conv_a_w'], da_cv, tag + 'conv_a')
    (dz_a1, dz_a2), _ = rows_bwd(f_glu, [(z, 512, ZB_A1), (z, 512, ZB_A2)], [], '', [da_pre], [BF16, BF16], T,
                                 tag + 'glu')
    dz = jnp.concatenate([dz_gate, dz_a1, dz_a2, dz_q, dz_k, dz_v, dz_dgate, dz_gu, dz_gv, dz_pool, dz_bd], axis=1)
    dh1 = matmul(dz, W['w_in'], 'nt', name=tag + 'dh1')
    G['w_in'] = matmul(sv['h1'], dz, 'tn', name=tag + 'gw_in')
    (dx0,), (G['norm_mix'],) = rows_bwd(f_norm_res, [(sv['x'], D, 0)], [W['norm_mix']], 'p', [dh1, dx1], [F32], T,
                                        tag + 'norm_mix')
    return dx0, G, rode


def local_step(x, mem, tgt, norm_f, n_layers, weights_of, fwd_ride=None, bwd_ride=None):
    saved, weights, carried = [], [], None
    for l in range(n_layers):
        weights.append(weights_of(l, carried))
        x, sv, carried = layer_fwd(x, mem, weights[l], f'l{l}_', fwd_ride(l) if fwd_ride else None)
        saved.append(sv)
    loss, dx, g_nf = loss_head(x, tgt, norm_f, min(x.shape[0], 256), 'loss_head')
    grads, rode, ride = [None] * n_layers, {}, None
    for l in reversed(range(n_layers)):
        dx, grads[l], res = layer_bwd(dx, mem, weights[l], saved[l], f'l{l}_', ride)
        if ride is not None:
            rode[l + 1] = res
        ride = bwd_ride(l, grads[l]) if bwd_ride else None
    return loss, dx, grads, g_nf, rode, ride


def _row(v):
    return v.reshape(1, -1)


def _lane_pad(v):
    return jnp.zeros((8, 128), F32).at[0, :v.shape[0]].set(v)


def layer_weights(full, l):
    w_in = full['w_in'][l]
    cols = [w_in[:, a:b] for a, b in Z_ORDER]
    cols.append(jnp.zeros((w_in.shape[0], Z_W - sum(b - a for a, b in Z_ORDER)), w_in.dtype))
    dn_w = full['dn_conv_w'][l]
    W = {n: _row(full[n][l]) for n in ('norm_mix', 'conv_a_b', 'ln_a_g', 'ln_a_b', 'dn_norm_g', 'gm_ln_g', 'gm_ln_b',
                                       'pool_scale', 'norm_xa', 'norm_mem', 'norm_mlp')}
    W.update(w_in=jnp.concatenate(cols, axis=1), conv_a_w=full['conv_a_w'][l],
             dn_wq=dn_w[:, :MIX_W], dn_wk=dn_w[:, MIX_W:2 * MIX_W], dn_wv=dn_w[:, 2 * MIX_W:],
             alp=_lane_pad(full['dn_a_log'][l]), dtp=_lane_pad(full['dn_dt_bias'][l]),
             gm_ws=full['gm_ws'][l], gm_b=[full['gm_bs'][l][g].reshape(GM_CHUNK, 1) for g in range(GM_GROUPS)],
             pool_w=full['pool_w'][l])
    for n in ('w_branch', 'w_out', 'xa_wq', 'xa_wkv', 'xa_wo', 'mlp_w1', 'mlp_w2'):
        W[n] = full[n][l]
    return W


def layer_grads(G):
    g_in = G['w_in']
    starts = np.cumsum([0] + [b - a for a, b in Z_ORDER])
    pieces = sorted(zip(Z_ORDER, starts[:-1]))
    out = {n: G[n].reshape(-1) for n in ('norm_mix', 'conv_a_b', 'ln_a_g', 'ln_a_b', 'dn_norm_g', 'gm_ln_g', 'gm_ln_b',
                                          'pool_scale', 'norm_xa', 'norm_mem', 'norm_mlp')}
    out.update(w_in=jnp.concatenate([g_in[:, s:s + b - a] for (a, b), s in pieces], axis=1),
               conv_a_w=G['conv_a_w'], dn_conv_w=jnp.concatenate([G['dn_wq'], G['dn_wk'], G['dn_wv']], axis=1),
               dn_a_log=G['alp'][0, :DN_HEADS], dn_dt_bias=G['dtp'][0, :DN_HEADS], gm_ws=G['gm_ws'],
               gm_bs=jnp.stack([b.reshape(-1) for b in G['gm_b']]), pool_w=G['pool_w'])
    for n in ('w_branch', 'w_out', 'xa_wq', 'xa_wkv', 'xa_wo', 'mlp_w1', 'mlp_w2'):
        out[n] = G[n]
    return out


COLUMN_SHARDED = ('w_in', 'conv_a_w', 'dn_conv_w', 'w_branch', 'xa_wkv', 'mlp_w1')


def _unshard(name, g):
    if name in COLUMN_SHARDED:
        t = jnp.moveaxis(g, 0, -2)
        return t.reshape(t.shape[:-2] + (t.shape[-2] * t.shape[-1],))
    t = jnp.moveaxis(g, 0, 1)
    return t.reshape((t.shape[0], t.shape[1] * t.shape[2]) + t.shape[3:])


def _shard_rows(name, g):
    if name in COLUMN_SHARDED:
        t = g.reshape(g.shape[:-1] + (8, g.shape[-1] // 8))
        return jnp.moveaxis(t, -2, 0)
    t = g.reshape((g.shape[0], 8, g.shape[1] // 8) + g.shape[2:])
    return jnp.moveaxis(t, 1, 0)


def _packed_rows(a):
    return -(-a.size // 1024) * 8


def _pack(arrays):
    tiles = [jnp.pad(a.reshape(-1), (0, _packed_rows(a) * 128 - a.size)).reshape(-1, 128) for a in arrays]
    return jnp.concatenate(tiles, axis=0)


def _unpack(packed, like):
    out, at = [], 0
    for a in like:
        rows = _packed_rows(a)
        out.append(packed[at:at + rows].reshape(-1)[:a.size].reshape(a.shape))
        at += rows
    return out


def kernel(x, mem, norm_mix, w_in, conv_a_w, conv_a_b, ln_a_g, ln_a_b, dn_conv_w, dn_a_log, dn_dt_bias, dn_norm_g, gm_ln_g, gm_ln_b, gm_ws, gm_bs, pool_w, pool_scale, w_branch, w_out, norm_xa, norm_mem, xa_wq, xa_wkv, xa_wo, norm_mlp, mlp_w1, mlp_w2, norm_f, loss_target, m_norm_mix, m_w_in, m_conv_a_w, m_conv_a_b, m_ln_a_g, m_ln_a_b, m_dn_conv_w, m_dn_a_log, m_dn_dt_bias, m_dn_norm_g, m_gm_ln_g, m_gm_ln_b, m_gm_ws, m_gm_bs, m_pool_w, m_pool_scale, m_w_branch, m_w_out, m_norm_xa, m_norm_mem, m_xa_wq, m_xa_wkv, m_xa_wo, m_norm_mlp, m_mlp_w1, m_mlp_w2, m_norm_f, v_norm_mix, v_w_in, v_conv_a_w, v_conv_a_b, v_ln_a_g, v_ln_a_b, v_dn_conv_w, v_dn_a_log, v_dn_dt_bias, v_dn_norm_g, v_gm_ln_g, v_gm_ln_b, v_gm_ws, v_gm_bs, v_pool_w, v_pool_scale, v_w_branch, v_w_out, v_norm_xa, v_norm_mem, v_xa_wq, v_xa_wkv, v_xa_wo, v_norm_mlp, v_mlp_w1, v_mlp_w2, v_norm_f):
    args = locals()
    w = {n: args[n] for n in WEIGHTS}
    m = {n: args['m_' + n] for n in WEIGHTS}
    v = {n: args['v_' + n] for n in WEIGHTS}

    px, py, pc = _place()
    me = 4 * px + 2 * py + pc

    def shards_of(l):
        return [w[n][l].astype(BF16) if n in SENT_AS_BF16 else w[n][l] for n in SHARDED]

    def weights_of(l, gathered):
        if gathered is None:
            gathered = all_gather(shards_of(l), 'gather_weights')
        full = {n: w[n][l:l + 1] for n in REPLICATED if n != 'norm_f'}
        for n, g, own in zip(SHARDED, gathered, shards_of(l)):
            full[n] = _unshard(n, lax.dynamic_update_index_in_dim(g, own, me, 0)[:, None])
        return layer_weights(full, 0)

    def fwd_ride(l):
        return gather_ride(shards_of(l + 1)) if l + 1 < DEPTH else None

    per_layer, sums = [None] * DEPTH, [None] * DEPTH

    def bwd_ride(l, G):
        per_layer[l] = layer_grads(G)
        rows = [_as2d(_shard_rows(n, per_layer[l][n][None]), 1).astype(BF16 if n in SENT_AS_BF16 else F32)
                for n in SHARDED]
        from_sibling = exchange_cores(rows, f'reduce_cores_l{l}')
        sums[l] = [add_own_rows(g, pc, r, f'reduce_add_l{l}_{n}') for n, g, r in zip(SHARDED, rows, from_sibling)]
        return chips_ride(sums[l])

    loss, grad_x, grads, g_nf, from_chips, last = local_step(x[0], mem[0], loss_target[0], _row(norm_f), DEPTH,
                                                             weights_of, fwd_ride, bwd_ride)
    from_chips[0] = run_alone(last, 'reduce_chips_l0')
    gfull = {n: jnp.stack([g[n] for g in per_layer]) for n in REPLICATED if n != 'norm_f'}
    gfull['norm_f'] = g_nf.reshape(-1)

    out = {}
    for k, n in enumerate(SHARDED):
        own = jnp.concatenate([lax.dynamic_index_in_dim(sums[l][k], 2 * px + py, 0) for l in range(DEPTH)], axis=1)
        parts = jnp.concatenate([from_chips[l][k] for l in range(DEPTH)], axis=1)
        res = adamw_sum(parts, _as2d(w[n]), _as2d(m[n]), _as2d(v[n]), 'adamw_' + n, own=own, own_row=0)
        out[n] = [r.reshape(w[n].shape) for r in res]
    packed = _pack([gfull[n] for n in REPLICATED])
    (partials,) = all_gather([packed], 'gather_small_grads')
    partials = lax.dynamic_update_index_in_dim(partials, packed, 4 * px + 2 * py + pc, 0)
    res = adamw_sum(partials, _pack([w[n] for n in REPLICATED]), _pack([m[n] for n in REPLICATED]),
                    _pack([v[n] for n in REPLICATED]), 'adamw_small')
    like = [w[n] for n in REPLICATED]
    for k, r in enumerate(res):
        for n, a in zip(REPLICATED, _unpack(r, like)):
            out.setdefault(n, [None] * 4)[k] = a

    total = lax.psum(loss[0, 0], ('x', 'y', 'c'))
    return (total, grad_x[None], *[out[n][0] for n in WEIGHTS], *[out[n][1] for n in WEIGHTS],
            *[out[n][2] for n in WEIGHTS], *[out[n][3] for n in WEIGHTS])
```

```python
import functools
import math

import numpy as np
import jax
import jax.numpy as jnp
from jax import lax
from jax.experimental import pallas as pl
from jax.experimental.pallas import tpu as pltpu

F32 = jnp.float32
BF16 = jnp.bfloat16
HIGH = lax.Precision.HIGH
MESH_ID = pl.DeviceIdType.MESH
VMEM_LIMIT_V7X = 56 << 20

DEPTH = 4
MIX_W = 512
DN_HEADS = 4
DN_HEAD_DIM = 128
DN_CHUNK = 64
GM_CHUNK = 128
GM_GROUPS = 4
POOL_WINDOWS = (2, 4, 8, 16)
XA_HEADS = 4
CONV_PAD = 32

ADAM_LR = 0.001
ADAM_B1 = 0.9
ADAM_B2 = 0.999
ADAM_EPS = 1e-08
ADAM_WD = 0.01
ADAM_STEP = 10

WEIGHTS = ['norm_mix', 'w_in', 'conv_a_w', 'conv_a_b', 'ln_a_g', 'ln_a_b', 'dn_conv_w', 'dn_a_log', 'dn_dt_bias',
           'dn_norm_g', 'gm_ln_g', 'gm_ln_b', 'gm_ws', 'gm_bs', 'pool_w', 'pool_scale', 'w_branch', 'w_out',
           'norm_xa', 'norm_mem', 'xa_wq', 'xa_wkv', 'xa_wo', 'norm_mlp', 'mlp_w1', 'mlp_w2', 'norm_f']
SHARDED = ['w_in', 'conv_a_w', 'dn_conv_w', 'w_branch', 'w_out', 'xa_wq', 'xa_wkv', 'xa_wo', 'mlp_w1', 'mlp_w2']
SENT_AS_BF16 = ['w_in', 'w_branch', 'w_out', 'xa_wq', 'xa_wkv', 'xa_wo', 'mlp_w1', 'mlp_w2']
REPLICATED = [n for n in WEIGHTS if n not in SHARDED]

Z_W = 8832
Z_ORDER = ((4616, 8712), (0, 3072), (3080, 4616), (3072, 3080))
ZB_GATE, ZB_A1, ZB_A2, ZB_Q, ZB_K, ZB_V, ZB_DGATE, ZB_GU, ZB_GV, ZB_POOL = 0, 8, 9, 10, 11, 12, 13, 14, 15, 16
ZB128_BD = 68


def _cparams(*sem):
    return pltpu.CompilerParams(dimension_semantics=sem, vmem_limit_bytes=VMEM_LIMIT_V7X)


def _dot(a, b, ca, cb):
    return lax.dot_general(a.astype(BF16), b.astype(BF16), (((ca,), (cb,)), ((), ())), preferred_element_type=F32)


def _doth(a, b, ca, cb):
    return lax.dot_general(a, b, (((ca,), (cb,)), ((), ())), precision=HIGH, preferred_element_type=F32)


def _make_mm(dot):
    @jax.custom_vjp
    def nn(a, b):
        return dot(a, b, 1, 0)
    nn.defvjp(lambda a, b: (dot(a, b, 1, 0), (a, b)), lambda r, g: (dot(g, r[1], 1, 1), dot(r[0], g, 0, 0)))

    @jax.custom_vjp
    def nt(a, b):
        return dot(a, b, 1, 1)
    nt.defvjp(lambda a, b: (dot(a, b, 1, 1), (a, b)), lambda r, g: (dot(g, r[1], 1, 0), dot(g, r[0], 0, 0)))

    @jax.custom_vjp
    def tn(a, b):
        return dot(a, b, 0, 0)
    tn.defvjp(lambda a, b: (dot(a, b, 0, 0), (a, b)), lambda r, g: (dot(r[1], g, 1, 1), dot(r[0], g, 1, 0)))
    return nn, nt, tn


mm, mm_nt, mm_tn = _make_mm(_dot)
mmh, mmh_nt, mmh_tn = _make_mm(_doth)


@jax.custom_vjp
def _mmw(a, w, wz):
    return _dot(a, w, 1, 0)


_mmw.defvjp(lambda a, w, wz: (_dot(a, w, 1, 0), (a, w)),
            lambda r, g: (_dot(g, r[1], 1, 1), jnp.zeros_like(r[1]), _dot(r[0], g, 0, 0)))


def mmw(a, wpair):
    w, wz = wpair
    return _dot(a, w, 1, 0) if wz is None else _mmw(a, w, wz)


def wsel(wpair, n):
    return (wpair[0][n], None if wpair[1] is None else wpair[1][n])


def _sigmoid(x):
    return 1.0 / (1.0 + jnp.exp(-x))


def _silu(x):
    return x * _sigmoid(x)


def _rms(x, g, eps=1e-6):
    return x * lax.rsqrt(jnp.mean(x * x, axis=-1, keepdims=True) + eps) * g


def _ln(x, g, b, eps=1e-5):
    mu = jnp.mean(x, axis=-1, keepdims=True)
    d = x - mu
    return d * lax.rsqrt(jnp.mean(d * d, axis=-1, keepdims=True) + eps) * g + b


def _gelu(x):
    return 0.5 * x * (1.0 + lax.erf(x * (2.0 ** -0.5)))


def _softplus(x):
    return jnp.maximum(x, 0.0) + jnp.log(1.0 + jnp.exp(-jnp.abs(x)))


def _row_spec(T, width, cb):
    return pl.BlockSpec((T, width), lambda i: (i, cb))


def _whole_spec(p):
    nd = p.ndim
    return pl.BlockSpec(p.shape, lambda i: (0,) * nd)


def _load_params(refs, kinds, with_zeros):
    out = []
    for r, k in zip(refs, kinds):
        if k == 'w':
            out.append((r[...], jnp.zeros(r.shape, F32) if with_zeros else None))
        else:
            out.append(r[...].astype(F32))
    return out


def rows_fwd(f, rows, params, kinds, outs, T, name, ride=None):
    S = rows[0][0].shape[0]
    nr, npar = len(rows), len(params)

    def body(*refs):
        r = [x[...].astype(F32) for x in refs[:nr]]
        p = _load_params(refs[nr:nr + npar], kinds, False)
        res = f(*r, *p)
        for o_ref, o in zip(refs[nr + npar:], res):
            o_ref[...] = o.astype(o_ref.dtype)

    return call_kernel(
        body, (S // T,), [_row_spec(T, w, cb) for _, w, cb in rows] + [_whole_spec(p) for p in params],
        [_row_spec(T, w, 0) for w, _ in outs], [jax.ShapeDtypeStruct((S, w), dt) for w, dt in outs],
        [a for a, _, _ in rows] + list(params), name, ("parallel",), ride=ride)


def rows_bwd(f, rows, params, kinds, cts, row_dtypes, T, name, ride=None):
    S = rows[0][0].shape[0]
    nr, npar, nc = len(rows), len(params), len(cts)
    want = [i for i, dt in enumerate(row_dtypes) if dt is not None]

    def body(*refs):
        r = [x[...].astype(F32) for x in refs[:nr]]
        p = _load_params(refs[nr:nr + npar], kinds, True)
        g = [x[...].astype(F32) for x in refs[nr + npar:nr + npar + nc]]
        d_rows = refs[nr + npar + nc:nr + npar + nc + len(want)]
        d_params = refs[nr + npar + nc + len(want):]
        _, vjp = jax.vjp(f, *r, *p)
        grads = vjp(tuple(g))
        for o_ref, i in zip(d_rows, want):
            o_ref[...] = grads[i].astype(o_ref.dtype)
        first = pl.program_id(0) == 0
        for o_ref, gp, k in zip(d_params, grads[nr:], kinds):
            gp = gp[1] if k == 'w' else gp

            @pl.when(first)
            def _():
                o_ref[...] = gp

            @pl.when(jnp.logical_not(first))
            def _():
                o_ref[...] += gp

    res = call_kernel(
        body, (S // T,),
        ([_row_spec(T, w, cb) for _, w, cb in rows] + [_whole_spec(p) for p in params]
         + [_row_spec(T, c.shape[1], 0) for c in cts]),
        [_row_spec(T, rows[i][1], 0) for i in want] + [_whole_spec(p) for p in params],
        ([jax.ShapeDtypeStruct((S, rows[i][1]), row_dtypes[i]) for i in want]
         + [jax.ShapeDtypeStruct(p.shape, F32) for p in params]),
        [a for a, _, _ in rows] + list(params) + list(cts), name, ("arbitrary",), ride=ride)
    return res[:len(want)], res[len(want):]


def f_norm(x, g):
    return (_rms(x, g),)


def f_glu(a1, a2):
    return (a1 * _sigmoid(a2),)


def f_lnsilu(cv, cb, g, b):
    return (_silu(_ln(cv + cb, g, b)),)


def f_gbeta(bd, alp, dtp):
    j = lax.broadcasted_iota(jnp.int32, (128, MIX_W), 0)
    head = lax.broadcasted_iota(jnp.int32, (128, MIX_W), 1) // DN_HEAD_DIM
    e_lo = (j == head).astype(F32)
    e_hi = (j == head + DN_HEADS).astype(F32)
    beta = _sigmoid(mmh(bd, e_lo))
    a_log = jnp.sum(mmh(alp, e_lo), axis=0, keepdims=True)
    dt_bias = jnp.sum(mmh(dtp, e_lo), axis=0, keepdims=True)
    g = -jnp.exp(a_log) * _softplus(mmh(bd, e_hi) + dt_bias)
    return g, beta


def f_gmlp(u_in, v_in, lg, lb, ws, b0, b1, b2, b3):
    T = u_in.shape[0]
    u = _gelu(u_in)
    vg = _ln(_gelu(v_in), lg, lb)
    tril = (lax.broadcasted_iota(jnp.int32, (GM_CHUNK, GM_CHUNK), 0)
            >= lax.broadcasted_iota(jnp.int32, (GM_CHUNK, GM_CHUNK), 1))
    bias = (b0, b1, b2, b3)
    n = T // GM_CHUNK
    w = [jnp.where(tril, ws[gi], 0.0) for gi in range(GM_GROUPS)]
    mixed = [[mm(w[gi], vg[r * GM_CHUNK:(r + 1) * GM_CHUNK, gi * 128:(gi + 1) * 128]) for gi in range(GM_GROUPS)]
             for r in range(n)]
    chunks = [jnp.concatenate([mixed[r][gi] + bias[gi] for gi in range(GM_GROUPS)], axis=1) for r in range(n)]
    return (u * (chunks[0] if n == 1 else jnp.concatenate(chunks, axis=0)),)


def f_pool(cs, xin, pw, scale):
    T = cs.shape[0]
    t = pl.program_id(0) * T + lax.broadcasted_iota(jnp.int32, (T, 128), 0)
    pooled = [cs[:, gi * 128:(gi + 1) * 128] / jnp.minimum(t + 1, win).astype(F32) - xin[:, gi * 128:(gi + 1) * 128]
              for gi, win in enumerate(POOL_WINDOWS)]
    cols = [mm(pooled[gi], pw[gi]) for gi in range(len(POOL_WINDOWS))]
    return (jnp.concatenate(cols, axis=1) * scale,)


def f_merge(a, o, c, p, gate, wb):
    D = gate.shape[1] // 4
    proj = [mmw(br, wsel(wb, n)) for n, br in enumerate((a, o, c, p))]
    terms = [_sigmoid(gate[:, n * D:(n + 1) * D]) * proj[n] for n in range(4)]
    return (terms[0] + terms[1] + terms[2] + terms[3],)


def f_kv(mem, nm, wkv):
    return (mmw(_rms(mem, nm), wkv),)


def f_xattn(x, kv, nx, wq, wo):
    D = x.shape[1]
    hd = D // XA_HEADS
    q = mmw(_rms(x, nx), wq)
    hs = range(XA_HEADS)
    s = [mm_nt(q[:, h * hd:(h + 1) * hd], kv[:, h * hd:(h + 1) * hd]) * (hd ** -0.5) for h in hs]
    e = [jnp.exp(t - jnp.max(t, axis=-1, keepdims=True)) for t in s]
    pr = [t / jnp.sum(t, axis=-1, keepdims=True) for t in e]
    heads = [mm(pr[h], kv[:, D + h * hd:D + (h + 1) * hd]) for h in hs]
    return (x + mmw(jnp.concatenate(heads, axis=1), wo),)


def f_loss(x, tgt, nf):
    err = _rms(x, nf) - tgt
    return (0.5 * jnp.sum(jnp.mean(err * err, axis=-1, keepdims=True), axis=0, keepdims=True),)


def _mm_tiles(mode, M, N, K):
    wide = 512 if N % 512 == 0 else 384
    if mode == 'tn':
        return min(M, 512), wide, K
    tm = min(M, 1024)
    if K <= 1024:
        if N % 512 == 0:
            return tm, wide, K
        return min(M, 512), N // 3, K
    if K % 2048 == 0:
        return tm, min(N, 1024) if mode == 'nt' else wide, 2048
    return tm, min(N, 1024), K // 3


def matmul(a, b, mode, *, name, out_dtype=F32, res=None, act=None, gate=None, ride=None):
    M, K = a.shape if mode != 'tn' else a.shape[::-1]
    tm, tn, tk = _mm_tiles(mode, M, b.shape[0] if mode == 'nt' else b.shape[1], K)
    N = b.shape[0] if mode == 'nt' else b.shape[1]
    assert M % tm == 0 and N % tn == 0 and K % tk == 0, (a.shape, b.shape, mode, tm, tn, tk)
    nk = K // tk
    size = lambda t: t.size * t.dtype.itemsize
    rows_outer = size(a) + (M // tm) * size(b) <= size(b) + (N // tn) * size(a)

    def spec(shape, index):
        if rows_outer:
            return pl.BlockSpec(shape, lambda i, j, k: index(i, j, k))
        return pl.BlockSpec(shape, lambda j, i, k: index(i, j, k))

    if mode == 'nn':
        a_spec, b_spec = spec((tm, tk), lambda i, j, k: (i, k)), spec((tk, tn), lambda i, j, k: (k, j))
        ca, cb = 1, 0
    elif mode == 'nt':
        a_spec, b_spec = spec((tm, tk), lambda i, j, k: (i, k)), spec((tn, tk), lambda i, j, k: (j, k))
        ca, cb = 1, 1
    else:
        a_spec, b_spec = spec((tk, tm), lambda i, j, k: (k, i)), spec((tk, tn), lambda i, j, k: (k, j))
        ca, cb = 0, 0
    o_spec = spec((tm, tn), lambda i, j, k: (i, j))
    extra = [e for e in (res, gate) if e is not None]

    def body(*refs):
        a_ref, b_ref = refs[:2]
        e_refs = refs[2:2 + len(extra)]
        o_refs = refs[2 + len(extra):2 + len(extra) + (2 if act else 1)]
        part = _dot(a_ref[...], b_ref[...], ca, cb)

        def finish(acc):
            if res is not None:
                acc = acc + e_refs[0][...]
            if gate is not None:
                acc = acc * (2.0 * jnp.maximum(e_refs[-1][...], 0.0))
            o_refs[0][...] = acc.astype(o_refs[0].dtype)
            if act:
                r = jnp.maximum(acc, 0.0)
                o_refs[1][...] = (r * r).astype(o_refs[1].dtype)

        if nk == 1:
            finish(part)
        else:
            acc_ref = refs[-1]
            k = pl.program_id(2)

            @pl.when(k == 0)
            def _():
                acc_ref[...] = part

            @pl.when(k > 0)
            def _():
                acc_ref[...] += part

            @pl.when(k == nk - 1)
            def _():
                finish(acc_ref[...])

    out_shape = [jax.ShapeDtypeStruct((M, N), out_dtype)]
    if act:
        out_shape.append(jax.ShapeDtypeStruct((M, N), BF16))
    res_ = call_kernel(
        body, (M // tm, N // tn, nk) if rows_outer else (N // tn, M // tm, nk),
        [a_spec, b_spec] + [o_spec] * len(extra), [o_spec] * len(out_shape), out_shape, [a, b, *extra], name,
        ("parallel", "parallel", "arbitrary"), scratch=[pltpu.VMEM((tm, tn), F32)] if nk > 1 else [], ride=ride)
    return res_ if act else res_[0]


def _conv_rows(S):
    return min(S, 512)


def conv_fwd(x, cb0, w, name):
    S = x.shape[0]
    K = w.shape[0]
    R = _conv_rows(S)

    def body(x_ref, w_ref, y_ref, pad_ref):
        pad_ref[pl.ds(0, CONV_PAD), :] = jnp.zeros((CONV_PAD, 128), F32)
        pad_ref[pl.ds(CONV_PAD, S), :] = x_ref[...]
        wv = w_ref[...]

        def chunk(r, carry):
            r0 = pl.multiple_of(r * R, R)
            win = pad_ref[pl.ds(r0, R + CONV_PAD), :]
            acc = jnp.zeros((R, 128), F32)
            for s in range(K):
                sh = win if s == 0 else pltpu.roll(win, s, 0)
                acc = acc + sh[CONV_PAD:, :] * wv[K - 1 - s:K - s, :]
            y_ref[pl.ds(r0, R), :] = acc
            return carry

        lax.fori_loop(0, S // R, chunk, 0)

    return pl.pallas_call(
        body, grid=(4,), name=name,
        in_specs=[pl.BlockSpec((S, 128), lambda j: (0, cb0 * 4 + j)), pl.BlockSpec((K, 128), lambda j: (0, j))],
        out_specs=pl.BlockSpec((S, 128), lambda j: (0, j)),
        out_shape=jax.ShapeDtypeStruct((S, MIX_W), F32),
        scratch_shapes=[pltpu.VMEM((S + CONV_PAD, 128), F32)],
        compiler_params=_cparams("parallel"),
    )(x, w)


def conv_bwd(x, cb0, w, dy, name, add=None, out_dtype=F32):
    S = x.shape[0]
    K = w.shape[0]
    R = _conv_rows(S)
    W = R + CONV_PAD

    def body(*refs):
        x_ref, w_ref, dy_ref = refs[:3]
        add_ref = refs[3] if add is not None else None
        dx_ref, dw_ref, xpad_ref, dypad_ref = refs[-4:]
        xpad_ref[pl.ds(0, CONV_PAD), :] = jnp.zeros((CONV_PAD, 128), F32)
        xpad_ref[pl.ds(CONV_PAD, S), :] = x_ref[...]
        dypad_ref[pl.ds(S, CONV_PAD), :] = jnp.zeros((CONV_PAD, 128), F32)
        dypad_ref[pl.ds(0, S), :] = dy_ref[...].astype(F32)
        dw_ref[...] = jnp.zeros((K, 128), F32)
        wv = w_ref[...]

        def chunk(r, carry):
            r0 = pl.multiple_of(r * R, R)
            xwin = xpad_ref[pl.ds(r0, W), :]
            dwin = dypad_ref[pl.ds(r0, W), :]
            dyc = dwin[:R, :]
            acc = jnp.zeros((R, 128), F32)
            for s in range(K):
                up = dwin if s == 0 else pltpu.roll(dwin, W - s, 0)
                acc = acc + up[:R, :] * wv[K - 1 - s:K - s, :]
                xs = xwin if s == 0 else pltpu.roll(xwin, s, 0)
                dw_ref[pl.ds(K - 1 - s, 1), :] += jnp.sum(dyc * xs[CONV_PAD:, :], axis=0, keepdims=True)
            if add_ref is not None:
                acc = acc + add_ref[pl.ds(r0, R), :].astype(F32)
            dx_ref[pl.ds(r0, R), :] = acc.astype(dx_ref.dtype)
            return carry

        lax.fori_loop(0, S // R, chunk, 0)

    col = pl.BlockSpec((S, 128), lambda j: (0, j))
    ins = [x, w, dy] + ([add] if add is not None else [])
    return pl.pallas_call(
        body, grid=(4,), name=name,
        in_specs=[pl.BlockSpec((S, 128), lambda j: (0, cb0 * 4 + j)), pl.BlockSpec((K, 128), lambda j: (0, j)), col]
        + ([col] if add is not None else []),
        out_specs=[col, pl.BlockSpec((K, 128), lambda j: (0, j))],
        out_shape=[jax.ShapeDtypeStruct((S, MIX_W), out_dtype), jax.ShapeDtypeStruct((K, MIX_W), F32)],
        scratch_shapes=[pltpu.VMEM((S + CONV_PAD, 128), F32), pltpu.VMEM((S + CONV_PAD, 128), F32)],
        compiler_params=_cparams("parallel"),
    )(*ins)


DN_PREP_ROWS = 2 * DN_CHUNK


def _unit_lower_inverses(mats):
    C = mats[0].shape[0]
    eye = (lax.broadcasted_iota(jnp.int32, (C, C), 0) == lax.broadcasted_iota(jnp.int32, (C, C), 1)).astype(F32)
    invs = [eye - a for a in mats]
    pws = [mmh(a, a) for a in mats]
    for _ in range(5):
        both = [mmh(jnp.concatenate([inv, pw], axis=0), pw) for inv, pw in zip(invs, pws)]
        invs = [inv + b[:C] for inv, b in zip(invs, both)]
        pws = [b[C:] for b in both]
    return invs


@jax.custom_vjp
def _known_inverse(a, inv):
    return inv


_known_inverse.defvjp(lambda a, inv: (inv, inv),
                      lambda inv, g: (-mmh_nt(mmh_tn(inv, g), inv), jnp.zeros_like(inv)))


def _delta_prep(qc, kc, vc, ge, be, inv_known=None):
    C, Dh = DN_CHUNK, DN_HEAD_DIM
    n = qc.shape[0] // C
    ii = lax.broadcasted_iota(jnp.int32, (C, C), 0)
    jj = lax.broadcasted_iota(jnp.int32, (C, C), 1)
    causal, strict = ii >= jj, ii > jj
    sum_lhs = jnp.concatenate([causal.astype(F32), jnp.ones((C, C), F32)], axis=0)
    mean_lanes = jnp.full((C, Dh), 1.0 / Dh, F32)
    pairs = [(r, h) for r in range(n) for h in range(DN_HEADS)]
    pick = lambda t, w: [t[r * C:(r + 1) * C, h * w:(h + 1) * w] for r, h in pairs]
    q, k, v = [[_silu(t) for t in pick(b, Dh)] for b in (qc, kc, vc)]
    q = [t * lax.rsqrt(jnp.sum(t * t, axis=-1, keepdims=True) + 1e-6) * (Dh ** -0.5) for t in q]
    k = [t * lax.rsqrt(jnp.sum(t * t, axis=-1, keepdims=True) + 1e-6) for t in k]
    beta = pick(be, Dh)
    sums = [mmh(sum_lhs, g) for g in pick(ge, Dh)]
    gam, g_last = [s[:C] for s in sums], [s[C:] for s in sums]
    gam_row = [mmh_nt(mean_lanes, t) for t in gam]
    decay = [jnp.where(causal, jnp.exp(jnp.where(causal, gc[:, :C] - gr, 0.0)), 0.0) for gc, gr in zip(gam, gam_row)]
    kb = [a * b for a, b in zip(k, beta)]
    scores = [mm_nt(jnp.concatenate([a, b], axis=0), c) for a, b, c in zip(kb, q, k)]
    a = [jnp.where(strict, s[:C] * d, 0.0) for s, d in zip(scores, decay)]
    if inv_known is None:
        inv = _unit_lower_inverses(a)
    else:
        inv = [_known_inverse(m, inv_known[h][r * C:(r + 1) * C]) for m, (r, h) in zip(a, pairs)]
    e_gam = [jnp.exp(t) for t in gam]
    uw = [mmh(i, jnp.concatenate([vv * b, kk * e], axis=1)) for i, vv, b, kk, e in zip(inv, v, beta, kb, e_gam)]

    def wide(parts):
        rows = [jnp.concatenate(parts[r * DN_HEADS:(r + 1) * DN_HEADS], axis=1) for r in range(n)]
        return rows[0] if n == 1 else jnp.concatenate(rows, axis=0)

    def narrow(parts):
        per_head = [[parts[r * DN_HEADS + h] for r in range(n)] for h in range(DN_HEADS)]
        return [p[0] if n == 1 else jnp.concatenate(p, axis=0) for p in per_head]

    return (wide([t[:, :Dh] for t in uw]), wide([t[:, Dh:] for t in uw]), wide([a_ * b_ for a_, b_ in zip(q, e_gam)]),
            wide([kk * jnp.exp(gl - gm) for kk, gl, gm in zip(k, g_last, gam)]), wide([jnp.exp(t) for t in g_last]),
            narrow([s[C:] * d for s, d in zip(scores, decay)]), narrow(inv))


def _delta_step(states, u, w, qd, kd, sc, attns, gate, ng):
    C, Dh = DN_CHUNK, DN_HEAD_DIM
    heads = range(DN_HEADS)
    cut = lambda t: [t[:, h * Dh:(h + 1) * Dh] for h in heads]
    u, w, qd, kd, sc, gate = cut(u), cut(w), cut(qd), cut(kd), cut(sc), cut(gate)
    on_state = [mm(jnp.concatenate([w[h], qd[h]], axis=0), states[h]) for h in heads]
    v_new = [u[h] - on_state[h][:C] for h in heads]
    o = [on_state[h][C:] + mm(attns[h], v_new[h]) for h in heads]
    new_states = [states[h] * jnp.concatenate([sc[h], sc[h]], axis=0) + mm_tn(kd[h], v_new[h]) for h in heads]
    outs = [_rms(o[h], ng) * _silu(gate[h]) for h in heads]
    return new_states, jnp.concatenate(outs, axis=1)


def call_kernel(body, grid, in_specs, out_specs, out_shape, operands, name, semantics, scratch=(), ride=None):
    if ride is None:
        return pl.pallas_call(body, grid=grid, name=name, in_specs=list(in_specs), out_specs=list(out_specs),
                              out_shape=list(out_shape), scratch_shapes=list(scratch),
                              compiler_params=_cparams(*semantics))(*operands)
    n_out = len(out_specs)
    res = pl.pallas_call(
        carry(body, len(in_specs), n_out, len(scratch), ride, grid), grid=grid, name=name,
        in_specs=list(in_specs) + [ANY] * len(ride.inputs), out_specs=list(out_specs) + [ANY] * len(ride.out_shape),
        out_shape=list(out_shape) + ride.out_shape, scratch_shapes=list(scratch) + ride.scratch,
        compiler_params=_cparams(*["arbitrary"] * len(grid)),
    )(*operands, *ride.inputs)
    ride.results = list(res[n_out:])
    return list(res[:n_out])


def delta_prep_fwd(qc, kc, vc, ge, be, name, ride=None):
    S = qc.shape[0]
    R = min(S, DN_PREP_ROWS)

    def body(q_ref, k_ref, v_ref, ge_ref, be_ref, u_ref, w_ref, qd_ref, kd_ref, sc_ref, at_ref, iv_ref):
        u, w, qd, kd, sc, attns, invs = _delta_prep(q_ref[...], k_ref[...], v_ref[...], ge_ref[...], be_ref[...])
        for ref, val in zip((u_ref, w_ref, qd_ref, kd_ref, sc_ref), (u, w, qd, kd, sc)):
            ref[...] = val
        for h in range(DN_HEADS):
            at_ref[h] = attns[h]
            iv_ref[h] = invs[h]

    blk = pl.BlockSpec((R, MIX_W), lambda n: (n, 0))
    hblk = pl.BlockSpec((DN_HEADS, R, DN_CHUNK), lambda n: (0, n, 0))
    wide = jax.ShapeDtypeStruct((S, MIX_W), F32)
    narrow = jax.ShapeDtypeStruct((DN_HEADS, S, DN_CHUNK), F32)
    return call_kernel(body, (S // R,), [blk] * 5, [blk] * 5 + [hblk] * 2, [wide] * 5 + [narrow] * 2,
                       [qc, kc, vc, ge, be], name, ("parallel",), ride=ride)


def delta_prep_bwd(qc, kc, vc, ge, be, inv, cts, name, ride=None):
    S = qc.shape[0]
    R = min(S, DN_PREP_ROWS)

    def body(q_ref, k_ref, v_ref, ge_ref, be_ref, iv_ref, du_ref, dw_ref, dqd_ref, dkd_ref, dsc_ref, dat_ref, *outs):
        invs = [iv_ref[h] for h in range(DN_HEADS)]
        fn = lambda *blocks: _delta_prep(*blocks, invs)[:6]
        _, vjp = jax.vjp(fn, q_ref[...], k_ref[...], v_ref[...], ge_ref[...], be_ref[...])
        grads = vjp((du_ref[...], dw_ref[...], dqd_ref[...], dkd_ref[...], dsc_ref[...],
                     [dat_ref[h] for h in range(DN_HEADS)]))
        for ref, g in zip(outs, grads):
            ref[...] = g

    blk = pl.BlockSpec((R, MIX_W), lambda n: (n, 0))
    hblk = pl.BlockSpec((DN_HEADS, R, DN_CHUNK), lambda n: (0, n, 0))
    return call_kernel(body, (S // R,), [blk] * 5 + [hblk] + [blk] * 5 + [hblk], [blk] * 5,
                       [jax.ShapeDtypeStruct((S, MIX_W), F32)] * 5, [qc, kc, vc, ge, be, inv, *cts], name, ("parallel",),
                       ride=ride)


def delta_step_fwd(prep, z, ng, name):
    S = prep[0].shape[0]
    N = S // DN_CHUNK
    C = DN_CHUNK

    def body(u_ref, w_ref, qd_ref, kd_ref, sc_ref, at_ref, gate_ref, ng_ref, o_ref, st_ref, s_ref):
        @pl.when(pl.program_id(0) == 0)
        def _():
            s_ref[...] = jnp.zeros(s_ref.shape, F32)

        states = [s_ref[h] for h in range(DN_HEADS)]
        for h in range(DN_HEADS):
            st_ref[0, h] = states[h]
        new_states, o = _delta_step(states, u_ref[...], w_ref[...], qd_ref[...], kd_ref[...], sc_ref[...],
                                    [at_ref[h] for h in range(DN_HEADS)], gate_ref[...], ng_ref[...])
        for h in range(DN_HEADS):
            s_ref[h] = new_states[h]
        o_ref[...] = o.astype(o_ref.dtype)

    blk = pl.BlockSpec((C, MIX_W), lambda n: (n, 0))
    return pl.pallas_call(
        body, grid=(N,), name=name,
        in_specs=[blk] * 5 + [pl.BlockSpec((DN_HEADS, C, C), lambda n: (0, n, 0)),
                              pl.BlockSpec((C, MIX_W), lambda n: (n, ZB_DGATE)),
                              pl.BlockSpec((1, DN_HEAD_DIM), lambda n: (0, 0))],
        out_specs=[blk, pl.BlockSpec((1, DN_HEADS, DN_HEAD_DIM, DN_HEAD_DIM), lambda n: (n, 0, 0, 0))],
        out_shape=[jax.ShapeDtypeStruct((S, MIX_W), BF16),
                   jax.ShapeDtypeStruct((N, DN_HEADS, DN_HEAD_DIM, DN_HEAD_DIM), F32)],
        scratch_shapes=[pltpu.VMEM((DN_HEADS, DN_HEAD_DIM, DN_HEAD_DIM), F32)],
        compiler_params=_cparams("arbitrary"),
    )(*prep, z, ng)


def delta_step_bwd(prep, z, ng, states, do, name):
    S = prep[0].shape[0]
    N = S // DN_CHUNK
    C = DN_CHUNK

    def body(u_ref, w_ref, qd_ref, kd_ref, sc_ref, at_ref, gate_ref, ng_ref, st_ref, do_ref,
             du_ref, dw_ref, dqd_ref, dkd_ref, dsc_ref, dat_ref, dgate_ref, dng_ref, ds_ref):
        first = pl.program_id(0) == 0

        @pl.when(first)
        def _():
            ds_ref[...] = jnp.zeros(ds_ref.shape, F32)

        args = ([st_ref[0, h] for h in range(DN_HEADS)], u_ref[...], w_ref[...], qd_ref[...], kd_ref[...],
                sc_ref[...], [at_ref[h] for h in range(DN_HEADS)], gate_ref[...].astype(F32), ng_ref[...])
        _, vjp = jax.vjp(_delta_step, *args)
        d_states, du, dw, dqd, dkd, dsc, dat, dgate, dng = vjp(([ds_ref[h] for h in range(DN_HEADS)],
                                                               do_ref[...].astype(F32)))
        for h in range(DN_HEADS):
            ds_ref[h] = d_states[h]
            dat_ref[h] = dat[h]
        for ref, g in zip((du_ref, dw_ref, dqd_ref, dkd_ref, dsc_ref), (du, dw, dqd, dkd, dsc)):
            ref[...] = g
        dgate_ref[...] = dgate.astype(dgate_ref.dtype)

        @pl.when(first)
        def _():
            dng_ref[...] = dng

        @pl.when(jnp.logical_not(first))
        def _():
            dng_ref[...] += dng

    blk = pl.BlockSpec((C, MIX_W), lambda n: (N - 1 - n, 0))
    hblk = pl.BlockSpec((DN_HEADS, C, C), lambda n: (0, N - 1 - n, 0))
    ngs = pl.BlockSpec((1, DN_HEAD_DIM), lambda n: (0, 0))
    f32o = jax.ShapeDtypeStruct((S, MIX_W), F32)
    return pl.pallas_call(
        body, grid=(N,), name=name,
        in_specs=[blk] * 5 + [hblk, pl.BlockSpec((C, MIX_W), lambda n: (N - 1 - n, ZB_DGATE)), ngs,
                              pl.BlockSpec((1, DN_HEADS, DN_HEAD_DIM, DN_HEAD_DIM), lambda n: (N - 1 - n, 0, 0, 0)),
                              blk],
        out_specs=[blk] * 5 + [hblk, blk, ngs],
        out_shape=[f32o] * 5 + [jax.ShapeDtypeStruct((DN_HEADS, S, C), F32), jax.ShapeDtypeStruct((S, MIX_W), BF16),
                                jax.ShapeDtypeStruct((1, DN_HEAD_DIM), F32)],
        scratch_shapes=[pltpu.VMEM((DN_HEADS, DN_HEAD_DIM, DN_HEAD_DIM), F32)],
        compiler_params=_cparams("arbitrary"),
    )(*prep, z, ng, states, do)


ANY = pl.BlockSpec(memory_space=pl.ANY)


def _place():
    return lax.axis_index("x"), lax.axis_index("y"), lax.axis_index("c")


PHASES = ('start', 'forward', 'finish')


class Ride:
    def __init__(self, inputs, out_shape, scratch, run):
        self.inputs, self.out_shape, self.scratch, self.run = list(inputs), list(out_shape), list(scratch), run
        self.results = None


def run_alone(ride, name):
    n_in, n_out = len(ride.inputs), len(ride.out_shape)

    def body(*refs):
        for phase in PHASES:
            ride.run(phase, refs[:n_in], refs[n_in:n_in + n_out], refs[n_in + n_out:])

    ride.results = list(pl.pallas_call(body, name=name, in_specs=[ANY] * n_in, out_specs=[ANY] * n_out,
                                       out_shape=ride.out_shape, scratch_shapes=ride.scratch)(*ride.inputs))
    return ride.results


def carry(body, n_in, n_out, n_scratch, ride, grid):
    r_in, r_out = len(ride.inputs), len(ride.out_shape)
    steps = math.prod(grid)
    late = (7 * steps) // 8

    def carrying(*refs):
        b = n_in + r_in
        c = b + n_out
        d = c + r_out
        e = d + n_scratch
        ride_refs = (refs[n_in:b], refs[c:d], refs[e:])
        step = 0
        for axis, size in enumerate(grid):
            step = step * size + pl.program_id(axis)

        @pl.when(step == 0)
        def _():
            ride.run('start', *ride_refs)

        @pl.when(step == late)
        def _():
            ride.run('forward', *ride_refs)

        body(*refs[:n_in], *refs[b:c], *refs[d:e])

        @pl.when(step == steps - 1)
        def _():
            ride.run('finish', *ride_refs)

    return carrying


def gather_ride(shards):
    n = len(shards)

    def run(phase, ins, outs, sems):
        send_sems, recv_sems = sems
        x, y, c = _place()
        me, sibling = (x, y, c), (x, y, 1 - c)
        chips = [(1 - x, y), (x, 1 - y), (1 - x, 1 - y)]

        def copy(a, k, block, to, src=None):
            row = 4 * block[0] + 2 * block[1] + block[2]
            return pltpu.make_async_remote_copy(
                src_ref=outs[a].at[row] if src is None else src, dst_ref=outs[a].at[row],
                send_sem=send_sems.at[a, k], recv_sem=recv_sems.at[a, k], device_id=to, device_id_type=MESH_ID)

        def first():
            return [cp for a in range(n) for cp in
                    [copy(a, 0, me, sibling, src=ins[a])]
                    + [copy(a, 1 + j, me, (*chip, c), src=ins[a]) for j, chip in enumerate(chips)]]

        def passed():
            return [copy(a, 4 + j, (*chip, c), sibling) for j, chip in enumerate(chips) for a in range(n)]

        if phase == 'start':
            for cp in first():
                cp.start()
        elif phase == 'forward':
            for cp, (j, chip, a) in zip(passed(), [(j, chip, a) for j, chip in enumerate(chips) for a in range(n)]):
                copy(a, 1 + j, (*chip, c), me).wait_recv()
                cp.start()
        else:
            for a in range(n):
                copy(a, 0, sibling, me).wait_recv()
                for j, chip in enumerate(chips):
                    copy(a, 4 + j, (*chip, 1 - c), me).wait_recv()
            for cp in first() + passed():
                cp.wait_send()

    return Ride(shards, [jax.ShapeDtypeStruct((8,) + s.shape, s.dtype) for s in shards],
                [pltpu.SemaphoreType.DMA((n, 7)), pltpu.SemaphoreType.DMA((n, 7))], run)


def all_gather(shards, name):
    return run_alone(gather_ride(shards), name)


def exchange_cores(grads, name):
    n = len(grads)

    def body(*refs):
        ins, outs = refs[:n], refs[n:2 * n]
        send_sems, recv_sems = refs[2 * n:]
        x, y, c = _place()
        copies = [pltpu.make_async_remote_copy(
            src_ref=ins[a].at[2 * k + 1 - c], dst_ref=outs[a].at[k], send_sem=send_sems.at[a, k],
            recv_sem=recv_sems.at[a, k], device_id=(x, y, 1 - c), device_id_type=MESH_ID)
            for a in range(n) for k in range(4)]
        for cp in copies:
            cp.start()
        for cp in copies:
            cp.wait()

    return pl.pallas_call(
        body, name=name, in_specs=[ANY] * n, out_specs=[ANY] * n,
        out_shape=[jax.ShapeDtypeStruct((4,) + g.shape[1:], g.dtype) for g in grads],
        scratch_shapes=[pltpu.SemaphoreType.DMA((n, 4))] * 2,
    )(*grads)


def chips_ride(parts):
    n = len(parts)

    def run(phase, ins, outs, sems):
        send_sems, recv_sems = sems
        x, y, c = _place()
        chips = [(1 - x, y), (x, 1 - y), (1 - x, 1 - y)]
        if phase == 'forward':
            return
        copies = [pltpu.make_async_remote_copy(
            src_ref=ins[a].at[2 * px + py], dst_ref=outs[a].at[j], send_sem=send_sems.at[a, j],
            recv_sem=recv_sems.at[a, j], device_id=(px, py, c), device_id_type=MESH_ID)
            for a in range(n) for j, (px, py) in enumerate(chips)]
        for cp in copies:
            cp.start() if phase == 'start' else cp.wait()

    return Ride(parts, [jax.ShapeDtypeStruct((3,) + p.shape[1:], p.dtype) for p in parts],
                [pltpu.SemaphoreType.DMA((n, 3))] * 2, run)


def exchange_chips(parts, name):
    return run_alone(chips_ride(parts), name)


def _as2d(a, lead=0):
    return a.reshape(a.shape[:lead] + (-1, a.shape[-1]))


def _row_tile(rows, cols, n_arrays):
    budget = VMEM_LIMIT_V7X // 2
    lanes = -(-cols // 128) * 128
    t = budget // (2 * n_arrays * lanes * 4)
    if t >= rows:
        return rows
    return max(16, t // 16 * 16)


def _index_operand(i):
    return jnp.asarray(i, jnp.int32).reshape(1)


def add_own_rows(grads, core, recv, name):
    _, R, C = grads.shape
    T = _row_tile(R, C, 3)

    def body(c_ref, g_ref, r_ref, o_ref):
        o_ref[...] = (g_ref[...].astype(F32) + r_ref[...].astype(F32)).astype(o_ref.dtype)

    blk = pl.BlockSpec((1, T, C), lambda k, i, c: (k, i, 0))
    return pl.pallas_call(
        body, name=name, out_shape=jax.ShapeDtypeStruct((4, R, C), grads.dtype),
        grid_spec=pltpu.PrefetchScalarGridSpec(
            num_scalar_prefetch=1, grid=(4, pl.cdiv(R, T)),
            in_specs=[pl.BlockSpec((1, T, C), lambda k, i, c: (2 * k + c[0], i, 0)), blk], out_specs=blk),
        compiler_params=_cparams("parallel", "parallel"),
    )(_index_operand(core), grads, recv)


def _adamw(w, g, m, v):
    m = ADAM_B1 * m + (1.0 - ADAM_B1) * g
    v = ADAM_B2 * v + (1.0 - ADAM_B2) * jnp.square(g)
    m_hat = m / (1.0 - ADAM_B1 ** ADAM_STEP)
    v_hat = v / (1.0 - ADAM_B2 ** ADAM_STEP)
    delta = -ADAM_LR * (m_hat / (jnp.sqrt(v_hat) + ADAM_EPS) + ADAM_WD * w)
    return delta, m, v


def adamw_sum(parts, w, m, v, name, own=None, own_row=None):
    P, R, C = parts.shape
    T = _row_tile(R, C, P + 8)
    has_own = own is not None

    def body(i_ref, *refs):
        refs = list(refs)
        own_ref = refs.pop(0) if has_own else None
        p_ref, w_ref, m_ref, v_ref, g_ref, d_ref, nm_ref, nv_ref = refs
        terms = ([own_ref[0]] if has_own else []) + [p_ref[k] for k in range(P)]
        terms = [t.astype(F32) for t in terms]
        g = terms[0]
        for t in terms[1:]:
            g = g + t
        d, nm, nv = _adamw(w_ref[...], g, m_ref[...], v_ref[...])
        g_ref[...] = g
        d_ref[...] = d
        nm_ref[...] = nm
        nv_ref[...] = nv

    blk = pl.BlockSpec((T, C), lambda i, r: (i, 0))
    in_specs = [pl.BlockSpec((P, T, C), lambda i, r: (0, i, 0)), blk, blk, blk]
    operands = [parts, w, m, v]
    if has_own:
        in_specs.insert(0, pl.BlockSpec((1, T, C), lambda i, r: (r[0], i, 0)))
        operands.insert(0, own)
    return pl.pallas_call(
        body, name=name, out_shape=[jax.ShapeDtypeStruct((R, C), F32)] * 4,
        grid_spec=pltpu.PrefetchScalarGridSpec(num_scalar_prefetch=1, grid=(pl.cdiv(R, T),), in_specs=in_specs,
                                               out_specs=[blk] * 4),
        compiler_params=_cparams("parallel"),
    )(_index_operand(0 if own_row is None else own_row), *operands)


def f_norm_res(x, g):
    return _rms(x, g), x


def _pool_taps():
    taps = np.zeros((16, MIX_W), np.float32)
    for gi, win in enumerate(POOL_WINDOWS):
        taps[16 - win:, gi * 128:(gi + 1) * 128] = 1.0
    return jnp.asarray(taps)


def loss_head(x, tgt, nf, T, name):
    S, D = x.shape

    def body(x_ref, t_ref, g_ref, l_ref, dx_ref, dg_ref):
        val, vjp = jax.vjp(f_loss, x_ref[...], t_ref[...], g_ref[...])
        dx, _, dg = vjp((jnp.ones((1, 1), F32),))
        dx_ref[...] = dx
        first = pl.program_id(0) == 0
        lv = jnp.broadcast_to(val[0], (1, 128))

        @pl.when(first)
        def _():
            l_ref[...] = lv
            dg_ref[...] = dg

        @pl.when(jnp.logical_not(first))
        def _():
            l_ref[...] += lv
            dg_ref[...] += dg

    row = pl.BlockSpec((T, D), lambda i: (i, 0))
    return pl.pallas_call(
        body, grid=(S // T,), name=name,
        in_specs=[row, row, pl.BlockSpec((1, D), lambda i: (0, 0))],
        out_specs=[pl.BlockSpec((1, 128), lambda i: (0, 0)), row, pl.BlockSpec((1, D), lambda i: (0, 0))],
        out_shape=[jax.ShapeDtypeStruct((1, 128), F32), jax.ShapeDtypeStruct((S, D), F32),
                   jax.ShapeDtypeStruct((1, D), F32)],
        compiler_params=_cparams("arbitrary"),
    )(x, tgt, nf)


FWD_HOSTS = ('w_in', 'delta_prep', 'mlp_w1')
BWD_HOSTS = ('xattn', 'merge')


def layer_fwd(x, mem, W, tag, rides={}):
    S, D = x.shape
    T = min(S, 256)
    sv = {'x': x}
    (h1,) = rows_fwd(f_norm, [(x, D, 0)], [W['norm_mix']], 'p', [(D, BF16)], T, tag + 'norm_mix')
    z = matmul(h1, W['w_in'], 'nn', name=tag + 'w_in', ride=rides.get('w_in'))
    (a_pre,) = rows_fwd(f_glu, [(z, 512, ZB_A1), (z, 512, ZB_A2)], [], '', [(512, F32)], T, tag + 'glu')
    a_cv = conv_fwd(a_pre, 0, W['conv_a_w'], tag + 'conv_a')
    (a,) = rows_fwd(f_lnsilu, [(a_cv, 512, 0)], [W['conv_a_b'], W['ln_a_g'], W['ln_a_b']], 'ppp', [(512, BF16)], T,
                    tag + 'ln_a')
    qc = conv_fwd(z, ZB_Q, W['dn_wq'], tag + 'conv_q')
    kc = conv_fwd(z, ZB_K, W['dn_wk'], tag + 'conv_k')
    vc = conv_fwd(z, ZB_V, W['dn_wv'], tag + 'conv_v')
    ge, be = rows_fwd(f_gbeta, [(z, 128, ZB128_BD)], [W['alp'], W['dtp']], 'pp', [(512, F32), (512, F32)], T,
                      tag + 'gbeta')
    *prep, dn_inv = delta_prep_fwd(qc, kc, vc, ge, be, tag + 'delta_prep', rides.get('delta_prep'))
    o, states = delta_step_fwd(prep, z, W['dn_norm_g'], tag + 'delta')
    gm_params = [W['gm_ln_g'], W['gm_ln_b'], W['gm_ws']] + W['gm_b']
    (c,) = rows_fwd(f_gmlp, [(z, 512, ZB_GU), (z, 512, ZB_GV)], gm_params, 'p' * 7, [(512, BF16)], T, tag + 'gmlp')
    cs = conv_fwd(z, ZB_POOL, _pool_taps(), tag + 'pool_sum')
    (p,) = rows_fwd(f_pool, [(cs, 512, 0), (z, 512, ZB_POOL)], [W['pool_w'], W['pool_scale']], 'pp', [(512, BF16)], T,
                    tag + 'pool')
    (merged,) = rows_fwd(f_merge, [(a, 512, 0), (o, 512, 0), (c, 512, 0), (p, 512, 0), (z, 4 * D, ZB_GATE)],
                         [W['w_branch']], 'w', [(D, BF16)], min(S, 128), tag + 'merge')
    x1 = matmul(merged, W['w_out'], 'nn', res=x, name=tag + 'w_out')
    (kv,) = rows_fwd(f_kv, [(mem, D, 0)], [W['norm_mem'], W['xa_wkv']], 'pw', [(2 * D, F32)], mem.shape[0],
                     tag + 'kv')
    (x2,) = rows_fwd(f_xattn, [(x1, D, 0)], [kv, W['norm_xa'], W['xa_wq'], W['xa_wo']], 'ppww', [(D, F32)], T,
                     tag + 'xattn')
    (h3,) = rows_fwd(f_norm, [(x2, D, 0)], [W['norm_mlp']], 'p', [(D, BF16)], T, tag + 'norm_mlp')
    pre, r = matmul(h3, W['mlp_w1'], 'nn', act='relu2', name=tag + 'mlp_w1', ride=rides.get('mlp_w1'))
    x3 = matmul(r, W['mlp_w2'], 'nn', res=x2, name=tag + 'mlp_w2')
    sv.update(h1=h1, z=z, a_pre=a_pre, a_cv=a_cv, a=a, qc=qc, kc=kc, vc=vc, ge=ge, be=be, prep=prep, dn_inv=dn_inv, o=o,
              states=states, c=c,
              cs=cs, p=p, merged=merged, x1=x1, kv=kv, x2=x2, h3=h3, pre=pre, r=r)
    return x3, sv


def layer_bwd(dx, mem, W, sv, tag, rides={}):
    S, D = dx.shape
    T = min(S, 256)
    tag = tag + 'b_'
    G = {}
    z = sv['z']
    da = matmul(dx, W['mlp_w2'], 'nt', gate=sv['pre'], out_dtype=BF16, name=tag + 'mlp_da')
    G['mlp_w2'] = matmul(sv['r'], dx, 'tn', name=tag + 'mlp_gw2')
    dh3 = matmul(da, W['mlp_w1'], 'nt', name=tag + 'mlp_dh')
    G['mlp_w1'] = matmul(sv['h3'], da, 'tn', name=tag + 'mlp_gw1')
    (dx2,), (G['norm_mlp'],) = rows_bwd(f_norm_res, [(sv['x2'], D, 0)], [W['norm_mlp']], 'p', [dh3, dx], [F32], T,
                                        tag + 'norm_mlp')
    (dx1,), (dkv, G['norm_xa'], G['xa_wq'], G['xa_wo']) = rows_bwd(
        f_xattn, [(sv['x1'], D, 0)], [sv['kv'], W['norm_xa'], W['xa_wq'], W['xa_wo']], 'ppww', [dx2], [F32],
        min(S, 128), tag + 'xattn', rides.get('xattn'))
    _, (G['norm_mem'], G['xa_wkv']) = rows_bwd(f_kv, [(mem, D, 0)], [W['norm_mem'], W['xa_wkv']], 'pw', [dkv], [None],
                                               mem.shape[0], tag + 'kv')
    dmerged = matmul(dx1, W['w_out'], 'nt', out_dtype=BF16, name=tag + 'dmerged')
    G['w_out'] = matmul(sv['merged'], dx1, 'tn', name=tag + 'gw_out')
    (d_a, d_o, d_c, d_p, dz_gate), (G['w_branch'],) = rows_bwd(
        f_merge, [(sv['a'], 512, 0), (sv['o'], 512, 0), (sv['c'], 512, 0), (sv['p'], 512, 0), (z, 4 * D, ZB_GATE)],
        [W['w_branch']], 'w', [dmerged], [F32, F32, F32, F32, BF16], min(S, 128), tag + 'merge', rides.get('merge'))
    (dcs, dpx), (G['pool_w'], G['pool_scale']) = rows_bwd(
        f_pool, [(sv['cs'], 512, 0), (z, 512, ZB_POOL)], [W['pool_w'], W['pool_scale']], 'pp', [d_p], [F32, F32], T,
        tag + 'pool')
    dz_pool, _ = conv_bwd(z, ZB_POOL, _pool_taps(), dcs, tag + 'pool_sum', add=dpx, out_dtype=BF16)
    gm_params = [W['gm_ln_g'], W['gm_ln_b'], W['gm_ws']] + W['gm_b']
    (dz_gu, dz_gv), gm_g = rows_bwd(f_gmlp, [(z, 512, ZB_GU), (z, 512, ZB_GV)], gm_params, 'p' * 7, [d_c],
                                    [BF16, BF16], T, tag + 'gmlp')
    G['gm_ln_g'], G['gm_ln_b'], G['gm_ws'] = gm_g[:3]
    G['gm_b'] = list(gm_g[3:])
    *d_prep, dz_dgate, G['dn_norm_g'] = delta_step_bwd(sv['prep'], z, W['dn_norm_g'], sv['states'], d_o, tag + 'delta')
    dqc, dkc, dvc, dge, dbe = delta_prep_bwd(sv['qc'], sv['kc'], sv['vc'], sv['ge'], sv['be'], sv['dn_inv'], d_prep,
                                             tag + 'delta_prep')
    (dz_bd,), (G['alp'], G['dtp']) = rows_bwd(f_gbeta, [(z, 128, ZB128_BD)], [W['alp'], W['dtp']], 'pp', [dge, dbe],
                                              [BF16], T, tag + 'gbeta')
    dz_q, G['dn_wq'] = conv_bwd(z, ZB_Q, W['dn_wq'], dqc, tag + 'conv_q', out_dtype=BF16)
    dz_k, G['dn_wk'] = conv_bwd(z, ZB_K, W['dn_wk'], dkc, tag + 'conv_k', out_dtype=BF16)
    dz_v, G['dn_wv'] = conv_bwd(z, ZB_V, W['dn_wv'], dvc, tag + 'conv_v', out_dtype=BF16)
    (da_cv,), (G['conv_a_b'], G['ln_a_g'], G['ln_a_b']) = rows_bwd(
        f_lnsilu, [(sv['a_cv'], 512, 0)], [W['conv_a_b'], W['ln_a_g'], W['ln_a_b']], 'ppp', [d_a], [F32], T,
        tag + 'ln_a')
    da_pre, G['conv_a_w'] = conv_bwd(sv['a_pre'], 0, W['conv_a_w'], da_cv, tag + 'conv_a')
    (dz_a1, dz_a2), _ = rows_bwd(f_glu, [(z, 512, ZB_A1), (z, 512, ZB_A2)], [], '', [da_pre], [BF16, BF16], T,
                                 tag + 'glu')
    dz = jnp.concatenate([dz_gate, dz_a1, dz_a2, dz_q, dz_k, dz_v, dz_dgate, dz_gu, dz_gv, dz_pool, dz_bd], axis=1)
    dh1 = matmul(dz, W['w_in'], 'nt', name=tag + 'dh1')
    G['w_in'] = matmul(sv['h1'], dz, 'tn', name=tag + 'gw_in')
    (dx0,), (G['norm_mix'],) = rows_bwd(f_norm_res, [(sv['x'], D, 0)], [W['norm_mix']], 'p', [dh1, dx1], [F32], T,
                                        tag + 'norm_mix')
    return dx0, G


def local_step(x, mem, tgt, norm_f, n_layers, weights_of, fwd_rides=None, bwd_rides=None):
    saved, weights, carried = [], [], {}
    for l in range(n_layers):
        weights.append(weights_of(l, carried))
        carried = fwd_rides(l) if fwd_rides else {}
        x, sv = layer_fwd(x, mem, weights[l], f'l{l}_', carried)
        saved.append(sv)
    loss, dx, g_nf = loss_head(x, tgt, norm_f, min(x.shape[0], 256), 'loss_head')
    grads, rides = [None] * n_layers, {}
    for l in reversed(range(n_layers)):
        dx, grads[l] = layer_bwd(dx, mem, weights[l], saved[l], f'l{l}_', rides)
        rides = bwd_rides(l, grads[l]) if bwd_rides else {}
    return loss, dx, grads, g_nf, rides


def _row(v):
    return v.reshape(1, -1)


def _lane_pad(v):
    return jnp.zeros((8, 128), F32).at[0, :v.shape[0]].set(v)


def layer_weights(full, l):
    w_in = full['w_in'][l]
    cols = [w_in[:, a:b] for a, b in Z_ORDER]
    cols.append(jnp.zeros((w_in.shape[0], Z_W - sum(b - a for a, b in Z_ORDER)), w_in.dtype))
    dn_w = full['dn_conv_w'][l]
    W = {n: _row(full[n][l]) for n in ('norm_mix', 'conv_a_b', 'ln_a_g', 'ln_a_b', 'dn_norm_g', 'gm_ln_g', 'gm_ln_b',
                                       'pool_scale', 'norm_xa', 'norm_mem', 'norm_mlp')}
    W.update(w_in=jnp.concatenate(cols, axis=1), conv_a_w=full['conv_a_w'][l],
             dn_wq=dn_w[:, :MIX_W], dn_wk=dn_w[:, MIX_W:2 * MIX_W], dn_wv=dn_w[:, 2 * MIX_W:],
             alp=_lane_pad(full['dn_a_log'][l]), dtp=_lane_pad(full['dn_dt_bias'][l]),
             gm_ws=full['gm_ws'][l], gm_b=[full['gm_bs'][l][g].reshape(GM_CHUNK, 1) for g in range(GM_GROUPS)],
             pool_w=full['pool_w'][l])
    for n in ('w_branch', 'w_out', 'xa_wq', 'xa_wkv', 'xa_wo', 'mlp_w1', 'mlp_w2'):
        W[n] = full[n][l]
    return W


def layer_grads(G):
    g_in = G['w_in']
    starts = np.cumsum([0] + [b - a for a, b in Z_ORDER])
    pieces = sorted(zip(Z_ORDER, starts[:-1]))
    out = {n: G[n].reshape(-1) for n in ('norm_mix', 'conv_a_b', 'ln_a_g', 'ln_a_b', 'dn_norm_g', 'gm_ln_g', 'gm_ln_b',
                                          'pool_scale', 'norm_xa', 'norm_mem', 'norm_mlp')}
    out.update(w_in=jnp.concatenate([g_in[:, s:s + b - a] for (a, b), s in pieces], axis=1),
               conv_a_w=G['conv_a_w'], dn_conv_w=jnp.concatenate([G['dn_wq'], G['dn_wk'], G['dn_wv']], axis=1),
               dn_a_log=G['alp'][0, :DN_HEADS], dn_dt_bias=G['dtp'][0, :DN_HEADS], gm_ws=G['gm_ws'],
               gm_bs=jnp.stack([b.reshape(-1) for b in G['gm_b']]), pool_w=G['pool_w'])
    for n in ('w_branch', 'w_out', 'xa_wq', 'xa_wkv', 'xa_wo', 'mlp_w1', 'mlp_w2'):
        out[n] = G[n]
    return out


COLUMN_SHARDED = ('w_in', 'conv_a_w', 'dn_conv_w', 'w_branch', 'xa_wkv', 'mlp_w1')


def _unshard(name, g):
    if name in COLUMN_SHARDED:
        t = jnp.moveaxis(g, 0, -2)
        return t.reshape(t.shape[:-2] + (t.shape[-2] * t.shape[-1],))
    t = jnp.moveaxis(g, 0, 1)
    return t.reshape((t.shape[0], t.shape[1] * t.shape[2]) + t.shape[3:])


def _shard_rows(name, g):
    if name in COLUMN_SHARDED:
        t = g.reshape(g.shape[:-1] + (8, g.shape[-1] // 8))
        return jnp.moveaxis(t, -2, 0)
    t = g.reshape((g.shape[0], 8, g.shape[1] // 8) + g.shape[2:])
    return jnp.moveaxis(t, 1, 0)


def _packed_rows(a):
    return -(-a.size // 1024) * 8


def _pack(arrays):
    tiles = [jnp.pad(a.reshape(-1), (0, _packed_rows(a) * 128 - a.size)).reshape(-1, 128) for a in arrays]
    return jnp.concatenate(tiles, axis=0)


def _unpack(packed, like):
    out, at = [], 0
    for a in like:
        rows = _packed_rows(a)
        out.append(packed[at:at + rows].reshape(-1)[:a.size].reshape(a.shape))
        at += rows
    return out


def kernel(x, mem, norm_mix, w_in, conv_a_w, conv_a_b, ln_a_g, ln_a_b, dn_conv_w, dn_a_log, dn_dt_bias, dn_norm_g, gm_ln_g, gm_ln_b, gm_ws, gm_bs, pool_w, pool_scale, w_branch, w_out, norm_xa, norm_mem, xa_wq, xa_wkv, xa_wo, norm_mlp, mlp_w1, mlp_w2, norm_f, loss_target, m_norm_mix, m_w_in, m_conv_a_w, m_conv_a_b, m_ln_a_g, m_ln_a_b, m_dn_conv_w, m_dn_a_log, m_dn_dt_bias, m_dn_norm_g, m_gm_ln_g, m_gm_ln_b, m_gm_ws, m_gm_bs, m_pool_w, m_pool_scale, m_w_branch, m_w_out, m_norm_xa, m_norm_mem, m_xa_wq, m_xa_wkv, m_xa_wo, m_norm_mlp, m_mlp_w1, m_mlp_w2, m_norm_f, v_norm_mix, v_w_in, v_conv_a_w, v_conv_a_b, v_ln_a_g, v_ln_a_b, v_dn_conv_w, v_dn_a_log, v_dn_dt_bias, v_dn_norm_g, v_gm_ln_g, v_gm_ln_b, v_gm_ws, v_gm_bs, v_pool_w, v_pool_scale, v_w_branch, v_w_out, v_norm_xa, v_norm_mem, v_xa_wq, v_xa_wkv, v_xa_wo, v_norm_mlp, v_mlp_w1, v_mlp_w2, v_norm_f):
    args = locals()
    w = {n: args[n] for n in WEIGHTS}
    m = {n: args['m_' + n] for n in WEIGHTS}
    v = {n: args['v_' + n] for n in WEIGHTS}

    px, py, pc = _place()
    me = 4 * px + 2 * py + pc

    fwd_share = {'w_in': ['w_in'], 'mlp_w1': ['mlp_w1', 'mlp_w2'],
                 'delta_prep': [n for n in SHARDED if n not in ('w_in', 'mlp_w1', 'mlp_w2')]}
    bwd_share = {'xattn': ['w_in', 'mlp_w1'], 'merge': [n for n in SHARDED if n not in ('w_in', 'mlp_w1')]}
    assert set(fwd_share) <= set(FWD_HOSTS) and set(bwd_share) <= set(BWD_HOSTS)

    def shards_of(l):
        return {n: w[n][l].astype(BF16) if n in SENT_AS_BF16 else w[n][l] for n in SHARDED}

    def fwd_rides(l):
        if l + 1 == DEPTH:
            return {}
        shards = shards_of(l + 1)
        return {host: gather_ride([shards[n] for n in names]) for host, names in fwd_share.items()}

    def weights_of(l, carried):
        shards = shards_of(l)
        if carried:
            gathered = {n: g for host, names in fwd_share.items() for n, g in zip(names, carried[host].results)}
        else:
            gathered = dict(zip(SHARDED, all_gather([shards[n] for n in SHARDED], 'gather_weights')))
        full = {n: w[n][l:l + 1] for n in REPLICATED if n != 'norm_f'}
        for n in SHARDED:
            full[n] = _unshard(n, lax.dynamic_update_index_in_dim(gathered[n], shards[n], me, 0)[:, None])
        return layer_weights(full, 0)

    per_layer, sums, chips = [None] * DEPTH, [None] * DEPTH, [None] * DEPTH

    def bwd_rides(l, G):
        per_layer[l] = layer_grads(G)
        rows = [_as2d(_shard_rows(n, per_layer[l][n][None]), 1).astype(BF16 if n in SENT_AS_BF16 else F32)
                for n in SHARDED]
        from_sibling = exchange_cores(rows, f'reduce_cores_l{l}')
        sums[l] = {n: add_own_rows(g, pc, r, f'reduce_add_l{l}_{n}') for n, g, r in zip(SHARDED, rows, from_sibling)}
        chips[l] = {host: chips_ride([sums[l][n] for n in names]) for host, names in bwd_share.items()}
        return chips[l]

    loss, grad_x, grads, g_nf, last = local_step(x[0], mem[0], loss_target[0], _row(norm_f), DEPTH, weights_of,
                                                 fwd_rides, bwd_rides)
    for host, ride in last.items():
        run_alone(ride, 'reduce_chips_l0_' + host)
    from_chips = [{n: r for host, names in bwd_share.items() for n, r in zip(names, chips[l][host].results)}
                  for l in range(DEPTH)]
    gfull = {n: jnp.stack([g[n] for g in per_layer]) for n in REPLICATED if n != 'norm_f'}
    gfull['norm_f'] = g_nf.reshape(-1)

    out = {}
    for n in SHARDED:
        own = jnp.concatenate([lax.dynamic_index_in_dim(sums[l][n], 2 * px + py, 0) for l in range(DEPTH)], axis=1)
        parts = jnp.concatenate([from_chips[l][n] for l in range(DEPTH)], axis=1)
        res = adamw_sum(parts, _as2d(w[n]), _as2d(m[n]), _as2d(v[n]), 'adamw_' + n, own=own, own_row=0)
        out[n] = [r.reshape(w[n].shape) for r in res]
    packed = _pack([gfull[n] for n in REPLICATED])
    (partials,) = all_gather([packed], 'gather_small_grads')
    partials = lax.dynamic_update_index_in_dim(partials, packed, 4 * px + 2 * py + pc, 0)
    res = adamw_sum(partials, _pack([w[n] for n in REPLICATED]), _pack([m[n] for n in REPLICATED]),
                    _pack([v[n] for n in REPLICATED]), 'adamw_small')
    like = [w[n] for n in REPLICATED]
    for k, r in enumerate(res):
        for n, a in zip(REPLICATED, _unpack(r, like)):
            out.setdefault(n, [None] * 4)[k] = a

    total = lax.psum(loss[0, 0], ('x', 'y', 'c'))
    return (total, grad_x[None], *[out[n][0] for n in WEIGHTS], *[out[n][1] for n in WEIGHTS],
            *[out[n][2] for n in WEIGHTS], *[out[n][3] for n in WEIGHTS])
```

```python
import functools
import math

import numpy as np
import jax
import jax.numpy as jnp
from jax import lax
from jax.experimental import pallas as pl
from jax.experimental.pallas import tpu as pltpu

F32 = jnp.float32
BF16 = jnp.bfloat16
HIGH = lax.Precision.HIGH
MESH_ID = pl.DeviceIdType.MESH
VMEM_LIMIT_V7X = 56 << 20

DEPTH = 4
MIX_W = 512
DN_HEADS = 4
DN_HEAD_DIM = 128
DN_CHUNK = 64
GM_CHUNK = 128
GM_GROUPS = 4
POOL_WINDOWS = (2, 4, 8, 16)
XA_HEADS = 4
CONV_PAD = 32

ADAM_LR = 0.001
ADAM_B1 = 0.9
ADAM_B2 = 0.999
ADAM_EPS = 1e-08
ADAM_WD = 0.01
ADAM_STEP = 10

WEIGHTS = ['norm_mix', 'w_in', 'conv_a_w', 'conv_a_b', 'ln_a_g', 'ln_a_b', 'dn_conv_w', 'dn_a_log', 'dn_dt_bias',
           'dn_norm_g', 'gm_ln_g', 'gm_ln_b', 'gm_ws', 'gm_bs', 'pool_w', 'pool_scale', 'w_branch', 'w_out',
           'norm_xa', 'norm_mem', 'xa_wq', 'xa_wkv', 'xa_wo', 'norm_mlp', 'mlp_w1', 'mlp_w2', 'norm_f']
SHARDED = ['w_in', 'conv_a_w', 'dn_conv_w', 'w_branch', 'w_out', 'xa_wq', 'xa_wkv', 'xa_wo', 'mlp_w1', 'mlp_w2']
SENT_AS_BF16 = ['w_in', 'w_branch', 'w_out', 'xa_wq', 'xa_wkv', 'xa_wo', 'mlp_w1', 'mlp_w2']
REPLICATED = [n for n in WEIGHTS if n not in SHARDED]

Z_W = 8832
Z_ORDER = ((4616, 8712), (0, 3072), (3080, 4616), (3072, 3080))
ZB_GATE, ZB_A1, ZB_A2, ZB_Q, ZB_K, ZB_V, ZB_DGATE, ZB_GU, ZB_GV, ZB_POOL = 0, 8, 9, 10, 11, 12, 13, 14, 15, 16
ZB128_BD = 68


def _cparams(*sem):
    return pltpu.CompilerParams(dimension_semantics=sem, vmem_limit_bytes=VMEM_LIMIT_V7X)


def _dot(a, b, ca, cb):
    return lax.dot_general(a.astype(BF16), b.astype(BF16), (((ca,), (cb,)), ((), ())), preferred_element_type=F32)


def _doth(a, b, ca, cb):
    return lax.dot_general(a, b, (((ca,), (cb,)), ((), ())), precision=HIGH, preferred_element_type=F32)


def _make_mm(dot):
    @jax.custom_vjp
    def nn(a, b):
        return dot(a, b, 1, 0)
    nn.defvjp(lambda a, b: (dot(a, b, 1, 0), (a, b)), lambda r, g: (dot(g, r[1], 1, 1), dot(r[0], g, 0, 0)))

    @jax.custom_vjp
    def nt(a, b):
        return dot(a, b, 1, 1)
    nt.defvjp(lambda a, b: (dot(a, b, 1, 1), (a, b)), lambda r, g: (dot(g, r[1], 1, 0), dot(g, r[0], 0, 0)))

    @jax.custom_vjp
    def tn(a, b):
        return dot(a, b, 0, 0)
    tn.defvjp(lambda a, b: (dot(a, b, 0, 0), (a, b)), lambda r, g: (dot(r[1], g, 1, 1), dot(r[0], g, 1, 0)))
    return nn, nt, tn


mm, mm_nt, mm_tn = _make_mm(_dot)
mmh, mmh_nt, mmh_tn = _make_mm(_doth)


@jax.custom_vjp
def _mmw(a, w, wz):
    return _dot(a, w, 1, 0)


_mmw.defvjp(lambda a, w, wz: (_dot(a, w, 1, 0), (a, w)),
            lambda r, g: (_dot(g, r[1], 1, 1), jnp.zeros_like(r[1]), _dot(r[0], g, 0, 0)))


def mmw(a, wpair):
    w, wz = wpair
    return _dot(a, w, 1, 0) if wz is None else _mmw(a, w, wz)


def wsel(wpair, n):
    return (wpair[0][n], None if wpair[1] is None else wpair[1][n])


def _sigmoid(x):
    return 1.0 / (1.0 + jnp.exp(-x))


def _silu(x):
    return x * _sigmoid(x)


def _rms(x, g, eps=1e-6):
    return x * lax.rsqrt(jnp.mean(x * x, axis=-1, keepdims=True) + eps) * g


def _ln(x, g, b, eps=1e-5):
    mu = jnp.mean(x, axis=-1, keepdims=True)
    d = x - mu
    return d * lax.rsqrt(jnp.mean(d * d, axis=-1, keepdims=True) + eps) * g + b


def _gelu(x):
    return 0.5 * x * (1.0 + lax.erf(x * (2.0 ** -0.5)))


def _softplus(x):
    return jnp.maximum(x, 0.0) + jnp.log(1.0 + jnp.exp(-jnp.abs(x)))


def _row_spec(T, width, cb):
    return pl.BlockSpec((T, width), lambda i: (i, cb))


def _whole_spec(p):
    nd = p.ndim
    return pl.BlockSpec(p.shape, lambda i: (0,) * nd)


def _load_params(refs, kinds, with_zeros):
    out = []
    for r, k in zip(refs, kinds):
        if k == 'w':
            out.append((r[...], jnp.zeros(r.shape, F32) if with_zeros else None))
        else:
            out.append(r[...].astype(F32))
    return out


def rows_fwd(f, rows, params, kinds, outs, T, name, ride=None):
    S = rows[0][0].shape[0]
    nr, npar = len(rows), len(params)

    def body(*refs):
        r = [x[...].astype(F32) for x in refs[:nr]]
        p = _load_params(refs[nr:nr + npar], kinds, False)
        res = f(*r, *p)
        for o_ref, o in zip(refs[nr + npar:], res):
            o_ref[...] = o.astype(o_ref.dtype)

    return call_kernel(
        body, (S // T,), [_row_spec(T, w, cb) for _, w, cb in rows] + [_whole_spec(p) for p in params],
        [_row_spec(T, w, 0) for w, _ in outs], [jax.ShapeDtypeStruct((S, w), dt) for w, dt in outs],
        [a for a, _, _ in rows] + list(params), name, ("parallel",), ride=ride)


def rows_bwd(f, rows, params, kinds, cts, row_dtypes, T, name, ride=None):
    S = rows[0][0].shape[0]
    nr, npar, nc = len(rows), len(params), len(cts)
    want = [i for i, dt in enumerate(row_dtypes) if dt is not None]

    def body(*refs):
        r = [x[...].astype(F32) for x in refs[:nr]]
        p = _load_params(refs[nr:nr + npar], kinds, True)
        g = [x[...].astype(F32) for x in refs[nr + npar:nr + npar + nc]]
        d_rows = refs[nr + npar + nc:nr + npar + nc + len(want)]
        d_params = refs[nr + npar + nc + len(want):]
        _, vjp = jax.vjp(f, *r, *p)
        grads = vjp(tuple(g))
        for o_ref, i in zip(d_rows, want):
            o_ref[...] = grads[i].astype(o_ref.dtype)
        first = pl.program_id(0) == 0
        for o_ref, gp, k in zip(d_params, grads[nr:], kinds):
            gp = gp[1] if k == 'w' else gp

            @pl.when(first)
            def _():
                o_ref[...] = gp

            @pl.when(jnp.logical_not(first))
            def _():
                o_ref[...] += gp

    res = call_kernel(
        body, (S // T,),
        ([_row_spec(T, w, cb) for _, w, cb in rows] + [_whole_spec(p) for p in params]
         + [_row_spec(T, c.shape[1], 0) for c in cts]),
        [_row_spec(T, rows[i][1], 0) for i in want] + [_whole_spec(p) for p in params],
        ([jax.ShapeDtypeStruct((S, rows[i][1]), row_dtypes[i]) for i in want]
         + [jax.ShapeDtypeStruct(p.shape, F32) for p in params]),
        [a for a, _, _ in rows] + list(params) + list(cts), name, ("arbitrary",), ride=ride)
    return res[:len(want)], res[len(want):]


def f_norm(x, g):
    return (_rms(x, g),)


def f_glu(a1, a2):
    return (a1 * _sigmoid(a2),)


def f_lnsilu(cv, cb, g, b):
    return (_silu(_ln(cv + cb, g, b)),)


def f_gbeta(bd, alp, dtp):
    j = lax.broadcasted_iota(jnp.int32, (128, MIX_W), 0)
    head = lax.broadcasted_iota(jnp.int32, (128, MIX_W), 1) // DN_HEAD_DIM
    e_lo = (j == head).astype(F32)
    e_hi = (j == head + DN_HEADS).astype(F32)
    beta = _sigmoid(mmh(bd, e_lo))
    a_log = jnp.sum(mmh(alp, e_lo), axis=0, keepdims=True)
    dt_bias = jnp.sum(mmh(dtp, e_lo), axis=0, keepdims=True)
    g = -jnp.exp(a_log) * _softplus(mmh(bd, e_hi) + dt_bias)
    return g, beta


def f_gmlp(u_in, v_in, lg, lb, ws, b0, b1, b2, b3):
    T = u_in.shape[0]
    u = _gelu(u_in)
    vg = _ln(_gelu(v_in), lg, lb)
    tril = (lax.broadcasted_iota(jnp.int32, (GM_CHUNK, GM_CHUNK), 0)
            >= lax.broadcasted_iota(jnp.int32, (GM_CHUNK, GM_CHUNK), 1))
    bias = (b0, b1, b2, b3)
    n = T // GM_CHUNK
    w = [jnp.where(tril, ws[gi], 0.0) for gi in range(GM_GROUPS)]
    mixed = [[mm(w[gi], vg[r * GM_CHUNK:(r + 1) * GM_CHUNK, gi * 128:(gi + 1) * 128]) for gi in range(GM_GROUPS)]
             for r in range(n)]
    chunks = [jnp.concatenate([mixed[r][gi] + bias[gi] for gi in range(GM_GROUPS)], axis=1) for r in range(n)]
    return (u * (chunks[0] if n == 1 else jnp.concatenate(chunks, axis=0)),)


def f_pool(cs, xin, pw, scale):
    T = cs.shape[0]
    t = pl.program_id(0) * T + lax.broadcasted_iota(jnp.int32, (T, 128), 0)
    pooled = [cs[:, gi * 128:(gi + 1) * 128] / jnp.minimum(t + 1, win).astype(F32) - xin[:, gi * 128:(gi + 1) * 128]
              for gi, win in enumerate(POOL_WINDOWS)]
    cols = [mm(pooled[gi], pw[gi]) for gi in range(len(POOL_WINDOWS))]
    return (jnp.concatenate(cols, axis=1) * scale,)


def f_merge(a, o, c, p, gate, wb):
    D = gate.shape[1] // 4
    proj = [mmw(br, wsel(wb, n)) for n, br in enumerate((a, o, c, p))]
    terms = [_sigmoid(gate[:, n * D:(n + 1) * D]) * proj[n] for n in range(4)]
    return (terms[0] + terms[1] + terms[2] + terms[3],)


def f_kv(mem, nm, wkv):
    return (mmw(_rms(mem, nm), wkv),)


def f_xattn(x, kv, nx, wq, wo):
    D = x.shape[1]
    hd = D // XA_HEADS
    q = mmw(_rms(x, nx), wq)
    hs = range(XA_HEADS)
    s = [mm_nt(q[:, h * hd:(h + 1) * hd], kv[:, h * hd:(h + 1) * hd]) * (hd ** -0.5) for h in hs]
    e = [jnp.exp(t - jnp.max(t, axis=-1, keepdims=True)) for t in s]
    pr = [t / jnp.sum(t, axis=-1, keepdims=True) for t in e]
    heads = [mm(pr[h], kv[:, D + h * hd:D + (h + 1) * hd]) for h in hs]
    return (x + mmw(jnp.concatenate(heads, axis=1), wo),)


def f_loss(x, tgt, nf):
    err = _rms(x, nf) - tgt
    return (0.5 * jnp.sum(jnp.mean(err * err, axis=-1, keepdims=True), axis=0, keepdims=True),)


def _mm_tiles(mode, M, N, K):
    wide = 512 if N % 512 == 0 else 384
    if mode == 'tn':
        return min(M, 512), wide, K
    tm = min(M, 1024)
    if K <= 1024:
        if N % 512 == 0:
            return tm, wide, K
        return min(M, 512), N // 3, K
    if K % 2048 == 0:
        return tm, min(N, 1024) if mode == 'nt' else wide, 2048
    return tm, min(N, 1024), K // 3


def matmul(a, b, mode, *, name, out_dtype=F32, res=None, act=None, gate=None, ride=None):
    M, K = a.shape if mode != 'tn' else a.shape[::-1]
    tm, tn, tk = _mm_tiles(mode, M, b.shape[0] if mode == 'nt' else b.shape[1], K)
    N = b.shape[0] if mode == 'nt' else b.shape[1]
    assert M % tm == 0 and N % tn == 0 and K % tk == 0, (a.shape, b.shape, mode, tm, tn, tk)
    nk = K // tk
    size = lambda t: t.size * t.dtype.itemsize
    rows_outer = size(a) + (M // tm) * size(b) <= size(b) + (N // tn) * size(a)

    def spec(shape, index):
        if rows_outer:
            return pl.BlockSpec(shape, lambda i, j, k: index(i, j, k))
        return pl.BlockSpec(shape, lambda j, i, k: index(i, j, k))

    if mode == 'nn':
        a_spec, b_spec = spec((tm, tk), lambda i, j, k: (i, k)), spec((tk, tn), lambda i, j, k: (k, j))
        ca, cb = 1, 0
    elif mode == 'nt':
        a_spec, b_spec = spec((tm, tk), lambda i, j, k: (i, k)), spec((tn, tk), lambda i, j, k: (j, k))
        ca, cb = 1, 1
    else:
        a_spec, b_spec = spec((tk, tm), lambda i, j, k: (k, i)), spec((tk, tn), lambda i, j, k: (k, j))
        ca, cb = 0, 0
    o_spec = spec((tm, tn), lambda i, j, k: (i, j))
    extra = [e for e in (res, gate) if e is not None]

    def body(*refs):
        a_ref, b_ref = refs[:2]
        e_refs = refs[2:2 + len(extra)]
        o_refs = refs[2 + len(extra):2 + len(extra) + (2 if act else 1)]
        part = _dot(a_ref[...], b_ref[...], ca, cb)

        def finish(acc):
            if res is not None:
                acc = acc + e_refs[0][...]
            if gate is not None:
                acc = acc * (2.0 * jnp.maximum(e_refs[-1][...], 0.0))
            o_refs[0][...] = acc.astype(o_refs[0].dtype)
            if act:
                r = jnp.maximum(acc, 0.0)
                o_refs[1][...] = (r * r).astype(o_refs[1].dtype)

        if nk == 1:
            finish(part)
        else:
            acc_ref = refs[-1]
            k = pl.program_id(2)

            @pl.when(k == 0)
            def _():
                acc_ref[...] = part

            @pl.when(k > 0)
            def _():
                acc_ref[...] += part

            @pl.when(k == nk - 1)
            def _():
                finish(acc_ref[...])

    out_shape = [jax.ShapeDtypeStruct((M, N), out_dtype)]
    if act:
        out_shape.append(jax.ShapeDtypeStruct((M, N), BF16))
    res_ = call_kernel(
        body, (M // tm, N // tn, nk) if rows_outer else (N // tn, M // tm, nk),
        [a_spec, b_spec] + [o_spec] * len(extra), [o_spec] * len(out_shape), out_shape, [a, b, *extra], name,
        ("parallel", "parallel", "arbitrary"), scratch=[pltpu.VMEM((tm, tn), F32)] if nk > 1 else [], ride=ride)
    return res_ if act else res_[0]


def _conv_rows(S):
    return min(S, 512)


def conv_fwd(x, cb0, w, name):
    S = x.shape[0]
    K = w.shape[0]
    R = _conv_rows(S)

    def body(x_ref, w_ref, y_ref, pad_ref):
        pad_ref[pl.ds(0, CONV_PAD), :] = jnp.zeros((CONV_PAD, 128), F32)
        pad_ref[pl.ds(CONV_PAD, S), :] = x_ref[...]
        wv = w_ref[...]

        def chunk(r, carry):
            r0 = pl.multiple_of(r * R, R)
            win = pad_ref[pl.ds(r0, R + CONV_PAD), :]
            acc = jnp.zeros((R, 128), F32)
            for s in range(K):
                sh = win if s == 0 else pltpu.roll(win, s, 0)
                acc = acc + sh[CONV_PAD:, :] * wv[K - 1 - s:K - s, :]
            y_ref[pl.ds(r0, R), :] = acc
            return carry

        lax.fori_loop(0, S // R, chunk, 0)

    return pl.pallas_call(
        body, grid=(4,), name=name,
        in_specs=[pl.BlockSpec((S, 128), lambda j: (0, cb0 * 4 + j)), pl.BlockSpec((K, 128), lambda j: (0, j))],
        out_specs=pl.BlockSpec((S, 128), lambda j: (0, j)),
        out_shape=jax.ShapeDtypeStruct((S, MIX_W), F32),
        scratch_shapes=[pltpu.VMEM((S + CONV_PAD, 128), F32)],
        compiler_params=_cparams("parallel"),
    )(x, w)


def conv_bwd(x, cb0, w, dy, name, add=None, out_dtype=F32):
    S = x.shape[0]
    K = w.shape[0]
    R = _conv_rows(S)
    W = R + CONV_PAD

    def body(*refs):
        x_ref, w_ref, dy_ref = refs[:3]
        add_ref = refs[3] if add is not None else None
        dx_ref, dw_ref, xpad_ref, dypad_ref = refs[-4:]
        xpad_ref[pl.ds(0, CONV_PAD), :] = jnp.zeros((CONV_PAD, 128), F32)
        xpad_ref[pl.ds(CONV_PAD, S), :] = x_ref[...]
        dypad_ref[pl.ds(S, CONV_PAD), :] = jnp.zeros((CONV_PAD, 128), F32)
        dypad_ref[pl.ds(0, S), :] = dy_ref[...].astype(F32)
        dw_ref[...] = jnp.zeros((K, 128), F32)
        wv = w_ref[...]

        def chunk(r, carry):
            r0 = pl.multiple_of(r * R, R)
            xwin = xpad_ref[pl.ds(r0, W), :]
            dwin = dypad_ref[pl.ds(r0, W), :]
            dyc = dwin[:R, :]
            acc = jnp.zeros((R, 128), F32)
            for s in range(K):
                up = dwin if s == 0 else pltpu.roll(dwin, W - s, 0)
                acc = acc + up[:R, :] * wv[K - 1 - s:K - s, :]
                xs = xwin if s == 0 else pltpu.roll(xwin, s, 0)
                dw_ref[pl.ds(K - 1 - s, 1), :] += jnp.sum(dyc * xs[CONV_PAD:, :], axis=0, keepdims=True)
            if add_ref is not None:
                acc = acc + add_ref[pl.ds(r0, R), :].astype(F32)
            dx_ref[pl.ds(r0, R), :] = acc.astype(dx_ref.dtype)
            return carry

        lax.fori_loop(0, S // R, chunk, 0)

    col = pl.BlockSpec((S, 128), lambda j: (0, j))
    ins = [x, w, dy] + ([add] if add is not None else [])
    return pl.pallas_call(
        body, grid=(4,), name=name,
        in_specs=[pl.BlockSpec((S, 128), lambda j: (0, cb0 * 4 + j)), pl.BlockSpec((K, 128), lambda j: (0, j)), col]
        + ([col] if add is not None else []),
        out_specs=[col, pl.BlockSpec((K, 128), lambda j: (0, j))],
        out_shape=[jax.ShapeDtypeStruct((S, MIX_W), out_dtype), jax.ShapeDtypeStruct((K, MIX_W), F32)],
        scratch_shapes=[pltpu.VMEM((S + CONV_PAD, 128), F32), pltpu.VMEM((S + CONV_PAD, 128), F32)],
        compiler_params=_cparams("parallel"),
    )(*ins)


DN_PREP_ROWS = 2 * DN_CHUNK


def _unit_lower_inverses(mats):
    C = mats[0].shape[0]
    eye = (lax.broadcasted_iota(jnp.int32, (C, C), 0) == lax.broadcasted_iota(jnp.int32, (C, C), 1)).astype(F32)
    invs = [eye - a for a in mats]
    pws = [mmh(a, a) for a in mats]
    for _ in range(5):
        both = [mmh(jnp.concatenate([inv, pw], axis=0), pw) for inv, pw in zip(invs, pws)]
        invs = [inv + b[:C] for inv, b in zip(invs, both)]
        pws = [b[C:] for b in both]
    return invs


@jax.custom_vjp
def _known_inverse(a, inv):
    return inv


_known_inverse.defvjp(lambda a, inv: (inv, inv),
                      lambda inv, g: (-mmh_nt(mmh_tn(inv, g), inv), jnp.zeros_like(inv)))


def _delta_prep(qc, kc, vc, ge, be, inv_known=None):
    C, Dh = DN_CHUNK, DN_HEAD_DIM
    n = qc.shape[0] // C
    ii = lax.broadcasted_iota(jnp.int32, (C, C), 0)
    jj = lax.broadcasted_iota(jnp.int32, (C, C), 1)
    causal, strict = ii >= jj, ii > jj
    sum_lhs = jnp.concatenate([causal.astype(F32), jnp.ones((C, C), F32)], axis=0)
    mean_lanes = jnp.full((C, Dh), 1.0 / Dh, F32)
    pairs = [(r, h) for r in range(n) for h in range(DN_HEADS)]
    pick = lambda t, w: [t[r * C:(r + 1) * C, h * w:(h + 1) * w] for r, h in pairs]
    q, k, v = [[_silu(t) for t in pick(b, Dh)] for b in (qc, kc, vc)]
    q = [t * lax.rsqrt(jnp.sum(t * t, axis=-1, keepdims=True) + 1e-6) * (Dh ** -0.5) for t in q]
    k = [t * lax.rsqrt(jnp.sum(t * t, axis=-1, keepdims=True) + 1e-6) for t in k]
    beta = pick(be, Dh)
    sums = [mmh(sum_lhs, g) for g in pick(ge, Dh)]
    gam, g_last = [s[:C] for s in sums], [s[C:] for s in sums]
    gam_row = [mmh_nt(mean_lanes, t) for t in gam]
    decay = [jnp.where(causal, jnp.exp(jnp.where(causal, gc[:, :C] - gr, 0.0)), 0.0) for gc, gr in zip(gam, gam_row)]
    kb = [a * b for a, b in zip(k, beta)]
    scores = [mm_nt(jnp.concatenate([a, b], axis=0), c) for a, b, c in zip(kb, q, k)]
    a = [jnp.where(strict, s[:C] * d, 0.0) for s, d in zip(scores, decay)]
    if inv_known is None:
        inv = _unit_lower_inverses(a)
    else:
        inv = [_known_inverse(m, inv_known[h][r * C:(r + 1) * C]) for m, (r, h) in zip(a, pairs)]
    e_gam = [jnp.exp(t) for t in gam]
    uw = [mmh(i, jnp.concatenate([vv * b, kk * e], axis=1)) for i, vv, b, kk, e in zip(inv, v, beta, kb, e_gam)]

    def wide(parts):
        rows = [jnp.concatenate(parts[r * DN_HEADS:(r + 1) * DN_HEADS], axis=1) for r in range(n)]
        return rows[0] if n == 1 else jnp.concatenate(rows, axis=0)

    def narrow(parts):
        per_head = [[parts[r * DN_HEADS + h] for r in range(n)] for h in range(DN_HEADS)]
        return [p[0] if n == 1 else jnp.concatenate(p, axis=0) for p in per_head]

    return (wide([t[:, :Dh] for t in uw]), wide([t[:, Dh:] for t in uw]), wide([a_ * b_ for a_, b_ in zip(q, e_gam)]),
            wide([kk * jnp.exp(gl - gm) for kk, gl, gm in zip(k, g_last, gam)]), wide([jnp.exp(t) for t in g_last]),
            narrow([s[C:] * d for s, d in zip(scores, decay)]), narrow(inv))


def _delta_step(states, u, w, qd, kd, sc, attns, gate, ng):
    C, Dh = DN_CHUNK, DN_HEAD_DIM
    heads = range(DN_HEADS)
    cut = lambda t: [t[:, h * Dh:(h + 1) * Dh] for h in heads]
    u, w, qd, kd, sc, gate = cut(u), cut(w), cut(qd), cut(kd), cut(sc), cut(gate)
    on_state = [mm(jnp.concatenate([w[h], qd[h]], axis=0), states[h]) for h in heads]
    v_new = [u[h] - on_state[h][:C] for h in heads]
    o = [on_state[h][C:] + mm(attns[h], v_new[h]) for h in heads]
    new_states = [states[h] * jnp.concatenate([sc[h], sc[h]], axis=0) + mm_tn(kd[h], v_new[h]) for h in heads]
    outs = [_rms(o[h], ng) * _silu(gate[h]) for h in heads]
    return new_states, jnp.concatenate(outs, axis=1)


def call_kernel(body, grid, in_specs, out_specs, out_shape, operands, name, semantics, scratch=(), ride=None):
    if ride is None:
        return pl.pallas_call(body, grid=grid, name=name, in_specs=list(in_specs), out_specs=list(out_specs),
                              out_shape=list(out_shape), scratch_shapes=list(scratch),
                              compiler_params=_cparams(*semantics))(*operands)
    n_out = len(out_specs)
    res = pl.pallas_call(
        carry(body, len(in_specs), n_out, len(scratch), ride, grid), grid=grid, name=name,
        in_specs=list(in_specs) + [ANY] * len(ride.inputs), out_specs=list(out_specs) + [ANY] * len(ride.out_shape),
        out_shape=list(out_shape) + ride.out_shape, scratch_shapes=list(scratch) + ride.scratch,
        compiler_params=_cparams(*["arbitrary"] * len(grid)),
    )(*operands, *ride.inputs)
    ride.results = list(res[n_out:])
    return list(res[:n_out])


def delta_prep_fwd(qc, kc, vc, ge, be, name, ride=None):
    S = qc.shape[0]
    R = min(S, DN_PREP_ROWS)

    def body(q_ref, k_ref, v_ref, ge_ref, be_ref, u_ref, w_ref, qd_ref, kd_ref, sc_ref, at_ref, iv_ref):
        u, w, qd, kd, sc, attns, invs = _delta_prep(q_ref[...], k_ref[...], v_ref[...], ge_ref[...], be_ref[...])
        for ref, val in zip((u_ref, w_ref, qd_ref, kd_ref, sc_ref), (u, w, qd, kd, sc)):
            ref[...] = val
        for h in range(DN_HEADS):
            at_ref[h] = attns[h]
            iv_ref[h] = invs[h]

    blk = pl.BlockSpec((R, MIX_W), lambda n: (n, 0))
    hblk = pl.BlockSpec((DN_HEADS, R, DN_CHUNK), lambda n: (0, n, 0))
    wide = jax.ShapeDtypeStruct((S, MIX_W), F32)
    narrow = jax.ShapeDtypeStruct((DN_HEADS, S, DN_CHUNK), F32)
    return call_kernel(body, (S // R,), [blk] * 5, [blk] * 5 + [hblk] * 2, [wide] * 5 + [narrow] * 2,
                       [qc, kc, vc, ge, be], name, ("parallel",), ride=ride)


def delta_prep_bwd(qc, kc, vc, ge, be, inv, cts, name, ride=None):
    S = qc.shape[0]
    R = min(S, DN_PREP_ROWS)

    def body(q_ref, k_ref, v_ref, ge_ref, be_ref, iv_ref, du_ref, dw_ref, dqd_ref, dkd_ref, dsc_ref, dat_ref, *outs):
        invs = [iv_ref[h] for h in range(DN_HEADS)]
        fn = lambda *blocks: _delta_prep(*blocks, invs)[:6]
        _, vjp = jax.vjp(fn, q_ref[...], k_ref[...], v_ref[...], ge_ref[...], be_ref[...])
        grads = vjp((du_ref[...], dw_ref[...], dqd_ref[...], dkd_ref[...], dsc_ref[...],
                     [dat_ref[h] for h in range(DN_HEADS)]))
        for ref, g in zip(outs, grads):
            ref[...] = g

    blk = pl.BlockSpec((R, MIX_W), lambda n: (n, 0))
    hblk = pl.BlockSpec((DN_HEADS, R, DN_CHUNK), lambda n: (0, n, 0))
    return call_kernel(body, (S // R,), [blk] * 5 + [hblk] + [blk] * 5 + [hblk], [blk] * 5,
                       [jax.ShapeDtypeStruct((S, MIX_W), F32)] * 5, [qc, kc, vc, ge, be, inv, *cts], name, ("parallel",),
                       ride=ride)


def delta_step_fwd(prep, z, ng, name):
    S = prep[0].shape[0]
    N = S // DN_CHUNK
    C = DN_CHUNK

    def body(u_ref, w_ref, qd_ref, kd_ref, sc_ref, at_ref, gate_ref, ng_ref, o_ref, st_ref, s_ref):
        @pl.when(pl.program_id(0) == 0)
        def _():
            s_ref[...] = jnp.zeros(s_ref.shape, F32)

        states = [s_ref[h] for h in range(DN_HEADS)]
        for h in range(DN_HEADS):
            st_ref[0, h] = states[h]
        new_states, o = _delta_step(states, u_ref[...], w_ref[...], qd_ref[...], kd_ref[...], sc_ref[...],
                                    [at_ref[h] for h in range(DN_HEADS)], gate_ref[...], ng_ref[...])
        for h in range(DN_HEADS):
            s_ref[h] = new_states[h]
        o_ref[...] = o.astype(o_ref.dtype)

    blk = pl.BlockSpec((C, MIX_W), lambda n: (n, 0))
    return pl.pallas_call(
        body, grid=(N,), name=name,
        in_specs=[blk] * 5 + [pl.BlockSpec((DN_HEADS, C, C), lambda n: (0, n, 0)),
                              pl.BlockSpec((C, MIX_W), lambda n: (n, ZB_DGATE)),
                              pl.BlockSpec((1, DN_HEAD_DIM), lambda n: (0, 0))],
        out_specs=[blk, pl.BlockSpec((1, DN_HEADS, DN_HEAD_DIM, DN_HEAD_DIM), lambda n: (n, 0, 0, 0))],
        out_shape=[jax.ShapeDtypeStruct((S, MIX_W), BF16),
                   jax.ShapeDtypeStruct((N, DN_HEADS, DN_HEAD_DIM, DN_HEAD_DIM), F32)],
        scratch_shapes=[pltpu.VMEM((DN_HEADS, DN_HEAD_DIM, DN_HEAD_DIM), F32)],
        compiler_params=_cparams("arbitrary"),
    )(*prep, z, ng)


def delta_step_bwd(prep, z, ng, states, do, name):
    S = prep[0].shape[0]
    N = S // DN_CHUNK
    C = DN_CHUNK

    def body(u_ref, w_ref, qd_ref, kd_ref, sc_ref, at_ref, gate_ref, ng_ref, st_ref, do_ref,
             du_ref, dw_ref, dqd_ref, dkd_ref, dsc_ref, dat_ref, dgate_ref, dng_ref, ds_ref):
        first = pl.program_id(0) == 0

        @pl.when(first)
        def _():
            ds_ref[...] = jnp.zeros(ds_ref.shape, F32)

        args = ([st_ref[0, h] for h in range(DN_HEADS)], u_ref[...], w_ref[...], qd_ref[...], kd_ref[...],
                sc_ref[...], [at_ref[h] for h in range(DN_HEADS)], gate_ref[...].astype(F32), ng_ref[...])
        _, vjp = jax.vjp(_delta_step, *args)
        d_states, du, dw, dqd, dkd, dsc, dat, dgate, dng = vjp(([ds_ref[h] for h in range(DN_HEADS)],
                                                               do_ref[...].astype(F32)))
        for h in range(DN_HEADS):
            ds_ref[h] = d_states[h]
            dat_ref[h] = dat[h]
        for ref, g in zip((du_ref, dw_ref, dqd_ref, dkd_ref, dsc_ref), (du, dw, dqd, dkd, dsc)):
            ref[...] = g
        dgate_ref[...] = dgate.astype(dgate_ref.dtype)

        @pl.when(first)
        def _():
            dng_ref[...] = dng

        @pl.when(jnp.logical_not(first))
        def _():
            dng_ref[...] += dng

    blk = pl.BlockSpec((C, MIX_W), lambda n: (N - 1 - n, 0))
    hblk = pl.BlockSpec((DN_HEADS, C, C), lambda n: (0, N - 1 - n, 0))
    ngs = pl.BlockSpec((1, DN_HEAD_DIM), lambda n: (0, 0))
    f32o = jax.ShapeDtypeStruct((S, MIX_W), F32)
    return pl.pallas_call(
        body, grid=(N,), name=name,
        in_specs=[blk] * 5 + [hblk, pl.BlockSpec((C, MIX_W), lambda n: (N - 1 - n, ZB_DGATE)), ngs,
                              pl.BlockSpec((1, DN_HEADS, DN_HEAD_DIM, DN_HEAD_DIM), lambda n: (N - 1 - n, 0, 0, 0)),
                              blk],
        out_specs=[blk] * 5 + [hblk, blk, ngs],
        out_shape=[f32o] * 5 + [jax.ShapeDtypeStruct((DN_HEADS, S, C), F32), jax.ShapeDtypeStruct((S, MIX_W), BF16),
                                jax.ShapeDtypeStruct((1, DN_HEAD_DIM), F32)],
        scratch_shapes=[pltpu.VMEM((DN_HEADS, DN_HEAD_DIM, DN_HEAD_DIM), F32)],
        compiler_params=_cparams("arbitrary"),
    )(*prep, z, ng, states, do)


ANY = pl.BlockSpec(memory_space=pl.ANY)


def _place():
    return lax.axis_index("x"), lax.axis_index("y"), lax.axis_index("c")


PHASES = ('start', 'forward', 'finish')


class Ride:
    def __init__(self, inputs, out_shape, scratch, run):
        self.inputs, self.out_shape, self.scratch, self.run = list(inputs), list(out_shape), list(scratch), run
        self.results = None


def run_alone(ride, name):
    n_in, n_out = len(ride.inputs), len(ride.out_shape)

    def body(*refs):
        for phase in PHASES:
            ride.run(phase, refs[:n_in], refs[n_in:n_in + n_out], refs[n_in + n_out:])

    ride.results = list(pl.pallas_call(body, name=name, in_specs=[ANY] * n_in, out_specs=[ANY] * n_out,
                                       out_shape=ride.out_shape, scratch_shapes=ride.scratch)(*ride.inputs))
    return ride.results


def carry(body, n_in, n_out, n_scratch, ride, grid):
    r_in, r_out = len(ride.inputs), len(ride.out_shape)
    steps = math.prod(grid)
    late = (7 * steps) // 8

    def carrying(*refs):
        b = n_in + r_in
        c = b + n_out
        d = c + r_out
        e = d + n_scratch
        ride_refs = (refs[n_in:b], refs[c:d], refs[e:])
        step = 0
        for axis, size in enumerate(grid):
            step = step * size + pl.program_id(axis)

        @pl.when(step == 0)
        def _():
            ride.run('start', *ride_refs)

        @pl.when(step == late)
        def _():
            ride.run('forward', *ride_refs)

        body(*refs[:n_in], *refs[b:c], *refs[d:e])

        @pl.when(step == steps - 1)
        def _():
            ride.run('finish', *ride_refs)

    return carrying


def gather_ride(shards):
    n = len(shards)

    def run(phase, ins, outs, sems):
        send_sems, recv_sems = sems
        x, y, c = _place()
        me, sibling = (x, y, c), (x, y, 1 - c)
        chips = [(1 - x, y), (x, 1 - y), (1 - x, 1 - y)]

        def copy(a, k, block, to, src=None):
            row = 4 * block[0] + 2 * block[1] + block[2]
            return pltpu.make_async_remote_copy(
                src_ref=outs[a].at[row] if src is None else src, dst_ref=outs[a].at[row],
                send_sem=send_sems.at[a, k], recv_sem=recv_sems.at[a, k], device_id=to, device_id_type=MESH_ID)

        def first():
            return [cp for a in range(n) for cp in
                    [copy(a, 0, me, sibling, src=ins[a])]
                    + [copy(a, 1 + j, me, (*chip, c), src=ins[a]) for j, chip in enumerate(chips)]]

        def passed():
            return [copy(a, 4 + j, (*chip, c), sibling) for j, chip in enumerate(chips) for a in range(n)]

        if phase == 'start':
            for cp in first():
                cp.start()
        elif phase == 'forward':
            for cp, (j, chip, a) in zip(passed(), [(j, chip, a) for j, chip in enumerate(chips) for a in range(n)]):
                copy(a, 1 + j, (*chip, c), me).wait_recv()
                cp.start()
        else:
            for a in range(n):
                copy(a, 0, sibling, me).wait_recv()
                for j, chip in enumerate(chips):
                    copy(a, 4 + j, (*chip, 1 - c), me).wait_recv()
            for cp in first() + passed():
                cp.wait_send()

    return Ride(shards, [jax.ShapeDtypeStruct((8,) + s.shape, s.dtype) for s in shards],
                [pltpu.SemaphoreType.DMA((n, 7)), pltpu.SemaphoreType.DMA((n, 7))], run)


def all_gather(shards, name):
    return run_alone(gather_ride(shards), name)


def exchange_cores(grads, name):
    n = len(grads)

    def body(*refs):
        ins, outs = refs[:n], refs[n:2 * n]
        send_sems, recv_sems = refs[2 * n:]
        x, y, c = _place()
        copies = [pltpu.make_async_remote_copy(
            src_ref=ins[a].at[2 * k + 1 - c], dst_ref=outs[a].at[k], send_sem=send_sems.at[a, k],
            recv_sem=recv_sems.at[a, k], device_id=(x, y, 1 - c), device_id_type=MESH_ID)
            for a in range(n) for k in range(4)]
        for cp in copies:
            cp.start()
        for cp in copies:
            cp.wait()

    return pl.pallas_call(
        body, name=name, in_specs=[ANY] * n, out_specs=[ANY] * n,
        out_shape=[jax.ShapeDtypeStruct((4,) + g.shape[1:], g.dtype) for g in grads],
        scratch_shapes=[pltpu.SemaphoreType.DMA((n, 4))] * 2,
    )(*grads)


def chips_ride(parts):
    n = len(parts)

    def run(phase, ins, outs, sems):
        send_sems, recv_sems = sems
        x, y, c = _place()
        chips = [(1 - x, y), (x, 1 - y), (1 - x, 1 - y)]
        if phase == 'forward':
            return
        copies = [pltpu.make_async_remote_copy(
            src_ref=ins[a].at[2 * px + py], dst_ref=outs[a].at[j], send_sem=send_sems.at[a, j],
            recv_sem=recv_sems.at[a, j], device_id=(px, py, c), device_id_type=MESH_ID)
            for a in range(n) for j, (px, py) in enumerate(chips)]
        for cp in copies:
            cp.start() if phase == 'start' else cp.wait()

    return Ride(parts, [jax.ShapeDtypeStruct((3,) + p.shape[1:], p.dtype) for p in parts],
                [pltpu.SemaphoreType.DMA((n, 3))] * 2, run)


def exchange_chips(parts, name):
    return run_alone(chips_ride(parts), name)


def _as2d(a, lead=0):
    return a.reshape(a.shape[:lead] + (-1, a.shape[-1]))


def _row_tile(rows, cols, n_arrays):
    budget = VMEM_LIMIT_V7X // 2
    lanes = -(-cols // 128) * 128
    t = budget // (2 * n_arrays * lanes * 4)
    if t >= rows:
        return rows
    return max(16, t // 16 * 16)


def _index_operand(i):
    return jnp.asarray(i, jnp.int32).reshape(1)


def add_own_rows(grads, core, recv, name):
    _, R, C = grads.shape
    T = _row_tile(R, C, 3)

    def body(c_ref, g_ref, r_ref, o_ref):
        o_ref[...] = (g_ref[...].astype(F32) + r_ref[...].astype(F32)).astype(o_ref.dtype)

    blk = pl.BlockSpec((1, T, C), lambda k, i, c: (k, i, 0))
    return pl.pallas_call(
        body, name=name, out_shape=jax.ShapeDtypeStruct((4, R, C), grads.dtype),
        grid_spec=pltpu.PrefetchScalarGridSpec(
            num_scalar_prefetch=1, grid=(4, pl.cdiv(R, T)),
            in_specs=[pl.BlockSpec((1, T, C), lambda k, i, c: (2 * k + c[0], i, 0)), blk], out_specs=blk),
        compiler_params=_cparams("parallel", "parallel"),
    )(_index_operand(core), grads, recv)


def _adamw(w, g, m, v):
    m = ADAM_B1 * m + (1.0 - ADAM_B1) * g
    v = ADAM_B2 * v + (1.0 - ADAM_B2) * jnp.square(g)
    m_hat = m / (1.0 - ADAM_B1 ** ADAM_STEP)
    v_hat = v / (1.0 - ADAM_B2 ** ADAM_STEP)
    delta = -ADAM_LR * (m_hat / (jnp.sqrt(v_hat) + ADAM_EPS) + ADAM_WD * w)
    return delta, m, v


def adamw_sum(parts, w, m, v, name, own=None, own_row=None):
    P, R, C = parts.shape
    T = _row_tile(R, C, P + 8)
    has_own = own is not None

    def body(i_ref, *refs):
        refs = list(refs)
        own_ref = refs.pop(0) if has_own else None
        p_ref, w_ref, m_ref, v_ref, g_ref, d_ref, nm_ref, nv_ref = refs
        terms = ([own_ref[0]] if has_own else []) + [p_ref[k] for k in range(P)]
        terms = [t.astype(F32) for t in terms]
        g = terms[0]
        for t in terms[1:]:
            g = g + t
        d, nm, nv = _adamw(w_ref[...], g, m_ref[...], v_ref[...])
        g_ref[...] = g
        d_ref[...] = d
        nm_ref[...] = nm
        nv_ref[...] = nv

    blk = pl.BlockSpec((T, C), lambda i, r: (i, 0))
    in_specs = [pl.BlockSpec((P, T, C), lambda i, r: (0, i, 0)), blk, blk, blk]
    operands = [parts, w, m, v]
    if has_own:
        in_specs.insert(0, pl.BlockSpec((1, T, C), lambda i, r: (r[0], i, 0)))
        operands.insert(0, own)
    return pl.pallas_call(
        body, name=name, out_shape=[jax.ShapeDtypeStruct((R, C), F32)] * 4,
        grid_spec=pltpu.PrefetchScalarGridSpec(num_scalar_prefetch=1, grid=(pl.cdiv(R, T),), in_specs=in_specs,
                                               out_specs=[blk] * 4),
        compiler_params=_cparams("parallel"),
    )(_index_operand(0 if own_row is None else own_row), *operands)


def f_norm_res(x, g):
    return _rms(x, g), x


def _pool_taps():
    taps = np.zeros((16, MIX_W), np.float32)
    for gi, win in enumerate(POOL_WINDOWS):
        taps[16 - win:, gi * 128:(gi + 1) * 128] = 1.0
    return jnp.asarray(taps)


def loss_head(x, tgt, nf, T, name):
    S, D = x.shape

    def body(x_ref, t_ref, g_ref, l_ref, dx_ref, dg_ref):
        val, vjp = jax.vjp(f_loss, x_ref[...], t_ref[...], g_ref[...])
        dx, _, dg = vjp((jnp.ones((1, 1), F32),))
        dx_ref[...] = dx
        first = pl.program_id(0) == 0
        lv = jnp.broadcast_to(val[0], (1, 128))

        @pl.when(first)
        def _():
            l_ref[...] = lv
            dg_ref[...] = dg

        @pl.when(jnp.logical_not(first))
        def _():
            l_ref[...] += lv
            dg_ref[...] += dg

    row = pl.BlockSpec((T, D), lambda i: (i, 0))
    return pl.pallas_call(
        body, grid=(S // T,), name=name,
        in_specs=[row, row, pl.BlockSpec((1, D), lambda i: (0, 0))],
        out_specs=[pl.BlockSpec((1, 128), lambda i: (0, 0)), row, pl.BlockSpec((1, D), lambda i: (0, 0))],
        out_shape=[jax.ShapeDtypeStruct((1, 128), F32), jax.ShapeDtypeStruct((S, D), F32),
                   jax.ShapeDtypeStruct((1, D), F32)],
        compiler_params=_cparams("arbitrary"),
    )(x, tgt, nf)


FWD_HOSTS = ('w_in', 'delta_prep', 'mlp_w1')
BWD_HOSTS = ('xattn', 'merge', 'delta_prep')


def layer_fwd(x, mem, W, tag, rides={}):
    S, D = x.shape
    T = min(S, 256)
    sv = {'x': x}
    (h1,) = rows_fwd(f_norm, [(x, D, 0)], [W['norm_mix']], 'p', [(D, BF16)], T, tag + 'norm_mix')
    z = matmul(h1, W['w_in'], 'nn', name=tag + 'w_in', ride=rides.get('w_in'))
    (a_pre,) = rows_fwd(f_glu, [(z, 512, ZB_A1), (z, 512, ZB_A2)], [], '', [(512, F32)], T, tag + 'glu')
    a_cv = conv_fwd(a_pre, 0, W['conv_a_w'], tag + 'conv_a')
    (a,) = rows_fwd(f_lnsilu, [(a_cv, 512, 0)], [W['conv_a_b'], W['ln_a_g'], W['ln_a_b']], 'ppp', [(512, BF16)], T,
                    tag + 'ln_a')
    qc = conv_fwd(z, ZB_Q, W['dn_wq'], tag + 'conv_q')
    kc = conv_fwd(z, ZB_K, W['dn_wk'], tag + 'conv_k')
    vc = conv_fwd(z, ZB_V, W['dn_wv'], tag + 'conv_v')
    ge, be = rows_fwd(f_gbeta, [(z, 128, ZB128_BD)], [W['alp'], W['dtp']], 'pp', [(512, F32), (512, F32)], T,
                      tag + 'gbeta')
    *prep, dn_inv = delta_prep_fwd(qc, kc, vc, ge, be, tag + 'delta_prep', rides.get('delta_prep'))
    o, states = delta_step_fwd(prep, z, W['dn_norm_g'], tag + 'delta')
    gm_params = [W['gm_ln_g'], W['gm_ln_b'], W['gm_ws']] + W['gm_b']
    (c,) = rows_fwd(f_gmlp, [(z, 512, ZB_GU), (z, 512, ZB_GV)], gm_params, 'p' * 7, [(512, BF16)], T, tag + 'gmlp')
    cs = conv_fwd(z, ZB_POOL, _pool_taps(), tag + 'pool_sum')
    (p,) = rows_fwd(f_pool, [(cs, 512, 0), (z, 512, ZB_POOL)], [W['pool_w'], W['pool_scale']], 'pp', [(512, BF16)], T,
                    tag + 'pool')
    (merged,) = rows_fwd(f_merge, [(a, 512, 0), (o, 512, 0), (c, 512, 0), (p, 512, 0), (z, 4 * D, ZB_GATE)],
                         [W['w_branch']], 'w', [(D, BF16)], T, tag + 'merge')
    x1 = matmul(merged, W['w_out'], 'nn', res=x, name=tag + 'w_out')
    (kv,) = rows_fwd(f_kv, [(mem, D, 0)], [W['norm_mem'], W['xa_wkv']], 'pw', [(2 * D, F32)], mem.shape[0],
                     tag + 'kv')
    (x2,) = rows_fwd(f_xattn, [(x1, D, 0)], [kv, W['norm_xa'], W['xa_wq'], W['xa_wo']], 'ppww', [(D, F32)], T,
                     tag + 'xattn')
    (h3,) = rows_fwd(f_norm, [(x2, D, 0)], [W['norm_mlp']], 'p', [(D, BF16)], T, tag + 'norm_mlp')
    pre, r = matmul(h3, W['mlp_w1'], 'nn', act='relu2', name=tag + 'mlp_w1', ride=rides.get('mlp_w1'))
    x3 = matmul(r, W['mlp_w2'], 'nn', res=x2, name=tag + 'mlp_w2')
    sv.update(h1=h1, z=z, a_pre=a_pre, a_cv=a_cv, a=a, qc=qc, kc=kc, vc=vc, ge=ge, be=be, prep=prep, dn_inv=dn_inv, o=o,
              states=states, c=c,
              cs=cs, p=p, merged=merged, x1=x1, kv=kv, x2=x2, h3=h3, pre=pre, r=r)
    return x3, sv


def layer_bwd(dx, mem, W, sv, tag, rides={}):
    S, D = dx.shape
    T = min(S, 256)
    tag = tag + 'b_'
    G = {}
    z = sv['z']
    da = matmul(dx, W['mlp_w2'], 'nt', gate=sv['pre'], out_dtype=BF16, name=tag + 'mlp_da')
    G['mlp_w2'] = matmul(sv['r'], dx, 'tn', name=tag + 'mlp_gw2')
    dh3 = matmul(da, W['mlp_w1'], 'nt', name=tag + 'mlp_dh')
    G['mlp_w1'] = matmul(sv['h3'], da, 'tn', name=tag + 'mlp_gw1')
    (dx2,), (G['norm_mlp'],) = rows_bwd(f_norm_res, [(sv['x2'], D, 0)], [W['norm_mlp']], 'p', [dh3, dx], [F32], T,
                                        tag + 'norm_mlp')
    (dx1,), (dkv, G['norm_xa'], G['xa_wq'], G['xa_wo']) = rows_bwd(
        f_xattn, [(sv['x1'], D, 0)], [sv['kv'], W['norm_xa'], W['xa_wq'], W['xa_wo']], 'ppww', [dx2], [F32],
        T, tag + 'xattn', rides.get('xattn'))
    _, (G['norm_mem'], G['xa_wkv']) = rows_bwd(f_kv, [(mem, D, 0)], [W['norm_mem'], W['xa_wkv']], 'pw', [dkv], [None],
                                               mem.shape[0], tag + 'kv')
    dmerged = matmul(dx1, W['w_out'], 'nt', out_dtype=BF16, name=tag + 'dmerged')
    G['w_out'] = matmul(sv['merged'], dx1, 'tn', name=tag + 'gw_out')
    (d_a, d_o, d_c, d_p, dz_gate), (G['w_branch'],) = rows_bwd(
        f_merge, [(sv['a'], 512, 0), (sv['o'], 512, 0), (sv['c'], 512, 0), (sv['p'], 512, 0), (z, 4 * D, ZB_GATE)],
        [W['w_branch']], 'w', [dmerged], [F32, F32, F32, F32, BF16], T, tag + 'merge', rides.get('merge'))
    (dcs, dpx), (G['pool_w'], G['pool_scale']) = rows_bwd(
        f_pool, [(sv['cs'], 512, 0), (z, 512, ZB_POOL)], [W['pool_w'], W['pool_scale']], 'pp', [d_p], [F32, F32], T,
        tag + 'pool')
    dz_pool, _ = conv_bwd(z, ZB_POOL, _pool_taps(), dcs, tag + 'pool_sum', add=dpx, out_dtype=BF16)
    gm_params = [W['gm_ln_g'], W['gm_ln_b'], W['gm_ws']] + W['gm_b']
    (dz_gu, dz_gv), gm_g = rows_bwd(f_gmlp, [(z, 512, ZB_GU), (z, 512, ZB_GV)], gm_params, 'p' * 7, [d_c],
                                    [BF16, BF16], T, tag + 'gmlp')
    G['gm_ln_g'], G['gm_ln_b'], G['gm_ws'] = gm_g[:3]
    G['gm_b'] = list(gm_g[3:])
    *d_prep, dz_dgate, G['dn_norm_g'] = delta_step_bwd(sv['prep'], z, W['dn_norm_g'], sv['states'], d_o, tag + 'delta')
    dqc, dkc, dvc, dge, dbe = delta_prep_bwd(sv['qc'], sv['kc'], sv['vc'], sv['ge'], sv['be'], sv['dn_inv'], d_prep,
                                             tag + 'delta_prep', rides.get('delta_prep'))
    (dz_bd,), (G['alp'], G['dtp']) = rows_bwd(f_gbeta, [(z, 128, ZB128_BD)], [W['alp'], W['dtp']], 'pp', [dge, dbe],
                                              [BF16], T, tag + 'gbeta')
    dz_q, G['dn_wq'] = conv_bwd(z, ZB_Q, W['dn_wq'], dqc, tag + 'conv_q', out_dtype=BF16)
    dz_k, G['dn_wk'] = conv_bwd(z, ZB_K, W['dn_wk'], dkc, tag + 'conv_k', out_dtype=BF16)
    dz_v, G['dn_wv'] = conv_bwd(z, ZB_V, W['dn_wv'], dvc, tag + 'conv_v', out_dtype=BF16)
    (da_cv,), (G['conv_a_b'], G['ln_a_g'], G['ln_a_b']) = rows_bwd(
        f_lnsilu, [(sv['a_cv'], 512, 0)], [W['conv_a_b'], W['ln_a_g'], W['ln_a_b']], 'ppp', [d_a], [F32], T,
        tag + 'ln_a')
    da_pre, G['conv_a_w'] = conv_bwd(sv['a_pre'], 0, W['conv_a_w'], da_cv, tag + 'conv_a')
    (dz_a1, dz_a2), _ = rows_bwd(f_glu, [(z, 512, ZB_A1), (z, 512, ZB_A2)], [], '', [da_pre], [BF16, BF16], T,
                                 tag + 'glu')
    dz = jnp.concatenate([dz_gate, dz_a1, dz_a2, dz_q, dz_k, dz_v, dz_dgate, dz_gu, dz_gv, dz_pool, dz_bd], axis=1)
    dh1 = matmul(dz, W['w_in'], 'nt', name=tag + 'dh1')
    G['w_in'] = matmul(sv['h1'], dz, 'tn', name=tag + 'gw_in')
    (dx0,), (G['norm_mix'],) = rows_bwd(f_norm_res, [(sv['x'], D, 0)], [W['norm_mix']], 'p', [dh1, dx1], [F32], T,
                                        tag + 'norm_mix')
    return dx0, G


def local_step(x, mem, tgt, norm_f, n_layers, weights_of, fwd_rides=None, bwd_rides=None):
    saved, weights, carried = [], [], {}
    for l in range(n_layers):
        weights.append(weights_of(l, carried))
        carried = fwd_rides(l) if fwd_rides else {}
        x, sv = layer_fwd(x, mem, weights[l], f'l{l}_', carried)
        saved.append(sv)
    loss, dx, g_nf = loss_head(x, tgt, norm_f, min(x.shape[0], 256), 'loss_head')
    grads, rides = [None] * n_layers, {}
    for l in reversed(range(n_layers)):
        dx, grads[l] = layer_bwd(dx, mem, weights[l], saved[l], f'l{l}_', rides)
        rides = bwd_rides(l, grads[l]) if bwd_rides else {}
    return loss, dx, grads, g_nf, rides


def _row(v):
    return v.reshape(1, -1)


def _lane_pad(v):
    return jnp.zeros((8, 128), F32).at[0, :v.shape[0]].set(v)


def layer_weights(full, l):
    w_in = full['w_in'][l]
    cols = [w_in[:, a:b] for a, b in Z_ORDER]
    cols.append(jnp.zeros((w_in.shape[0], Z_W - sum(b - a for a, b in Z_ORDER)), w_in.dtype))
    dn_w = full['dn_conv_w'][l]
    W = {n: _row(full[n][l]) for n in ('norm_mix', 'conv_a_b', 'ln_a_g', 'ln_a_b', 'dn_norm_g', 'gm_ln_g', 'gm_ln_b',
                                       'pool_scale', 'norm_xa', 'norm_mem', 'norm_mlp')}
    W.update(w_in=jnp.concatenate(cols, axis=1), conv_a_w=full['conv_a_w'][l],
             dn_wq=dn_w[:, :MIX_W], dn_wk=dn_w[:, MIX_W:2 * MIX_W], dn_wv=dn_w[:, 2 * MIX_W:],
             alp=_lane_pad(full['dn_a_log'][l]), dtp=_lane_pad(full['dn_dt_bias'][l]),
             gm_ws=full['gm_ws'][l], gm_b=[full['gm_bs'][l][g].reshape(GM_CHUNK, 1) for g in range(GM_GROUPS)],
             pool_w=full['pool_w'][l])
    for n in ('w_branch', 'w_out', 'xa_wq', 'xa_wkv', 'xa_wo', 'mlp_w1', 'mlp_w2'):
        W[n] = full[n][l]
    return W


def layer_grads(G):
    g_in = G['w_in']
    starts = np.cumsum([0] + [b - a for a, b in Z_ORDER])
    pieces = sorted(zip(Z_ORDER, starts[:-1]))
    out = {n: G[n].reshape(-1) for n in ('norm_mix', 'conv_a_b', 'ln_a_g', 'ln_a_b', 'dn_norm_g', 'gm_ln_g', 'gm_ln_b',
                                          'pool_scale', 'norm_xa', 'norm_mem', 'norm_mlp')}
    out.update(w_in=jnp.concatenate([g_in[:, s:s + b - a] for (a, b), s in pieces], axis=1),
               conv_a_w=G['conv_a_w'], dn_conv_w=jnp.concatenate([G['dn_wq'], G['dn_wk'], G['dn_wv']], axis=1),
               dn_a_log=G['alp'][0, :DN_HEADS], dn_dt_bias=G['dtp'][0, :DN_HEADS], gm_ws=G['gm_ws'],
               gm_bs=jnp.stack([b.reshape(-1) for b in G['gm_b']]), pool_w=G['pool_w'])
    for n in ('w_branch', 'w_out', 'xa_wq', 'xa_wkv', 'xa_wo', 'mlp_w1', 'mlp_w2'):
        out[n] = G[n]
    return out


COLUMN_SHARDED = ('w_in', 'conv_a_w', 'dn_conv_w', 'w_branch', 'xa_wkv', 'mlp_w1')


def _unshard(name, g):
    if name in COLUMN_SHARDED:
        t = jnp.moveaxis(g, 0, -2)
        return t.reshape(t.shape[:-2] + (t.shape[-2] * t.shape[-1],))
    t = jnp.moveaxis(g, 0, 1)
    return t.reshape((t.shape[0], t.shape[1] * t.shape[2]) + t.shape[3:])


def _shard_rows(name, g):
    if name in COLUMN_SHARDED:
        t = g.reshape(g.shape[:-1] + (8, g.shape[-1] // 8))
        return jnp.moveaxis(t, -2, 0)
    t = g.reshape((g.shape[0], 8, g.shape[1] // 8) + g.shape[2:])
    return jnp.moveaxis(t, 1, 0)


def _packed_rows(a):
    return -(-a.size // 1024) * 8


def _pack(arrays):
    tiles = [jnp.pad(a.reshape(-1), (0, _packed_rows(a) * 128 - a.size)).reshape(-1, 128) for a in arrays]
    return jnp.concatenate(tiles, axis=0)


def _unpack(packed, like):
    out, at = [], 0
    for a in like:
        rows = _packed_rows(a)
        out.append(packed[at:at + rows].reshape(-1)[:a.size].reshape(a.shape))
        at += rows
    return out


def kernel(x, mem, norm_mix, w_in, conv_a_w, conv_a_b, ln_a_g, ln_a_b, dn_conv_w, dn_a_log, dn_dt_bias, dn_norm_g, gm_ln_g, gm_ln_b, gm_ws, gm_bs, pool_w, pool_scale, w_branch, w_out, norm_xa, norm_mem, xa_wq, xa_wkv, xa_wo, norm_mlp, mlp_w1, mlp_w2, norm_f, loss_target, m_norm_mix, m_w_in, m_conv_a_w, m_conv_a_b, m_ln_a_g, m_ln_a_b, m_dn_conv_w, m_dn_a_log, m_dn_dt_bias, m_dn_norm_g, m_gm_ln_g, m_gm_ln_b, m_gm_ws, m_gm_bs, m_pool_w, m_pool_scale, m_w_branch, m_w_out, m_norm_xa, m_norm_mem, m_xa_wq, m_xa_wkv, m_xa_wo, m_norm_mlp, m_mlp_w1, m_mlp_w2, m_norm_f, v_norm_mix, v_w_in, v_conv_a_w, v_conv_a_b, v_ln_a_g, v_ln_a_b, v_dn_conv_w, v_dn_a_log, v_dn_dt_bias, v_dn_norm_g, v_gm_ln_g, v_gm_ln_b, v_gm_ws, v_gm_bs, v_pool_w, v_pool_scale, v_w_branch, v_w_out, v_norm_xa, v_norm_mem, v_xa_wq, v_xa_wkv, v_xa_wo, v_norm_mlp, v_mlp_w1, v_mlp_w2, v_norm_f):
    args = locals()
    w = {n: args[n] for n in WEIGHTS}
    m = {n: args['m_' + n] for n in WEIGHTS}
    v = {n: args['v_' + n] for n in WEIGHTS}

    px, py, pc = _place()
    me = 4 * px + 2 * py + pc

    fwd_share = {'w_in': ['w_in'], 'mlp_w1': ['mlp_w1', 'mlp_w2'],
                 'delta_prep': [n for n in SHARDED if n not in ('w_in', 'mlp_w1', 'mlp_w2')]}
    bwd_share = {'xattn': ['w_in'], 'merge': ['mlp_w1', 'mlp_w2'],
                 'delta_prep': [n for n in SHARDED if n not in ('w_in', 'mlp_w1', 'mlp_w2')]}
    assert set(fwd_share) <= set(FWD_HOSTS) and set(bwd_share) <= set(BWD_HOSTS)

    def shards_of(l):
        return {n: w[n][l].astype(BF16) if n in SENT_AS_BF16 else w[n][l] for n in SHARDED}

    def fwd_rides(l):
        if l + 1 == DEPTH:
            return {}
        shards = shards_of(l + 1)
        return {host: gather_ride([shards[n] for n in names]) for host, names in fwd_share.items()}

    def weights_of(l, carried):
        shards = shards_of(l)
        if carried:
            gathered = {n: g for host, names in fwd_share.items() for n, g in zip(names, carried[host].results)}
        else:
            gathered = dict(zip(SHARDED, all_gather([shards[n] for n in SHARDED], 'gather_weights')))
        full = {n: w[n][l:l + 1] for n in REPLICATED if n != 'norm_f'}
        for n in SHARDED:
            full[n] = _unshard(n, lax.dynamic_update_index_in_dim(gathered[n], shards[n], me, 0)[:, None])
        return layer_weights(full, 0)

    per_layer, sums, chips = [None] * DEPTH, [None] * DEPTH, [None] * DEPTH

    def bwd_rides(l, G):
        per_layer[l] = layer_grads(G)
        rows = [_as2d(_shard_rows(n, per_layer[l][n][None]), 1).astype(BF16 if n in SENT_AS_BF16 else F32)
                for n in SHARDED]
        from_sibling = exchange_cores(rows, f'reduce_cores_l{l}')
        sums[l] = {n: add_own_rows(g, pc, r, f'reduce_add_l{l}_{n}') for n, g, r in zip(SHARDED, rows, from_sibling)}
        chips[l] = {host: chips_ride([sums[l][n] for n in names]) for host, names in bwd_share.items()}
        return chips[l]

    loss, grad_x, grads, g_nf, last = local_step(x[0], mem[0], loss_target[0], _row(norm_f), DEPTH, weights_of,
                                                 fwd_rides, bwd_rides)
    for host, ride in last.items():
        run_alone(ride, 'reduce_chips_l0_' + host)
    from_chips = [{n: r for host, names in bwd_share.items() for n, r in zip(names, chips[l][host].results)}
                  for l in range(DEPTH)]
    gfull = {n: jnp.stack([g[n] for g in per_layer]) for n in REPLICATED if n != 'norm_f'}
    gfull['norm_f'] = g_nf.reshape(-1)

    out = {}
    for n in SHARDED:
        own = jnp.concatenate([lax.dynamic_index_in_dim(sums[l][n], 2 * px + py, 0) for l in range(DEPTH)], axis=1)
        parts = jnp.concatenate([from_chips[l][n] for l in range(DEPTH)], axis=1)
        res = adamw_sum(parts, _as2d(w[n]), _as2d(m[n]), _as2d(v[n]), 'adamw_' + n, own=own, own_row=0)
        out[n] = [r.reshape(w[n].shape) for r in res]
    packed = _pack([gfull[n] for n in REPLICATED])
    (partials,) = all_gather([packed], 'gather_small_grads')
    partials = lax.dynamic_update_index_in_dim(partials, packed, 4 * px + 2 * py + pc, 0)
    res = adamw_sum(partials, _pack([w[n] for n in REPLICATED]), _pack([m[n] for n in REPLICATED]),
                    _pack([v[n] for n in REPLICATED]), 'adamw_small')
    like = [w[n] for n in REPLICATED]
    for k, r in enumerate(res):
        for n, a in zip(REPLICATED, _unpack(r, like)):
            out.setdefault(n, [None] * 4)[k] = a

    total = lax.psum(loss[0, 0], ('x', 'y', 'c'))
    return (total, grad_x[None], *[out[n][0] for n in WEIGHTS], *[out[n][1] for n in WEIGHTS],
            *[out[n][2] for n in WEIGHTS], *[out[n][3] for n in WEIGHTS])
```
